```python
import math
import jax, jax.numpy as jnp
from jax import lax
import numpy as np

D_MODEL = 1024
BATCH = 32
SEQ = 256
DEPTH = 2
DEC_BATCH = 8
DEC_SEQ = 1024
PAST_LEN = 256

GRID_W = 64
ROPE_BASE = 10000.0
Q_BLOCK = 128
EPS = 1e-6
H_A = 6
DN_A = 64
DR_A = 32
DV_A = 64
Q_RANK = 384
KV_RANK = 256
H_B = 4
DH_B = 32
DV_B = 64
H_C = 4
DK_C = 48
DV_C = 96
GATE_RANK = 16
GATE_NORM = 16.0
CHUNK = 64
WIDTH_A = H_A * DV_A
WIDTH_B = H_B * DV_B
WIDTH_C = H_C * DV_C
MIX_WIDTH = WIDTH_A + WIDTH_B + WIDTH_C
D_FF = 2816
N_MOD = 9
PROJ_SPLITS = (Q_RANK, KV_RANK, DR_A, H_B * 2 * DH_B, H_B * 2 * DH_B, H_B * DV_B,
               H_C * DK_C, H_C * DK_C, H_C * DV_C, WIDTH_C, GATE_RANK, GATE_RANK)
PROJ_COLS = sum(PROJ_SPLITS)

kernel_name = "hybrid_diffusion_mla_diff_gla_step"


def rmsnorm(x, g):
    xf = x.astype(jnp.float32)
    y = xf * lax.rsqrt(jnp.mean(xf * xf, axis=-1, keepdims=True) + EPS)
    return (y * g.astype(jnp.float32)).astype(x.dtype)


def modulate_norm(x, g, shift, scale):
    return rmsnorm(x, g) * (1 + scale) + shift


def swiglu(u, w13, w2):
    a, b = jnp.split(u @ w13, 2, axis=-1)
    return (jax.nn.silu(a) * b) @ w2


def grid_positions(n):
    n_rows = n // GRID_W
    rows = jnp.repeat(jnp.arange(n_rows), GRID_W)
    cols = jnp.arange(n_rows * GRID_W) % GRID_W
    return rows, cols


def rope_1d(x, pos):
    d = x.shape[-1]
    inv = ROPE_BASE ** (-jnp.arange(0, d, 2, dtype=jnp.float32) / d)
    ang = pos.astype(jnp.float32)[:, None] * inv
    shape = (ang.shape[0],) + (1,) * (x.ndim - 3) + (d // 2,)
    cos = jnp.cos(ang).reshape(shape).astype(x.dtype)
    sin = jnp.sin(ang).reshape(shape).astype(x.dtype)
    x1, x2 = x[..., : d // 2], x[..., d // 2:]
    return jnp.concatenate([x1 * cos - x2 * sin, x1 * sin + x2 * cos], axis=-1)


def axial_rope(x):
    rows, cols = grid_positions(x.shape[1])
    h = x.shape[-1] // 2
    return jnp.concatenate([rope_1d(x[..., :h], rows), rope_1d(x[..., h:], cols)], axis=-1)


def query_blocks(fn, qs):
    b, nq = qs[0].shape[:2]
    nb = nq // Q_BLOCK
    qb = tuple(a.reshape((b, nb, Q_BLOCK) + a.shape[2:]).swapaxes(0, 1) for a in qs)
    out = lax.map(lambda t: fn(*t), qb)
    return out.swapaxes(0, 1).reshape((b, nq) + out.shape[3:])


def mla_attend(q_nope, q_rope, ckv, krope, lp):
    k_nope = jnp.einsum("bkr,rhd->bkhd", ckv, lp["mla_w_uk"].reshape(KV_RANK, H_A, DN_A))
    v = jnp.einsum("bkr,rhd->bkhd", ckv, lp["mla_w_uv"].reshape(KV_RANK, H_A, DV_A))
    scale = (DN_A + DR_A) ** -0.5

    def blk(qn, qr):
        s = jnp.einsum("bqhd,bkhd->bhqk", qn, k_nope) + jnp.einsum("bqhd,bkd->bhqk", qr, krope)
        p = jax.nn.softmax(s.astype(jnp.float32) * scale, axis=-1).astype(v.dtype)
        return jnp.einsum("bhqk,bkhd->bqhd", p, v)

    return query_blocks(blk, (q_nope, q_rope))


def diff_attend(q, k, v, lp, lam_init):
    lam = (jnp.exp(jnp.sum(lp["diff_lq1"].astype(jnp.float32) * lp["diff_lk1"].astype(jnp.float32)))
           - jnp.exp(jnp.sum(lp["diff_lq2"].astype(jnp.float32) * lp["diff_lk2"].astype(jnp.float32)))
           + lam_init)
    scale = DH_B ** -0.5

    def blk(qb):
        s = jnp.einsum("bqhjd,bkhjd->bhjqk", qb, k).astype(jnp.float32) * scale
        p = jax.nn.softmax(s, axis=-1)
        a = p[:, :, 0] - lam * p[:, :, 1]
        return jnp.einsum("bhqk,bkhd->bqhd", a.astype(v.dtype), v)

    return query_blocks(blk, (q,))


def gla_chunk_scan(q, k, v, g, s0):
    b_, n, h, _ = q.shape
    nc = n // CHUNK

    def chunks(a):
        return a.reshape(b_, nc, CHUNK, h, a.shape[-1]).transpose(1, 0, 3, 2, 4).astype(jnp.float32)

    qc, kc, vc, gc = chunks(q), chunks(k), chunks(v), chunks(g)
    bcum = jnp.cumsum(gc, axis=-2)
    blast = bcum[..., -1:, :]
    qe = qc * jnp.exp(bcum)
    ke = kc * jnp.exp(-bcum)
    kd = kc * jnp.exp(blast - bcum)
    mask = jnp.tril(jnp.ones((CHUNK, CHUNK), jnp.float32))
    attn = jnp.einsum("nbhtd,nbhsd->nbhts", qe, ke) * mask
    intra = jnp.einsum("nbhts,nbhsv->nbhtv", attn, vc)
    upd = jnp.einsum("nbhsd,nbhsv->nbhdv", kd, vc)
    decay = jnp.exp(blast[..., 0, :])

    def step(s, inp):
        dec, u = inp
        return dec[..., None] * s + u, s

    s_fin, s_in = lax.scan(step, s0.astype(jnp.float32), (decay, upd))
    o = intra + jnp.einsum("nbhtd,nbhdv->nbhtv", qe, s_in)
    o = o.transpose(1, 0, 3, 2, 4).reshape(b_, n, h, v.shape[-1])
    return o.astype(v.dtype), s_fin.astype(q.dtype)


def gla_bidir(q, k, v, g_f, g_b, s_f0, s_b0):
    o_f, s_f = gla_chunk_scan(q, k, v, g_f, s_f0)
    flip = lambda a: a[:, ::-1]
    o_b, s_b = gla_chunk_scan(flip(q), flip(k), flip(v), flip(g_b), s_b0)
    return o_f + flip(o_b), s_f, s_b


def project(u, lp):
    b_, n, _ = u.shape
    offs = np.cumsum(PROJ_SPLITS)[:-1].tolist()
    (cq, ckv, krope, qb, kb, vb, qc, kc, vc, rc, glf, glb) = jnp.split(u @ lp["w_in"], offs, axis=-1)
    q_a = (rmsnorm(cq, lp["mla_q_norm"]) @ lp["mla_w_uq"]).reshape(b_, n, H_A, DN_A + DR_A)
    g_f = jax.nn.log_sigmoid((glf @ lp["gla_wg_f"] + lp["gla_bg_f"]).astype(jnp.float32)) / GATE_NORM
    g_b = jax.nn.log_sigmoid((glb @ lp["gla_wg_b"] + lp["gla_bg_b"]).astype(jnp.float32)) / GATE_NORM
    return {
        "q_nope": q_a[..., :DN_A], "q_rope": q_a[..., DN_A:],
        "ckv": rmsnorm(ckv, lp["mla_kv_norm"]), "krope": krope,
        "q_b": qb.reshape(b_, n, H_B, 2, DH_B), "k_b": kb.reshape(b_, n, H_B, 2, DH_B),
        "v_b": vb.reshape(b_, n, H_B, DV_B),
        "q_c": qc.reshape(b_, n, H_C, DK_C) * (DK_C ** -0.5), "k_c": kc.reshape(b_, n, H_C, DK_C),
        "v_c": vc.reshape(b_, n, H_C, DV_C), "r_c": rc,
        "g_f": g_f.reshape(b_, n, H_C, DK_C), "g_b": g_b.reshape(b_, n, H_C, DK_C),
    }


def merge_groups(o_a, o_b, o_c, r_c, lp, lam_init):
    b_, n = o_a.shape[:2]
    o_a = rmsnorm(o_a.reshape(b_, n, WIDTH_A), lp["mla_out_norm"])
    o_b = (rmsnorm(o_b, lp["diff_norm"]) * (1.0 - lam_init)).reshape(b_, n, WIDTH_B)
    o_c = rmsnorm(o_c, lp["gla_norm"]).reshape(b_, n, WIDTH_C) * jax.nn.silu(r_c)
    return jnp.concatenate([o_a, o_b, o_c], axis=-1) @ lp["w_out"]


def context_mixer(u, lp, lam_init):
    p = project(u, lp)
    b_, n = u.shape[:2]
    o_a = mla_attend(p["q_nope"], p["q_rope"], p["ckv"], p["krope"], lp)
    o_b = diff_attend(p["q_b"], p["k_b"], p["v_b"], lp, lam_init)
    zeros = jnp.zeros((b_, H_C, DK_C, DV_C), u.dtype)
    o_c, s_f, s_b = gla_bidir(p["q_c"], p["k_c"], p["v_c"], p["g_f"], p["g_b"], zeros, zeros)
    out = merge_groups(o_a, o_b, o_c, p["r_c"], lp, lam_init)
    ctx = (p["ckv"], p["krope"], p["k_b"].reshape(b_, n, H_B, 2 * DH_B), p["v_b"], s_f, s_b)
    return out, ctx


def latent_mixer(u, lp, lam_init, ctx):
    ckv_c, krope_c, kb_c, vb_c, sf_c, sb_c = ctx
    p = project(u, lp)
    q_rope = axial_rope(p["q_rope"])
    krope = axial_rope(p["krope"][:, :, None, :])[:, :, 0]
    o_a = mla_attend(p["q_nope"], q_rope, jnp.concatenate([ckv_c, p["ckv"]], axis=1),
                     jnp.concatenate([krope_c, krope], axis=1), lp)
    k_all = jnp.concatenate([kb_c.reshape(kb_c.shape[:3] + (2, DH_B)), axial_rope(p["k_b"])], axis=1)
    v_all = jnp.concatenate([vb_c, p["v_b"]], axis=1)
    o_b = diff_attend(axial_rope(p["q_b"]), k_all, v_all, lp, lam_init)
    o_c, _, _ = gla_bidir(p["q_c"], p["k_c"], p["v_c"], p["g_f"], p["g_b"], sf_c, sb_c)
    return merge_groups(o_a, o_b, o_c, p["r_c"], lp, lam_init), None


def trunk_layer(x, mod, lp, mixer_fn):
    sh1, sc1, g1, sh2, sc2, g2, sh3, sc3, g3 = jnp.split(mod, N_MOD, axis=-1)
    x = x + 0.5 * g1 * swiglu(modulate_norm(x, lp["norm_ffn1"], sh1, sc1), lp["ffn1_w13"], lp["ffn1_w2"])
    mixed, aux = mixer_fn(modulate_norm(x, lp["norm_mix"], sh2, sc2))
    x = x + g2 * mixed
    x = x + 0.5 * g3 * swiglu(modulate_norm(x, lp["norm_ffn2"], sh3, sc3), lp["ffn2_w13"], lp["ffn2_w2"])
    return x, aux


def setup_inputs(seed: int = 0) -> dict:
    key = jax.random.key(seed)
    ks = iter(jax.random.split(key, 48))

    def nrm(shape, scale):
        return jax.random.normal(next(ks), shape, jnp.float32) * scale

    def gain(shape):
        return 1.0 + nrm(shape, 0.05)

    L, D = DEPTH, D_MODEL
    return {
        "x_prompt": nrm((BATCH, SEQ, D), 1.0),
        "x_sample": nrm((DEC_BATCH, DEC_SEQ, D), 1.0),
        "cache_mla_ckv": nrm((DEC_BATCH, L, PAST_LEN, KV_RANK), 1.0),
        "cache_mla_krope": nrm((DEC_BATCH, L, PAST_LEN, DR_A), 1.0),
        "cache_diff_k": nrm((DEC_BATCH, L, PAST_LEN, H_B, 2 * DH_B), 1.0),
        "cache_diff_v": nrm((DEC_BATCH, L, PAST_LEN, H_B, DV_B), 1.0),
        "state_gla_fwd": nrm((DEC_BATCH, L, H_C, DK_C, DV_C), 0.5),
        "state_gla_bwd": nrm((DEC_BATCH, L, H_C, DK_C, DV_C), 0.5),
        "c": nrm((DEC_BATCH, D), 1.0),
        "c_ctx": nrm((D,), 1.0),
        "w_mod": nrm((L, D, N_MOD * D), 0.5 * D ** -0.5),
        "b_mod": nrm((L, N_MOD * D), 0.02),
        "norm_ffn1": gain((L, D)),
        "ffn1_w13": nrm((L, D, 2 * D_FF), D ** -0.5),
        "ffn1_w2": nrm((L, D_FF, D), D_FF ** -0.5),
        "norm_mix": gain((L, D)),
        "w_in": nrm((L, D, PROJ_COLS), D ** -0.5),
        "mla_q_norm": gain((L, Q_RANK)),
        "mla_w_uq": nrm((L, Q_RANK, H_A * (DN_A + DR_A)), Q_RANK ** -0.5),
        "mla_kv_norm": gain((L, KV_RANK)),
        "mla_w_uk": nrm((L, KV_RANK, H_A * DN_A), KV_RANK ** -0.5),
        "mla_w_uv": nrm((L, KV_RANK, H_A * DV_A), KV_RANK ** -0.5),
        "mla_out_norm": gain((L, WIDTH_A)),
        "diff_lq1": nrm((L, DH_B), 0.1),
        "diff_lk1": nrm((L, DH_B), 0.1),
        "diff_lq2": nrm((L, DH_B), 0.1),
        "diff_lk2": nrm((L, DH_B), 0.1),
        "diff_norm": gain((L, DV_B)),
        "gla_wg_f": nrm((L, GATE_RANK, H_C * DK_C), GATE_RANK ** -0.5),
        "gla_bg_f": nrm((L, H_C * DK_C), 0.1),
        "gla_wg_b": nrm((L, GATE_RANK, H_C * DK_C), GATE_RANK ** -0.5),
        "gla_bg_b": nrm((L, H_C * DK_C), 0.1),
        "gla_norm": gain((L, DV_C)),
        "w_out": nrm((L, MIX_WIDTH, D), MIX_WIDTH ** -0.5),
        "norm_ffn2": gain((L, D)),
        "ffn2_w13": nrm((L, D, 2 * D_FF), D ** -0.5),
        "ffn2_w2": nrm((L, D_FF, D), D_FF ** -0.5),
        "final_norm": gain((D,)),
    }


def reference(x_prompt, x_sample, cache_mla_ckv, cache_mla_krope, cache_diff_k, cache_diff_v,
              state_gla_fwd, state_gla_bwd, c, c_ctx, w_mod, b_mod, norm_ffn1, ffn1_w13, ffn1_w2,
              norm_mix, w_in, mla_q_norm, mla_w_uq, mla_kv_norm, mla_w_uk, mla_w_uv, mla_out_norm,
              diff_lq1, diff_lk1, diff_lq2, diff_lk2, diff_norm, gla_wg_f, gla_bg_f, gla_wg_b, gla_bg_b,
              gla_norm, w_out, norm_ffn2, ffn2_w13, ffn2_w2, final_norm):
    xp, xs = x_prompt, x_sample
    new_ckv, new_krope, new_dk, new_dv, new_sf, new_sb = [], [], [], [], [], []
    for l in range(DEPTH):
        lp = {
            "norm_ffn1": norm_ffn1[l], "ffn1_w13": ffn1_w13[l], "ffn1_w2": ffn1_w2[l],
            "norm_mix": norm_mix[l], "w_in": w_in[l],
            "mla_q_norm": mla_q_norm[l], "mla_w_uq": mla_w_uq[l], "mla_kv_norm": mla_kv_norm[l],
            "mla_w_uk": mla_w_uk[l], "mla_w_uv": mla_w_uv[l], "mla_out_norm": mla_out_norm[l],
            "diff_lq1": diff_lq1[l], "diff_lk1": diff_lk1[l], "diff_lq2": diff_lq2[l],
            "diff_lk2": diff_lk2[l], "diff_norm": diff_norm[l],
            "gla_wg_f": gla_wg_f[l], "gla_bg_f": gla_bg_f[l], "gla_wg_b": gla_wg_b[l],
            "gla_bg_b": gla_bg_b[l], "gla_norm": gla_norm[l], "w_out": w_out[l],
            "norm_ffn2": norm_ffn2[l], "ffn2_w13": ffn2_w13[l], "ffn2_w2": ffn2_w2[l],
        }
        lam_init = 0.8 - 0.6 * math.exp(-0.3 * l)
        mod_ctx = (jax.nn.silu(c_ctx) @ w_mod[l] + b_mod[l])[None, None, :]
        mod_lat = (jax.nn.silu(c) @ w_mod[l] + b_mod[l])[:, None, :]
        xp, ctx_t = trunk_layer(xp, mod_ctx, lp, lambda u: context_mixer(u, lp, lam_init))
        new_ckv.append(ctx_t[0]); new_krope.append(ctx_t[1]); new_dk.append(ctx_t[2])
        new_dv.append(ctx_t[3]); new_sf.append(ctx_t[4]); new_sb.append(ctx_t[5])
        cached = (cache_mla_ckv[:, l], cache_mla_krope[:, l], cache_diff_k[:, l], cache_diff_v[:, l],
                  state_gla_fwd[:, l], state_gla_bwd[:, l])
        xs, _ = trunk_layer(xs, mod_lat, lp, lambda u: latent_mixer(u, lp, lam_init, cached))
    y_prompt = rmsnorm(xp, final_norm)
    y_sample = rmsnorm(xs, final_norm)
    return (y_prompt, y_sample, jnp.stack(new_ckv, axis=1), jnp.stack(new_krope, axis=1),
            jnp.stack(new_dk, axis=1), jnp.stack(new_dv, axis=1), jnp.stack(new_sf, axis=1),
            jnp.stack(new_sb, axis=1))
```

```python
import functools
import math

import jax
import jax.numpy as jnp
from jax import lax
from jax.experimental import pallas as pl
from jax.experimental.pallas import tpu as pltpu

F32 = jnp.float32
BF16 = jnp.bfloat16

D_MODEL = 1024
DEPTH = 2
GRID_W = 64
ROPE_BASE = 10000.0
EPS = 1e-6
H_A, DN_A, DR_A, DV_A = 6, 64, 32, 64
Q_RANK, KV_RANK = 384, 256
H_B, DH_B, DV_B = 4, 32, 64
H_C, DK_C, DV_C = 4, 48, 96
GATE_RANK = 16
GATE_NORM = 16.0
CHUNK = 64
WIDTH_A = H_A * DV_A
WIDTH_B = H_B * DV_B
WIDTH_C = H_C * DV_C
D_FF = 2816
N_MOD = 9

LANE = 128
FF_CHUNK = 256
TOKEN_TILE = 512
Q_TILE = 256
MOD_TILE = 1152
VMEM_LIMIT = 56 * 1024 * 1024

HEAD_PAD = LANE
ROPE_OFF = DN_A
QK_C_PAD = 256
SEG = {}
_off = 0
for _name, _w in (("cq", Q_RANK), ("ckv", KV_RANK), ("krope", LANE), ("qb", 256), ("kb", 256),
                  ("vb", 256), ("qc", QK_C_PAD), ("kc", QK_C_PAD), ("vc", WIDTH_C), ("rc", WIDTH_C),
                  ("gl", LANE)):
    SEG[_name] = (_off, _off + _w)
    _off += _w
PROJ_PAD = _off


def _dot(a, b):
    return jnp.dot(a, b, preferred_element_type=F32)


def _dot_nt(a, b):
    return lax.dot_general(a, b, (((1,), (1,)), ((), ())), preferred_element_type=F32)


def _dot_tn(a, b):
    return lax.dot_general(a, b, (((0,), (0,)), ((), ())), preferred_element_type=F32)


def _dot_f32(a, b):
    return jnp.dot(a, b, preferred_element_type=F32, precision=lax.Precision.HIGHEST)


def _rms(x, gain):
    ms = jnp.mean(x * x, axis=-1, keepdims=True)
    return (x * lax.rsqrt(ms + EPS)) * gain


def _silu(x):
    return x * jax.nn.sigmoid(x)


def _const_spec(shape):
    nd = len(shape)
    return pl.BlockSpec(shape, lambda *_: (0,) * nd)


def _params(n_axes):
    return pltpu.CompilerParams(dimension_semantics=("arbitrary",) * n_axes,
                                vmem_limit_bytes=VMEM_LIMIT)


def _mod_kernel(c_ref, w_ref, b_ref, o_ref):
    s = _silu(c_ref[...]).astype(BF16)
    o_ref[0] = _dot(s, w_ref[0].astype(BF16)) + b_ref[0]


def _modulation(cvec, w_mod, b_mod):
    n_rows = cvec.shape[0]
    width = N_MOD * D_MODEL
    return pl.pallas_call(
        _mod_kernel,
        grid=(DEPTH, width // MOD_TILE),
        in_specs=[
            pl.BlockSpec((n_rows, D_MODEL), lambda l, j: (0, 0)),
            pl.BlockSpec((1, D_MODEL, MOD_TILE), lambda l, j: (l, 0, j)),
            pl.BlockSpec((1, 1, MOD_TILE), lambda l, j: (l, 0, j)),
        ],
        out_specs=pl.BlockSpec((1, n_rows, MOD_TILE), lambda l, j: (l, 0, j)),
        out_shape=jax.ShapeDtypeStruct((DEPTH, n_rows, width), F32),
        compiler_params=_params(2),
    )(cvec, w_mod, b_mod.reshape(DEPTH, 1, width))


def _mod_spec(tokens_per_group):
    return pl.BlockSpec((1, N_MOD, D_MODEL), lambda i: ((i * TOKEN_TILE) // tokens_per_group, 0, 0))


def _ada_norm(x, gain, mod_ref, first_row):
    shift = mod_ref[0, first_row:first_row + 1, :]
    scale = mod_ref[0, first_row + 1:first_row + 2, :]
    return _rms(x, gain) * (1.0 + scale) + shift


def _ffn_kernel(x_ref, mod_ref, gain_ref, w13_ref, w2_ref, fin_ref, o_ref, acc_ref, *, mod_row, final):
    x = x_ref[...]
    u = _ada_norm(x, gain_ref[...], mod_ref, mod_row).astype(BF16)
    for c in range(D_FF // FF_CHUNK):
        lo, hi = c * FF_CHUNK, (c + 1) * FF_CHUNK
        a = _dot(u, w13_ref[:, lo:hi])
        b = _dot(u, w13_ref[:, D_FF + lo:D_FF + hi])
        t = _dot((_silu(a) * b).astype(BF16), w2_ref[lo:hi, :])
        if c == 0:
            acc_ref[...] = t
        else:
            acc_ref[...] += t
    gate = mod_ref[0, mod_row + 2:mod_row + 3, :]
    y = x + (0.5 * gate) * acc_ref[...]
    if final:
        y = _rms(y, fin_ref[...])
    o_ref[...] = y


def _ffn(x, mod, tokens_per_group, gain, w13, w2, fin, *, mod_row, final):
    n = x.shape[0]
    tile = pl.BlockSpec((TOKEN_TILE, D_MODEL), lambda i: (i, 0))
    return pl.pallas_call(
        functools.partial(_ffn_kernel, mod_row=mod_row, final=final),
        grid=(n // TOKEN_TILE,),
        in_specs=[tile, _mod_spec(tokens_per_group), _const_spec((1, D_MODEL)),
                  _const_spec((D_MODEL, 2 * D_FF)), _const_spec((D_FF, D_MODEL)),
                  _const_spec((1, D_MODEL))],
        out_specs=tile,
        out_shape=jax.ShapeDtypeStruct((n, D_MODEL), F32),
        scratch_shapes=[pltpu.VMEM((TOKEN_TILE, D_MODEL), F32)],
        compiler_params=_params(1),
    )(x, mod, gain, w13, w2, fin)


def _rope(x, cos, sin_signed):
    first = (lax.broadcasted_iota(jnp.int32, (x.shape[0], LANE), 1) % 16) < 8
    outs = []
    for j in range(x.shape[1] // LANE):
        xj = x[:, j * LANE:(j + 1) * LANE]
        partner = jnp.where(first, pltpu.roll(xj, LANE - 8, 1), pltpu.roll(xj, 8, 1))
        outs.append(xj * cos + partner * sin_signed)
    return outs[0] if len(outs) == 1 else jnp.concatenate(outs, axis=1)


def _log_sigmoid(z):
    return jnp.minimum(z, 0.0) - jnp.log1p(jnp.exp(-jnp.abs(z)))


def _proj_kernel(*refs, rope):
    (x_ref, mod_ref, gain_ref, wp_ref, qn_ref, wuq_ref, kvn_ref, wg_ref, bg_ref) = refs[:9]
    refs = refs[9:]
    if rope:
        ch_ref, sh_ref, cf_ref, sf_ref = refs[:4]
        refs = refs[4:]
    (q_o, ckv_o, kr_o, qb_o, kb_o, vb_o, qc_o, kc_o, vc_o, rc_o, gf_o, gb_o) = refs

    u = _ada_norm(x_ref[...], gain_ref[...], mod_ref, 3).astype(BF16)

    def seg(name):
        lo, hi = SEG[name]
        return _dot(u, wp_ref[:, lo:hi])

    q = _dot(_rms(seg("cq"), qn_ref[...]).astype(BF16), wuq_ref[...])
    kr = seg("krope")
    qb = seg("qb")
    kb = seg("kb")
    if rope:
        q = _rope(q, ch_ref[...], sh_ref[...])
        kr = _rope(kr, ch_ref[...], sh_ref[...])
        qb = _rope(qb, cf_ref[...], sf_ref[...])
        kb = _rope(kb, cf_ref[...], sf_ref[...])
    q_o[...] = q.astype(BF16)
    ckv_o[...] = _rms(seg("ckv"), kvn_ref[...])
    kr_o[...] = kr
    qb_o[...] = qb.astype(BF16)
    kb_o[...] = kb
    vb_o[...] = seg("vb")
    qc_o[...] = seg("qc") * (DK_C ** -0.5)
    kc_o[...] = seg("kc")
    vc_o[...] = seg("vc")
    rc_o[...] = seg("rc")
    z = _dot(seg("gl").astype(BF16), wg_ref[...]) + bg_ref[...]
    g = _log_sigmoid(z) / GATE_NORM
    gf_o[...] = g[:, :QK_C_PAD]
    gb_o[...] = g[:, QK_C_PAD:]


PROJ_OUT = (("q", H_A * HEAD_PAD, BF16), ("ckv", KV_RANK, F32), ("kr", LANE, F32), ("qb", 256, BF16),
            ("kb", 256, F32), ("vb", 256, F32), ("qc", QK_C_PAD, F32), ("kc", QK_C_PAD, F32),
            ("vc", WIDTH_C, F32), ("rc", WIDTH_C, F32), ("gf", QK_C_PAD, F32), ("gb", QK_C_PAD, F32))


def _project(x, mod, tokens_per_group, lw, rope_tabs):
    n = x.shape[0]
    rope = rope_tabs is not None
    row = lambda w: pl.BlockSpec((TOKEN_TILE, w), lambda i: (i, 0))
    in_specs = [row(D_MODEL), _mod_spec(tokens_per_group), _const_spec((1, D_MODEL)),
                _const_spec((D_MODEL, PROJ_PAD)), _const_spec((1, Q_RANK)),
                _const_spec((Q_RANK, H_A * HEAD_PAD)), _const_spec((1, KV_RANK)),
                _const_spec((LANE, 2 * QK_C_PAD)), _const_spec((1, 2 * QK_C_PAD))]
    args = [x, mod, lw["norm_mix"], lw["wp"], lw["q_norm"], lw["wuq"], lw["kv_norm"], lw["wg"], lw["bg"]]
    if rope:
        n_pos = rope_tabs[0].shape[0]
        tab = pl.BlockSpec((TOKEN_TILE, LANE), lambda i: (i % (n_pos // TOKEN_TILE), 0))
        in_specs += [tab] * 4
        args += list(rope_tabs)
    outs = pl.pallas_call(
        functools.partial(_proj_kernel, rope=rope),
        grid=(n // TOKEN_TILE,),
        in_specs=in_specs,
        out_specs=[row(w) for _, w, _ in PROJ_OUT],
        out_shape=[jax.ShapeDtypeStruct((n, w), dt) for _, w, dt in PROJ_OUT],
        compiler_params=_params(1),
    )(*args)
    return {name: o for (name, _, _), o in zip(PROJ_OUT, outs)}


def _lane_mask(width, lo, hi):
    lane = lax.broadcasted_iota(jnp.int32, (1, width), 1)
    return jnp.where((lane >= lo) & (lane < hi), 1.0, 0.0).astype(F32)


def _softmax_parts(s):
    m = jnp.max(s, axis=-1, keepdims=True)
    e = jnp.exp(s - m)
    return e, jnp.sum(e, axis=-1, keepdims=True)


def _mla_kernel(*refs, n_cache):
    if n_cache:
        q_ref, ckv_ref, kr_ref, cckv_ref, ckr_ref, wuk_ref, wuv_ref, o_ref, kf_scr, v_scr = refs
    else:
        q_ref, ckv_ref, kr_ref, wuk_ref, wuv_ref, o_ref, kf_scr, v_scr = refs

    @pl.when(pl.program_id(1) == 0)
    def _():
        def fill(row0, ckv, kr):
            rows = ckv.shape[0]
            cb = ckv.astype(BF16)
            kr_all = jnp.concatenate([kr] * H_A, axis=1)
            kf_scr[row0:row0 + rows, :] = (_dot(cb, wuk_ref[...]) + kr_all).astype(BF16)
            v_scr[row0:row0 + rows, :] = _dot(cb, wuv_ref[...]).astype(BF16)

        if n_cache:
            fill(0, cckv_ref[...], ckr_ref[...])
        fill(n_cache, ckv_ref[...], kr_ref[...])

    scale = (DN_A + DR_A) ** -0.5
    heads = []
    for h in range(H_A):
        sl = slice(h * HEAD_PAD, (h + 1) * HEAD_PAD)
        s = _dot_nt(q_ref[:, sl], kf_scr[:, sl]) * scale
        e, l = _softmax_parts(s)
        heads.append(_dot(e.astype(BF16), v_scr[:, sl]) * (1.0 / l))
    o_ref[...] = jnp.concatenate([heads[2 * j] + heads[2 * j + 1] for j in range(H_A // 2)], axis=1)


def _mla(p, lw, n_batch, n_tok, cache, layer):
    n_cache = 0 if cache is None else cache[0].shape[2]
    nqt = n_tok // Q_TILE
    in_specs = [pl.BlockSpec((Q_TILE, H_A * HEAD_PAD), lambda b, i: (b * nqt + i, 0)),
                pl.BlockSpec((n_tok, KV_RANK), lambda b, i: (b, 0)),
                pl.BlockSpec((n_tok, LANE), lambda b, i: (b, 0))]
    args = [p["q"], p["ckv"], p["kr"]]
    if n_cache:
        in_specs += [pl.BlockSpec((None, None, n_cache, KV_RANK), lambda b, i: (b, layer, 0, 0)),
                     pl.BlockSpec((None, None, n_cache, LANE), lambda b, i: (b, layer, 0, 0))]
        args += list(cache)
    in_specs += [_const_spec((KV_RANK, H_A * HEAD_PAD))] * 2
    args += [lw["wuk"], lw["wuv"]]
    n_keys = n_cache + n_tok
    return pl.pallas_call(
        functools.partial(_mla_kernel, n_cache=n_cache),
        grid=(n_batch, nqt),
        in_specs=in_specs,
        out_specs=pl.BlockSpec((Q_TILE, WIDTH_A), lambda b, i: (b * nqt + i, 0)),
        out_shape=jax.ShapeDtypeStruct((n_batch * n_tok, WIDTH_A), F32),
        scratch_shapes=[pltpu.VMEM((n_keys, H_A * HEAD_PAD), BF16)] * 2,
        compiler_params=_params(2),
    )(*args)


def _diff_kernel(*refs, n_cache, lam_init):
    if n_cache:
        q_ref, k_ref, v_ref, ck_ref, cv_ref, lam_ref, o_ref, km_scr, vm_scr = refs
    else:
        q_ref, k_ref, v_ref, lam_ref, o_ref, km_scr, vm_scr = refs

    @pl.when(pl.program_id(1) == 0)
    def _():
        def fill(row0, k, v):
            rows = k.shape[0]
            for h in range(H_B):
                blk = slice((h // 2) * LANE, (h // 2 + 1) * LANE)
                base = (h % 2) * 2 * DH_B
                for j in range(2):
                    mask = _lane_mask(LANE, base + j * DH_B, base + (j + 1) * DH_B)
                    km_scr[2 * h + j, row0:row0 + rows, :] = (k[:, blk] * mask).astype(BF16)
                vmask = _lane_mask(LANE, base, base + DV_B)
                vm_scr[h, row0:row0 + rows, :] = (v[:, blk] * vmask).astype(BF16)

        if n_cache:
            fill(0, ck_ref[...], cv_ref[...])
        fill(n_cache, k_ref[...], v_ref[...])

    lv = lam_ref[...]
    lam = (jnp.exp(jnp.sum(lv[0:1] * lv[1:2], axis=-1, keepdims=True))
           - jnp.exp(jnp.sum(lv[2:3] * lv[3:4], axis=-1, keepdims=True)) + lam_init)
    scale = DH_B ** -0.5
    heads = []
    for h in range(H_B):
        qh = q_ref[:, (h // 2) * LANE:(h // 2 + 1) * LANE]
        probs = []
        for j in range(2):
            e, l = _softmax_parts(_dot_nt(qh, km_scr[2 * h + j]) * scale)
            probs.append(e * (1.0 / l))
        a = probs[0] - lam * probs[1]
        heads.append(_dot(a.astype(BF16), vm_scr[h]))
    o_ref[...] = jnp.concatenate([heads[2 * j] + heads[2 * j + 1] for j in range(H_B // 2)], axis=1)


def _diff(p, lw, n_batch, n_tok, cache, layer, lam_init):
    n_cache = 0 if cache is None else cache[0].shape[2]
    nqt = n_tok // Q_TILE
    in_specs = [pl.BlockSpec((Q_TILE, WIDTH_B), lambda b, i: (b * nqt + i, 0)),
                pl.BlockSpec((n_tok, WIDTH_B), lambda b, i: (b, 0)),
                pl.BlockSpec((n_tok, WIDTH_B), lambda b, i: (b, 0))]
    args = [p["qb"], p["kb"], p["vb"]]
    if n_cache:
        in_specs += [pl.BlockSpec((None, None, n_cache, WIDTH_B), lambda b, i: (b, layer, 0, 0))] * 2
        args += list(cache)
    in_specs.append(_const_spec((4, DH_B)))
    args.append(lw["lam"])
    n_keys = n_cache + n_tok
    return pl.pallas_call(
        functools.partial(_diff_kernel, n_cache=n_cache, lam_init=lam_init),
        grid=(n_batch, nqt),
        in_specs=in_specs,
        out_specs=pl.BlockSpec((Q_TILE, WIDTH_B), lambda b, i: (b * nqt + i, 0)),
        out_shape=jax.ShapeDtypeStruct((n_batch * n_tok, WIDTH_B), F32),
        scratch_shapes=[pltpu.VMEM((2 * H_B, n_keys, LANE), BF16), pltpu.VMEM((H_B, n_keys, LANE), BF16)],
        compiler_params=_params(2),
    )(*args)


def _gla_kernel(*refs, n_tok, has_init, emit_state):
    q_ref, k_ref, v_ref, gf_ref, gb_ref = refs[:5]
    refs = refs[5:]
    if has_init:
        s0f_ref, s0b_ref = refs[:2]
        refs = refs[2:]
    o_ref = refs[0]
    refs = refs[1:]
    if emit_state:
        sf_ref, sb_ref = refs[:2]
        refs = refs[2:]
    st_scr, = refs

    n_chunks = n_tok // CHUNK
    t_idx = lax.broadcasted_iota(jnp.int32, (CHUNK, CHUNK), 0)
    s_idx = lax.broadcasted_iota(jnp.int32, (CHUNK, CHUNK), 1)
    head_k = [_lane_mask(QK_C_PAD, h * DK_C, (h + 1) * DK_C) for h in range(H_C)]
    head_v = [_lane_mask(WIDTH_C, h * DV_C, (h + 1) * DV_C) for h in range(H_C)]
    row_v = lax.broadcasted_iota(jnp.int32, (WIDTH_C, QK_C_PAD), 0)
    col_k = lax.broadcasted_iota(jnp.int32, (WIDTH_C, QK_C_PAD), 1)
    diag = jnp.zeros((WIDTH_C, QK_C_PAD), F32)
    for h in range(H_C):
        inside = ((row_v >= h * DV_C) & (row_v < (h + 1) * DV_C)
                  & (col_k >= h * DK_C) & (col_k < (h + 1) * DK_C))
        diag = jnp.where(inside, 1.0, diag)

    def scan(g_ref, s0_ref, s_out_ref, reverse, first_pass):
        keep = (s_idx >= t_idx) if reverse else (s_idx <= t_idx)
        tri = jnp.where(keep, 1.0, 0.0).astype(F32)
        tri4 = jnp.concatenate([tri] * H_C, axis=1)
        if s0_ref is None:
            st_scr[...] = jnp.zeros_like(st_scr)
        else:
            st_scr[...] = s0_ref[...]

        def body(i, carry):
            c = (n_chunks - 1 - i) if reverse else i
            rows = pl.ds(pl.multiple_of(c * CHUNK, CHUNK), CHUNK)
            q, k, v, g = q_ref[rows, :], k_ref[rows, :], v_ref[rows, :], g_ref[rows, :]
            bcum = _dot_f32(tri, g)
            blast = bcum[0:1, :] if reverse else bcum[CHUNK - 1:CHUNK, :]
            qe = (q * jnp.exp(bcum)).astype(BF16)
            ke = k * jnp.exp(-bcum)
            kd = (k * jnp.exp(blast - bcum)).astype(BF16)
            ke_heads = jnp.concatenate([ke * head_k[h] for h in range(H_C)], axis=0).astype(BF16)
            attn = _dot_nt(qe, ke_heads) * tri4
            v_heads = jnp.concatenate([v * head_v[h] for h in range(H_C)], axis=0).astype(BF16)
            st = st_scr[...]
            o = _dot(attn.astype(BF16), v_heads) + _dot_nt(qe, st.astype(BF16))
            if first_pass:
                o_ref[rows, :] = o
            else:
                o_ref[rows, :] += o
            upd = _dot_tn(v.astype(BF16), kd) * diag
            st_scr[...] = st * jnp.exp(blast) + upd
            return carry

        lax.fori_loop(0, n_chunks, body, 0)
        if s_out_ref is not None:
            s_out_ref[...] = st_scr[...]

    scan(gf_ref, s0f_ref if has_init else None, sf_ref if emit_state else None, False, True)
    scan(gb_ref, s0b_ref if has_init else None, sb_ref if emit_state else None, True, False)


def _gla(p, n_batch, n_tok, init, emit_state):
    tok = lambda w: pl.BlockSpec((n_tok, w), lambda b: (b, 0))
    st_spec = pl.BlockSpec((None, WIDTH_C, QK_C_PAD), lambda b: (b, 0, 0))
    in_specs = [tok(QK_C_PAD), tok(QK_C_PAD), tok(WIDTH_C), tok(QK_C_PAD), tok(QK_C_PAD)]
    args = [p["qc"], p["kc"], p["vc"], p["gf"], p["gb"]]
    if init is not None:
        in_specs += [st_spec] * 2
        args += list(init)
    out_specs = [tok(WIDTH_C)]
    out_shape = [jax.ShapeDtypeStruct((n_batch * n_tok, WIDTH_C), F32)]
    if emit_state:
        out_specs += [st_spec] * 2
        out_shape += [jax.ShapeDtypeStruct((n_batch, WIDTH_C, QK_C_PAD), F32)] * 2
    return pl.pallas_call(
        functools.partial(_gla_kernel, n_tok=n_tok, has_init=init is not None, emit_state=emit_state),
        grid=(n_batch,),
        in_specs=in_specs,
        out_specs=out_specs,
        out_shape=out_shape,
        scratch_shapes=[pltpu.VMEM((WIDTH_C, QK_C_PAD), F32)],
        compiler_params=_params(1),
    )(*args)


def _merge_kernel(x_ref, mod_ref, oa_ref, ob_ref, oc_ref, rc_ref, na_ref, nb_ref, nc_ref,
                  gb_ref, gc_ref, wo_ref, o_ref, *, lam_init):
    a = _rms(oa_ref[...], na_ref[...])
    ob = ob_ref[...]
    b = ob * lax.rsqrt(_dot_f32(ob * ob, gb_ref[...]) + EPS) * nb_ref[...] * (1.0 - lam_init)
    oc = oc_ref[...]
    c = oc * lax.rsqrt(_dot_f32(oc * oc, gc_ref[...]) + EPS) * nc_ref[...] * _silu(rc_ref[...])
    mixed = (_dot(a.astype(BF16), wo_ref[0:WIDTH_A, :])
             + _dot(b.astype(BF16), wo_ref[WIDTH_A:WIDTH_A + WIDTH_B, :])
             + _dot(c.astype(BF16), wo_ref[WIDTH_A + WIDTH_B:, :]))
    o_ref[...] = x_ref[...] + mod_ref[0, 5:6, :] * mixed


def _merge(x, mod, tokens_per_group, oa, ob, oc, rc, lw, consts, lam_init):
    n = x.shape[0]
    row = lambda w: pl.BlockSpec((TOKEN_TILE, w), lambda i: (i, 0))
    return pl.pallas_call(
        functools.partial(_merge_kernel, lam_init=lam_init),
        grid=(n // TOKEN_TILE,),
        in_specs=[row(D_MODEL), _mod_spec(tokens_per_group), row(WIDTH_A), row(WIDTH_B), row(WIDTH_C),
                  row(WIDTH_C), _const_spec((1, WIDTH_A)), _const_spec((1, WIDTH_B)),
                  _const_spec((1, WIDTH_C)), _const_spec((WIDTH_B, WIDTH_B)),
                  _const_spec((WIDTH_C, WIDTH_C)), _const_spec((D_MODEL, D_MODEL))],
        out_specs=row(D_MODEL),
        out_shape=jax.ShapeDtypeStruct((n, D_MODEL), F32),
        compiler_params=_params(1),
    )(x, mod, oa, ob, oc, rc, lw["out_norm_a"], lw["out_norm_b"], lw["out_norm_c"],
      consts["group_b"], consts["group_c"], lw["wo"])


def _place(dst_width, pieces):
    cols = []
    for arr, width in pieces:
        pad = width - arr.shape[-1]
        cols.append(jnp.pad(arr, ((0, 0), (0, pad))) if pad else arr)
    out = jnp.concatenate(cols, axis=-1)
    assert out.shape[-1] == dst_width
    return out


def _layer_weights(l, w):
    w_in = w["w_in"][l]
    offs = [0]
    for width in (Q_RANK, KV_RANK, DR_A, 2 * H_B * DH_B, 2 * H_B * DH_B, H_B * DV_B, H_C * DK_C, H_C * DK_C,
                  WIDTH_C, WIDTH_C, GATE_RANK, GATE_RANK):
        offs.append(offs[-1] + width)
    col = lambda i: w_in[:, offs[i]:offs[i + 1]]
    zeros = lambda n: jnp.zeros((D_MODEL, n), F32)
    krope = jnp.concatenate([zeros(ROPE_OFF), col(2), zeros(LANE - ROPE_OFF - DR_A)], axis=1)
    gates = jnp.concatenate([col(10), col(11)], axis=1)
    wp = _place(PROJ_PAD, [(col(0), Q_RANK), (col(1), KV_RANK), (krope, LANE), (col(3), 256), (col(4), 256),
                           (col(5), 256), (col(6), QK_C_PAD), (col(7), QK_C_PAD), (col(8), WIDTH_C),
                           (col(9), WIDTH_C), (gates, LANE)])

    def pad_heads(m, width, lane0=0):
        m = m.reshape(m.shape[0], H_A, width)
        m = jnp.pad(m, ((0, 0), (0, 0), (lane0, HEAD_PAD - lane0 - width)))
        return m.reshape(m.shape[0], H_A * HEAD_PAD)

    wuv = w["mla_w_uv"][l].reshape(KV_RANK, H_A, DV_A)
    wuv = jnp.stack([jnp.pad(wuv[:, h], ((0, 0), ((h % 2) * DV_A, HEAD_PAD - DV_A - (h % 2) * DV_A)))
                     for h in range(H_A)], axis=1).reshape(KV_RANK, H_A * HEAD_PAD)
    wg = jnp.zeros((LANE, 2 * QK_C_PAD), F32)
    wg = wg.at[0:GATE_RANK, 0:H_C * DK_C].set(w["gla_wg_f"][l])
    wg = wg.at[GATE_RANK:2 * GATE_RANK, QK_C_PAD:QK_C_PAD + H_C * DK_C].set(w["gla_wg_b"][l])
    bg = jnp.zeros((1, 2 * QK_C_PAD), F32)
    bg = bg.at[0, 0:H_C * DK_C].set(w["gla_bg_f"][l])
    bg = bg.at[0, QK_C_PAD:QK_C_PAD + H_C * DK_C].set(w["gla_bg_b"][l])
    row = lambda v: v.reshape(1, -1)
    return {
        "norm_ffn1": row(w["norm_ffn1"][l]), "ffn1_w13": w["ffn1_w13"][l].astype(BF16),
        "ffn1_w2": w["ffn1_w2"][l].astype(BF16),
        "norm_ffn2": row(w["norm_ffn2"][l]), "ffn2_w13": w["ffn2_w13"][l].astype(BF16),
        "ffn2_w2": w["ffn2_w2"][l].astype(BF16),
        "norm_mix": row(w["norm_mix"][l]), "wp": wp.astype(BF16),
        "q_norm": row(w["mla_q_norm"][l]),
        "wuq": pad_heads(w["mla_w_uq"][l], DN_A + DR_A).astype(BF16),
        "kv_norm": row(w["mla_kv_norm"][l]),
        "wuk": pad_heads(w["mla_w_uk"][l], DN_A).astype(BF16),
        "wuv": wuv.astype(BF16),
        "wg": wg.astype(BF16), "bg": bg,
        "lam": jnp.stack([w["diff_lq1"][l], w["diff_lk1"][l], w["diff_lq2"][l], w["diff_lk2"][l]]),
        "out_norm_a": row(w["mla_out_norm"][l]),
        "out_norm_b": row(jnp.tile(w["diff_norm"][l], H_B)),
        "out_norm_c": row(jnp.tile(w["gla_norm"][l], H_C)),
        "wo": w["w_out"][l].astype(BF16),
    }


def _rope_tables(n_pos):
    pos = jnp.arange(n_pos)
    rows = (pos // GRID_W).astype(F32)
    cols = (pos % GRID_W).astype(F32)
    half = DH_B // 2
    inv = ROPE_BASE ** (-jnp.arange(0, half, 2, dtype=F32) / half)
    ang = jnp.concatenate([rows[:, None] * inv, rows[:, None] * inv,
                           cols[:, None] * inv, cols[:, None] * inv], axis=1)
    sign = jnp.tile(jnp.concatenate([-jnp.ones(8, F32), jnp.ones(8, F32)]), 2)
    cos32, sin32 = jnp.cos(ang), jnp.sin(ang) * sign
    cos_full, sin_full = jnp.tile(cos32, (1, LANE // 32)), jnp.tile(sin32, (1, LANE // 32))
    ones = lambda n: jnp.ones((n_pos, n), F32)
    zeros = lambda n: jnp.zeros((n_pos, n), F32)
    cos_head = jnp.concatenate([ones(ROPE_OFF), cos32, ones(LANE - ROPE_OFF - DR_A)], axis=1)
    sin_head = jnp.concatenate([zeros(ROPE_OFF), sin32, zeros(LANE - ROPE_OFF - DR_A)], axis=1)
    return cos_head, sin_head, cos_full, sin_full


def _group_mean_matrix(width, group):
    idx = jnp.arange(width) // group
    return jnp.where(idx[:, None] == idx[None, :], 1.0 / group, 0.0).astype(F32)


def _state_to_blockdiag_t(s):
    out = jnp.zeros((s.shape[0], WIDTH_C, QK_C_PAD), F32)
    for h in range(H_C):
        out = out.at[:, h * DV_C:(h + 1) * DV_C, h * DK_C:(h + 1) * DK_C].set(jnp.swapaxes(s[:, h], 1, 2))
    return out


def _blockdiag_t_to_state(st):
    return jnp.stack([jnp.swapaxes(st[:, h * DV_C:(h + 1) * DV_C, h * DK_C:(h + 1) * DK_C], 1, 2)
                      for h in range(H_C)], axis=1)


def kernel(x_prompt, x_sample, cache_mla_ckv, cache_mla_krope, cache_diff_k, cache_diff_v, state_gla_fwd, state_gla_bwd, c, c_ctx, w_mod, b_mod, norm_ffn1, ffn1_w13, ffn1_w2, norm_mix, w_in, mla_q_norm, mla_w_uq, mla_kv_norm, mla_w_uk, mla_w_uv, mla_out_norm, diff_lq1, diff_lk1, diff_lq2, diff_lk2, diff_norm, gla_wg_f, gla_bg_f, gla_wg_b, gla_bg_b, gla_norm, w_out, norm_ffn2, ffn2_w13, ffn2_w2, final_norm):
    w = dict(norm_ffn1=norm_ffn1, ffn1_w13=ffn1_w13, ffn1_w2=ffn1_w2, norm_mix=norm_mix, w_in=w_in,
             mla_q_norm=mla_q_norm, mla_w_uq=mla_w_uq, mla_kv_norm=mla_kv_norm, mla_w_uk=mla_w_uk,
             mla_w_uv=mla_w_uv, mla_out_norm=mla_out_norm, diff_lq1=diff_lq1, diff_lk1=diff_lk1,
             diff_lq2=diff_lq2, diff_lk2=diff_lk2, diff_norm=diff_norm, gla_wg_f=gla_wg_f,
             gla_bg_f=gla_bg_f, gla_wg_b=gla_wg_b, gla_bg_b=gla_bg_b, gla_norm=gla_norm, w_out=w_out,
             norm_ffn2=norm_ffn2, ffn2_w13=ffn2_w13, ffn2_w2=ffn2_w2)
    n_ctx_b, n_ctx_t, _ = x_prompt.shape
    n_lat_b, n_lat_t, _ = x_sample.shape
    n_past = cache_mla_ckv.shape[2]

    cvec = jnp.concatenate([c_ctx[None, :], c, jnp.zeros((16 - 1 - n_lat_b, D_MODEL), F32)], axis=0)
    mod = _modulation(cvec, w_mod, b_mod).reshape(DEPTH, 16, N_MOD, D_MODEL)

    rope_tabs = _rope_tables(n_lat_t)
    consts = {"group_b": _group_mean_matrix(WIDTH_B, DV_B), "group_c": _group_mean_matrix(WIDTH_C, DV_C)}
    cache_kr = jnp.pad(cache_mla_krope, ((0, 0), (0, 0), (0, 0), (ROPE_OFF, LANE - ROPE_OFF - DR_A)))
    cache_dk = cache_diff_k.reshape(n_lat_b, DEPTH, n_past, WIDTH_B)
    cache_dv = cache_diff_v.reshape(n_lat_b, DEPTH, n_past, WIDTH_B)
    fin = final_norm.reshape(1, D_MODEL)

    xp = x_prompt.reshape(n_ctx_b * n_ctx_t, D_MODEL)
    xs = x_sample.reshape(n_lat_b * n_lat_t, D_MODEL)
    new = {k: [] for k in ("ckv", "kr", "dk", "dv", "sf", "sb")}
    for l in range(DEPTH):
        lw = _layer_weights(l, w)
        lam_init = 0.8 - 0.6 * math.exp(-0.3 * l)
        last = l == DEPTH - 1
        mod_ctx, mod_lat = mod[l, 0:1], mod[l, 1:1 + n_lat_b]

        def trunk(x, m, tpg, n_batch, n_tok, rope, mla_cache, diff_cache, gla_init, emit_state):
            x = _ffn(x, m, tpg, lw["norm_ffn1"], lw["ffn1_w13"], lw["ffn1_w2"], fin, mod_row=0, final=False)
            p = _project(x, m, tpg, lw, rope)
            oa = _mla(p, lw, n_batch, n_tok, mla_cache, l)
            ob = _diff(p, lw, n_batch, n_tok, diff_cache, l, lam_init)
            gla_out = _gla(p, n_batch, n_tok, gla_init, emit_state)
            x = _merge(x, m, tpg, oa, ob, gla_out[0], p["rc"], lw, consts, lam_init)
            x = _ffn(x, m, tpg, lw["norm_ffn2"], lw["ffn2_w13"], lw["ffn2_w2"], fin, mod_row=6, final=last)
            return x, p, gla_out

        xp, p, gla_out = trunk(xp, mod_ctx, n_ctx_b * n_ctx_t, n_ctx_b, n_ctx_t, None, None, None, None, True)
        new["ckv"].append(p["ckv"].reshape(n_ctx_b, n_ctx_t, KV_RANK))
        new["kr"].append(p["kr"][:, ROPE_OFF:ROPE_OFF + DR_A].reshape(n_ctx_b, n_ctx_t, DR_A))
        new["dk"].append(p["kb"].reshape(n_ctx_b, n_ctx_t, H_B, 2 * DH_B))
        new["dv"].append(p["vb"].reshape(n_ctx_b, n_ctx_t, H_B, DV_B))
        new["sf"].append(_blockdiag_t_to_state(gla_out[1]))
        new["sb"].append(_blockdiag_t_to_state(gla_out[2]))

        gla_init = (_state_to_blockdiag_t(state_gla_fwd[:, l]), _state_to_blockdiag_t(state_gla_bwd[:, l]))
        xs, _, _ = trunk(xs, mod_lat, n_lat_t, n_lat_b, n_lat_t, rope_tabs, (cache_mla_ckv, cache_kr),
                         (cache_dk, cache_dv), gla_init, False)

    stack = lambda name: jnp.stack(new[name], axis=1)
    return (xp.reshape(x_prompt.shape), xs.reshape(x_sample.shape), stack("ckv"), stack("kr"),
            stack("dk"), stack("dv"), stack("sf"), stack("sb"))
```

```python
import functools
import math

import jax
import jax.numpy as jnp
from jax import lax
from jax.experimental import pallas as pl
from jax.experimental.pallas import tpu as pltpu

F32 = jnp.float32
BF16 = jnp.bfloat16

D_MODEL = 1024
DEPTH = 2
GRID_W = 64
ROPE_BASE = 10000.0
EPS = 1e-6
H_A, DN_A, DR_A, DV_A = 6, 64, 32, 64
Q_RANK, KV_RANK = 384, 256
H_B, DH_B, DV_B = 4, 32, 64
H_C, DK_C, DV_C = 4, 48, 96
GATE_RANK = 16
GATE_NORM = 16.0
CHUNK = 64
WIDTH_A = H_A * DV_A
WIDTH_B = H_B * DV_B
WIDTH_C = H_C * DV_C
D_FF = 2816
N_MOD = 9

LANE = 128
FF_CHUNK = 256
TOKEN_TILE = 512
Q_TILE = 256
MOD_TILE = 1152
GLA_UNROLL = 4
GLA_SEQS = 2
VMEM_LIMIT = 56 * 1024 * 1024

HEAD_PAD = LANE
ROPE_OFF = DN_A
QK_C_PAD = 256
SEG = {}
_off = 0
for _name, _w in (("cq", Q_RANK), ("ckv", KV_RANK), ("krope", LANE), ("qb", 256), ("kb", 256),
                  ("vb", 256), ("qc", QK_C_PAD), ("kc", QK_C_PAD), ("vc", WIDTH_C), ("rc", WIDTH_C),
                  ("gl", LANE)):
    SEG[_name] = (_off, _off + _w)
    _off += _w
PROJ_PAD = _off


def _dot(a, b):
    return jnp.dot(a, b, preferred_element_type=F32)


def _dot_nt(a, b):
    return lax.dot_general(a, b, (((1,), (1,)), ((), ())), preferred_element_type=F32)


def _dot_tn(a, b):
    return lax.dot_general(a, b, (((0,), (0,)), ((), ())), preferred_element_type=F32)


def _dot_f32(a, b):
    return jnp.dot(a, b, preferred_element_type=F32, precision=lax.Precision.HIGHEST)


def _rms(x, gain):
    ms = jnp.mean(x * x, axis=-1, keepdims=True)
    return (x * lax.rsqrt(ms + EPS)) * gain


def _silu(x):
    return x * jax.nn.sigmoid(x)


def _const_spec(shape):
    nd = len(shape)
    return pl.BlockSpec(shape, lambda *_: (0,) * nd)


def _params(n_axes):
    return pltpu.CompilerParams(dimension_semantics=("arbitrary",) * n_axes,
                                vmem_limit_bytes=VMEM_LIMIT)


def _mod_kernel(c_ref, w_ref, b_ref, o_ref):
    s = _silu(c_ref[...]).astype(BF16)
    o_ref[0] = _dot(s, w_ref[0].astype(BF16)) + b_ref[0]


def _modulation(cvec, w_mod, b_mod):
    n_rows = cvec.shape[0]
    width = N_MOD * D_MODEL
    return pl.pallas_call(
        _mod_kernel,
        grid=(DEPTH, width // MOD_TILE),
        in_specs=[
            pl.BlockSpec((n_rows, D_MODEL), lambda l, j: (0, 0)),
            pl.BlockSpec((1, D_MODEL, MOD_TILE), lambda l, j: (l, 0, j)),
            pl.BlockSpec((1, 1, MOD_TILE), lambda l, j: (l, 0, j)),
        ],
        out_specs=pl.BlockSpec((1, n_rows, MOD_TILE), lambda l, j: (l, 0, j)),
        out_shape=jax.ShapeDtypeStruct((DEPTH, n_rows, width), F32),
        compiler_params=_params(2),
    )(cvec, w_mod, b_mod.reshape(DEPTH, 1, width))


def _mod_spec(tokens_per_group):
    return pl.BlockSpec((1, N_MOD, D_MODEL), lambda i: ((i * TOKEN_TILE) // tokens_per_group, 0, 0))


def _ada_norm(x, gain, mod_ref, first_row):
    shift = mod_ref[0, first_row:first_row + 1, :]
    scale = mod_ref[0, first_row + 1:first_row + 2, :]
    return _rms(x, gain) * (1.0 + scale) + shift


def _ffn_kernel(x_ref, mod_ref, gain_ref, w13_ref, w2_ref, fin_ref, o_ref, acc_ref, *, mod_row, final):
    x = x_ref[...]
    u = _ada_norm(x, gain_ref[...], mod_ref, mod_row).astype(BF16)
    for c in range(D_FF // FF_CHUNK):
        lo, hi = c * FF_CHUNK, (c + 1) * FF_CHUNK
        a = _dot(u, w13_ref[:, lo:hi])
        b = _dot(u, w13_ref[:, D_FF + lo:D_FF + hi])
        t = _dot((_silu(a) * b).astype(BF16), w2_ref[lo:hi, :])
        if c == 0:
            acc_ref[...] = t
        else:
            acc_ref[...] += t
    gate = mod_ref[0, mod_row + 2:mod_row + 3, :]
    y = x + (0.5 * gate) * acc_ref[...]
    if final:
        y = _rms(y, fin_ref[...])
    o_ref[...] = y


def _ffn(x, mod, tokens_per_group, gain, w13, w2, fin, *, mod_row, final):
    n = x.shape[0]
    tile = pl.BlockSpec((TOKEN_TILE, D_MODEL), lambda i: (i, 0))
    return pl.pallas_call(
        functools.partial(_ffn_kernel, mod_row=mod_row, final=final),
        grid=(n // TOKEN_TILE,),
        in_specs=[tile, _mod_spec(tokens_per_group), _const_spec((1, D_MODEL)),
                  _const_spec((D_MODEL, 2 * D_FF)), _const_spec((D_FF, D_MODEL)),
                  _const_spec((1, D_MODEL))],
        out_specs=tile,
        out_shape=jax.ShapeDtypeStruct((n, D_MODEL), F32),
        scratch_shapes=[pltpu.VMEM((TOKEN_TILE, D_MODEL), F32)],
        compiler_params=_params(1),
    )(x, mod, gain, w13, w2, fin)


def _rope(x, cos, sin_signed):
    first = (lax.broadcasted_iota(jnp.int32, (x.shape[0], LANE), 1) % 16) < 8
    outs = []
    for j in range(x.shape[1] // LANE):
        xj = x[:, j * LANE:(j + 1) * LANE]
        partner = jnp.where(first, pltpu.roll(xj, LANE - 8, 1), pltpu.roll(xj, 8, 1))
        outs.append(xj * cos + partner * sin_signed)
    return outs[0] if len(outs) == 1 else jnp.concatenate(outs, axis=1)


def _log_sigmoid(z):
    return jnp.minimum(z, 0.0) - jnp.log1p(jnp.exp(-jnp.abs(z)))


def _proj_kernel(*refs, rope):
    (x_ref, mod_ref, gain_ref, wp_ref, qn_ref, wuq_ref, kvn_ref, wg_ref, bg_ref) = refs[:9]
    refs = refs[9:]
    if rope:
        ch_ref, sh_ref, cf_ref, sf_ref = refs[:4]
        refs = refs[4:]
    (q_o, ckv_o, kr_o, qb_o, kb_o, vb_o, qc_o, kc_o, vc_o, rc_o, gf_o, gb_o) = refs

    u = _ada_norm(x_ref[...], gain_ref[...], mod_ref, 3).astype(BF16)

    def seg(name):
        lo, hi = SEG[name]
        return _dot(u, wp_ref[:, lo:hi])

    q = _dot(_rms(seg("cq"), qn_ref[...]).astype(BF16), wuq_ref[...])
    kr = seg("krope")
    qb = seg("qb")
    kb = seg("kb")
    if rope:
        q = _rope(q, ch_ref[...], sh_ref[...])
        kr = _rope(kr, ch_ref[...], sh_ref[...])
        qb = _rope(qb, cf_ref[...], sf_ref[...])
        kb = _rope(kb, cf_ref[...], sf_ref[...])
    q_o[...] = q.astype(BF16)
    ckv_o[...] = _rms(seg("ckv"), kvn_ref[...])
    kr_o[...] = kr
    qb_o[...] = qb.astype(BF16)
    kb_o[...] = kb
    vb_o[...] = seg("vb")
    qc_o[...] = seg("qc") * (DK_C ** -0.5)
    kc_o[...] = seg("kc")
    vc_o[...] = seg("vc")
    rc_o[...] = seg("rc")
    z = _dot(seg("gl").astype(BF16), wg_ref[...]) + bg_ref[...]
    g = _log_sigmoid(z) / GATE_NORM
    gf_o[...] = g[:, :QK_C_PAD]
    gb_o[...] = g[:, QK_C_PAD:]


PROJ_OUT = (("q", H_A * HEAD_PAD, BF16), ("ckv", KV_RANK, F32), ("kr", LANE, F32), ("qb", 256, BF16),
            ("kb", 256, F32), ("vb", 256, F32), ("qc", QK_C_PAD, F32), ("kc", QK_C_PAD, F32),
            ("vc", WIDTH_C, F32), ("rc", WIDTH_C, F32), ("gf", QK_C_PAD, F32), ("gb", QK_C_PAD, F32))


def _project(x, mod, tokens_per_group, lw, rope_tabs):
    n = x.shape[0]
    rope = rope_tabs is not None
    row = lambda w: pl.BlockSpec((TOKEN_TILE, w), lambda i: (i, 0))
    in_specs = [row(D_MODEL), _mod_spec(tokens_per_group), _const_spec((1, D_MODEL)),
                _const_spec((D_MODEL, PROJ_PAD)), _const_spec((1, Q_RANK)),
                _const_spec((Q_RANK, H_A * HEAD_PAD)), _const_spec((1, KV_RANK)),
                _const_spec((LANE, 2 * QK_C_PAD)), _const_spec((1, 2 * QK_C_PAD))]
    args = [x, mod, lw["norm_mix"], lw["wp"], lw["q_norm"], lw["wuq"], lw["kv_norm"], lw["wg"], lw["bg"]]
    if rope:
        n_pos = rope_tabs[0].shape[0]
        tab = pl.BlockSpec((TOKEN_TILE, LANE), lambda i: (i % (n_pos // TOKEN_TILE), 0))
        in_specs += [tab] * 4
        args += list(rope_tabs)
    outs = pl.pallas_call(
        functools.partial(_proj_kernel, rope=rope),
        grid=(n // TOKEN_TILE,),
        in_specs=in_specs,
        out_specs=[row(w) for _, w, _ in PROJ_OUT],
        out_shape=[jax.ShapeDtypeStruct((n, w), dt) for _, w, dt in PROJ_OUT],
        compiler_params=_params(1),
    )(*args)
    return {name: o for (name, _, _), o in zip(PROJ_OUT, outs)}


def _lane_mask(width, lo, hi):
    lane = lax.broadcasted_iota(jnp.int32, (1, width), 1)
    return jnp.where((lane >= lo) & (lane < hi), 1.0, 0.0).astype(F32)


def _softmax_parts(s):
    m = jnp.max(s, axis=-1, keepdims=True)
    e = jnp.exp(s - m)
    return e, jnp.sum(e, axis=-1, keepdims=True)


def _mla_kernel(*refs, n_cache):
    if n_cache:
        q_ref, ckv_ref, kr_ref, cckv_ref, ckr_ref, wuk_ref, wuv_ref, o_ref, kf_scr, v_scr = refs
    else:
        q_ref, ckv_ref, kr_ref, wuk_ref, wuv_ref, o_ref, kf_scr, v_scr = refs

    @pl.when(pl.program_id(1) == 0)
    def _():
        def fill(row0, ckv, kr):
            rows = ckv.shape[0]
            cb = ckv.astype(BF16)
            kr_all = jnp.concatenate([kr] * H_A, axis=1)
            kf_scr[row0:row0 + rows, :] = (_dot(cb, wuk_ref[...]) + kr_all).astype(BF16)
            v_scr[row0:row0 + rows, :] = _dot(cb, wuv_ref[...]).astype(BF16)

        if n_cache:
            fill(0, cckv_ref[...], ckr_ref[...])
        fill(n_cache, ckv_ref[...], kr_ref[...])

    scale = (DN_A + DR_A) ** -0.5
    heads = []
    for h in range(H_A):
        sl = slice(h * HEAD_PAD, (h + 1) * HEAD_PAD)
        s = _dot_nt(q_ref[:, sl], kf_scr[:, sl]) * scale
        e, l = _softmax_parts(s)
        heads.append(_dot(e.astype(BF16), v_scr[:, sl]) * (1.0 / l))
    o_ref[...] = jnp.concatenate([heads[2 * j] + heads[2 * j + 1] for j in range(H_A // 2)], axis=1)


def _mla(p, lw, n_batch, n_tok, cache, layer):
    n_cache = 0 if cache is None else cache[0].shape[2]
    nqt = n_tok // Q_TILE
    in_specs = [pl.BlockSpec((Q_TILE, H_A * HEAD_PAD), lambda b, i: (b * nqt + i, 0)),
                pl.BlockSpec((n_tok, KV_RANK), lambda b, i: (b, 0)),
                pl.BlockSpec((n_tok, LANE), lambda b, i: (b, 0))]
    args = [p["q"], p["ckv"], p["kr"]]
    if n_cache:
        in_specs += [pl.BlockSpec((None, None, n_cache, KV_RANK), lambda b, i: (b, layer, 0, 0)),
                     pl.BlockSpec((None, None, n_cache, LANE), lambda b, i: (b, layer, 0, 0))]
        args += list(cache)
    in_specs += [_const_spec((KV_RANK, H_A * HEAD_PAD))] * 2
    args += [lw["wuk"], lw["wuv"]]
    n_keys = n_cache + n_tok
    return pl.pallas_call(
        functools.partial(_mla_kernel, n_cache=n_cache),
        grid=(n_batch, nqt),
        in_specs=in_specs,
        out_specs=pl.BlockSpec((Q_TILE, WIDTH_A), lambda b, i: (b * nqt + i, 0)),
        out_shape=jax.ShapeDtypeStruct((n_batch * n_tok, WIDTH_A), F32),
        scratch_shapes=[pltpu.VMEM((n_keys, H_A * HEAD_PAD), BF16)] * 2,
        compiler_params=_params(2),
    )(*args)


def _diff_kernel(*refs, n_cache, lam_init):
    if n_cache:
        q_ref, k_ref, v_ref, ck_ref, cv_ref, lam_ref, o_ref, km_scr, vm_scr = refs
    else:
        q_ref, k_ref, v_ref, lam_ref, o_ref, km_scr, vm_scr = refs

    @pl.when(pl.program_id(1) == 0)
    def _():
        def fill(row0, k, v):
            rows = k.shape[0]
            for h in range(H_B):
                blk = slice((h // 2) * LANE, (h // 2 + 1) * LANE)
                base = (h % 2) * 2 * DH_B
                for j in range(2):
                    mask = _lane_mask(LANE, base + j * DH_B, base + (j + 1) * DH_B)
                    km_scr[2 * h + j, row0:row0 + rows, :] = (k[:, blk] * mask).astype(BF16)
                vmask = _lane_mask(LANE, base, base + DV_B)
                vm_scr[h, row0:row0 + rows, :] = (v[:, blk] * vmask).astype(BF16)

        if n_cache:
            fill(0, ck_ref[...], cv_ref[...])
        fill(n_cache, k_ref[...], v_ref[...])

    lv = lam_ref[...]
    lam = (jnp.exp(jnp.sum(lv[0:1] * lv[1:2], axis=-1, keepdims=True))
           - jnp.exp(jnp.sum(lv[2:3] * lv[3:4], axis=-1, keepdims=True)) + lam_init)
    scale = DH_B ** -0.5
    heads = []
    for h in range(H_B):
        qh = q_ref[:, (h // 2) * LANE:(h // 2 + 1) * LANE]
        probs = []
        for j in range(2):
            e, l = _softmax_parts(_dot_nt(qh, km_scr[2 * h + j]) * scale)
            probs.append(e * (1.0 / l))
        a = probs[0] - lam * probs[1]
        heads.append(_dot(a.astype(BF16), vm_scr[h]))
    o_ref[...] = jnp.concatenate([heads[2 * j] + heads[2 * j + 1] for j in range(H_B // 2)], axis=1)


def _diff(p, lw, n_batch, n_tok, cache, layer, lam_init):
    n_cache = 0 if cache is None else cache[0].shape[2]
    nqt = n_tok // Q_TILE
    in_specs = [pl.BlockSpec((Q_TILE, WIDTH_B), lambda b, i: (b * nqt + i, 0)),
                pl.BlockSpec((n_tok, WIDTH_B), lambda b, i: (b, 0)),
                pl.BlockSpec((n_tok, WIDTH_B), lambda b, i: (b, 0))]
    args = [p["qb"], p["kb"], p["vb"]]
    if n_cache:
        in_specs += [pl.BlockSpec((None, None, n_cache, WIDTH_B), lambda b, i: (b, layer, 0, 0))] * 2
        args += list(cache)
    in_specs.append(_const_spec((4, DH_B)))
    args.append(lw["lam"])
    n_keys = n_cache + n_tok
    return pl.pallas_call(
        functools.partial(_diff_kernel, n_cache=n_cache, lam_init=lam_init),
        grid=(n_batch, nqt),
        in_specs=in_specs,
        out_specs=pl.BlockSpec((Q_TILE, WIDTH_B), lambda b, i: (b * nqt + i, 0)),
        out_shape=jax.ShapeDtypeStruct((n_batch * n_tok, WIDTH_B), F32),
        scratch_shapes=[pltpu.VMEM((2 * H_B, n_keys, LANE), BF16), pltpu.VMEM((H_B, n_keys, LANE), BF16)],
        compiler_params=_params(2),
    )(*args)


def _split3(x):
    hi = x.astype(BF16)
    r1 = x - hi.astype(F32)
    mid = r1.astype(BF16)
    lo = (r1 - mid.astype(F32)).astype(BF16)
    return hi, mid, lo


def _dot_exact_lhs(a_bf16, x):
    hi, mid, lo = _split3(x)
    return _dot(a_bf16, hi) + _dot(a_bf16, mid) + _dot(a_bf16, lo)


def _gla_kernel(*refs, n_tok, n_seq, has_init, emit_state):
    q_ref, k_ref, v_ref, gf_ref, gb_ref = refs[:5]
    refs = refs[5:]
    if has_init:
        s0f_ref, s0b_ref = refs[:2]
        refs = refs[2:]
    o_ref = refs[0]
    refs = refs[1:]
    if emit_state:
        sf_ref, sb_ref = refs[:2]
        refs = refs[2:]
    st_scr, = refs

    n_chunks = n_tok // CHUNK
    n_groups = n_chunks // GLA_UNROLL
    t_idx = lax.broadcasted_iota(jnp.int32, (CHUNK, CHUNK), 0)
    s_idx = lax.broadcasted_iota(jnp.int32, (CHUNK, CHUNK), 1)
    tri = [jnp.where(s_idx <= t_idx, 1.0, 0.0).astype(F32), jnp.where(s_idx >= t_idx, 1.0, 0.0).astype(F32)]
    tri4 = [jnp.concatenate([t] * H_C, axis=1) for t in tri]
    tri_b = [t.astype(BF16) for t in tri]
    head_k = [_lane_mask(QK_C_PAD, h * DK_C, (h + 1) * DK_C) for h in range(H_C)]
    head_v = [_lane_mask(WIDTH_C, h * DV_C, (h + 1) * DV_C) for h in range(H_C)]
    row_v = lax.broadcasted_iota(jnp.int32, (WIDTH_C, QK_C_PAD), 0)
    col_k = lax.broadcasted_iota(jnp.int32, (WIDTH_C, QK_C_PAD), 1)
    diag = jnp.zeros((WIDTH_C, QK_C_PAD), F32)
    for h in range(H_C):
        inside = ((row_v >= h * DV_C) & (row_v < (h + 1) * DV_C)
                  & (col_k >= h * DK_C) & (col_k < (h + 1) * DK_C))
        diag = jnp.where(inside, 1.0, diag)

    for s in range(n_seq):
        for d, s0_ref in enumerate((s0f_ref, s0b_ref) if has_init else (None, None)):
            st_scr[s, d] = jnp.zeros((WIDTH_C, QK_C_PAD), F32) if s0_ref is None else s0_ref[s]
    o_ref[...] = jnp.zeros_like(o_ref)

    def chunk_steps(chains):
        loaded = []
        for s, d, c in chains:
            g_ref = gb_ref if d else gf_ref
            row0 = s * n_tok + c * CHUNK
            rows = pl.ds(row0 if isinstance(row0, int) else pl.multiple_of(row0, CHUNK), CHUNK)
            bcum = _dot_exact_lhs(tri_b[d], g_ref[rows, :])
            loaded.append((rows, q_ref[rows, :], k_ref[rows, :], v_ref[rows, :], bcum))
        scaled = []
        for (s, d, c), (rows, q, k, v, bcum) in zip(chains, loaded):
            blast = bcum[0:1, :] if d else bcum[CHUNK - 1:CHUNK, :]
            qe = (q * jnp.exp(bcum)).astype(BF16)
            ke = k * jnp.exp(-bcum)
            kd = (k * jnp.exp(blast - bcum)).astype(BF16)
            ke_heads = jnp.concatenate([ke * head_k[h] for h in range(H_C)], axis=0).astype(BF16)
            v_heads = jnp.concatenate([v * head_v[h] for h in range(H_C)], axis=0).astype(BF16)
            scaled.append((qe, ke_heads, kd, v.astype(BF16), v_heads, jnp.exp(blast)))
        products = []
        for (s, d, c), (qe, ke_heads, kd, vb, v_heads, decay) in zip(chains, scaled):
            st = st_scr[s, d]
            attn = _dot_nt(qe, ke_heads)
            inter = _dot_nt(qe, st.astype(BF16))
            upd = _dot_tn(vb, kd)
            products.append((st, attn, inter, upd))
        masked = []
        for (s, d, c), (qe, ke_heads, kd, vb, v_heads, decay), (st, attn, inter, upd) in zip(
                chains, scaled, products):
            st_scr[s, d] = st * decay + upd * diag
            masked.append((attn * tri4[d]).astype(BF16))
        for (rows, *_), (_, _, _, _, v_heads, _), (_, _, inter, _), attn_b in zip(
                loaded, scaled, products, masked):
            o_ref[rows, :] += _dot(attn_b, v_heads) + inter

    def group(i):
        for j in range(GLA_UNROLL):
            fwd = i * GLA_UNROLL + j
            chunk_steps([(s, d, n_chunks - 1 - fwd if d else fwd) for s in range(n_seq) for d in (0, 1)])

    if n_groups == 1:
        group(0)
    else:
        def body(i, carry):
            group(i)
            return carry
        lax.fori_loop(0, n_groups, body, 0)

    if emit_state:
        for s in range(n_seq):
            sf_ref[s] = st_scr[s, 0]
            sb_ref[s] = st_scr[s, 1]


def _gla(p, n_batch, n_tok, init, emit_state):
    n_seq = GLA_SEQS
    tok = lambda w: pl.BlockSpec((n_seq * n_tok, w), lambda b: (b, 0))
    st_spec = pl.BlockSpec((n_seq, WIDTH_C, QK_C_PAD), lambda b: (b, 0, 0))
    in_specs = [tok(QK_C_PAD), tok(QK_C_PAD), tok(WIDTH_C), tok(QK_C_PAD), tok(QK_C_PAD)]
    args = [p["qc"], p["kc"], p["vc"], p["gf"], p["gb"]]
    if init is not None:
        in_specs += [st_spec] * 2
        args += list(init)
    out_specs = [tok(WIDTH_C)]
    out_shape = [jax.ShapeDtypeStruct((n_batch * n_tok, WIDTH_C), F32)]
    if emit_state:
        out_specs += [st_spec] * 2
        out_shape += [jax.ShapeDtypeStruct((n_batch, WIDTH_C, QK_C_PAD), F32)] * 2
    return pl.pallas_call(
        functools.partial(_gla_kernel, n_tok=n_tok, n_seq=n_seq, has_init=init is not None,
                          emit_state=emit_state),
        grid=(n_batch // n_seq,),
        in_specs=in_specs,
        out_specs=out_specs,
        out_shape=out_shape,
        scratch_shapes=[pltpu.VMEM((n_seq, 2, WIDTH_C, QK_C_PAD), F32)],
        compiler_params=_params(1),
    )(*args)


def _merge_kernel(x_ref, mod_ref, oa_ref, ob_ref, oc_ref, rc_ref, na_ref, nb_ref, nc_ref,
                  gb_ref, gc_ref, wo_ref, o_ref, *, lam_init):
    a = _rms(oa_ref[...], na_ref[...])
    ob = ob_ref[...]
    b = ob * lax.rsqrt(_dot_f32(ob * ob, gb_ref[...]) + EPS) * nb_ref[...] * (1.0 - lam_init)
    oc = oc_ref[...]
    c = oc * lax.rsqrt(_dot_f32(oc * oc, gc_ref[...]) + EPS) * nc_ref[...] * _silu(rc_ref[...])
    mixed = (_dot(a.astype(BF16), wo_ref[0:WIDTH_A, :])
             + _dot(b.astype(BF16), wo_ref[WIDTH_A:WIDTH_A + WIDTH_B, :])
             + _dot(c.astype(BF16), wo_ref[WIDTH_A + WIDTH_B:, :]))
    o_ref[...] = x_ref[...] + mod_ref[0, 5:6, :] * mixed


def _merge(x, mod, tokens_per_group, oa, ob, oc, rc, lw, consts, lam_init):
    n = x.shape[0]
    row = lambda w: pl.BlockSpec((TOKEN_TILE, w), lambda i: (i, 0))
    return pl.pallas_call(
        functools.partial(_merge_kernel, lam_init=lam_init),
        grid=(n // TOKEN_TILE,),
        in_specs=[row(D_MODEL), _mod_spec(tokens_per_group), row(WIDTH_A), row(WIDTH_B), row(WIDTH_C),
                  row(WIDTH_C), _const_spec((1, WIDTH_A)), _const_spec((1, WIDTH_B)),
                  _const_spec((1, WIDTH_C)), _const_spec((WIDTH_B, WIDTH_B)),
                  _const_spec((WIDTH_C, WIDTH_C)), _const_spec((D_MODEL, D_MODEL))],
        out_specs=row(D_MODEL),
        out_shape=jax.ShapeDtypeStruct((n, D_MODEL), F32),
        compiler_params=_params(1),
    )(x, mod, oa, ob, oc, rc, lw["out_norm_a"], lw["out_norm_b"], lw["out_norm_c"],
      consts["group_b"], consts["group_c"], lw["wo"])


def _place(dst_width, pieces):
    cols = []
    for arr, width in pieces:
        pad = width - arr.shape[-1]
        cols.append(jnp.pad(arr, ((0, 0), (0, pad))) if pad else arr)
    out = jnp.concatenate(cols, axis=-1)
    assert out.shape[-1] == dst_width
    return out


def _layer_weights(l, w):
    w_in = w["w_in"][l]
    offs = [0]
    for width in (Q_RANK, KV_RANK, DR_A, 2 * H_B * DH_B, 2 * H_B * DH_B, H_B * DV_B, H_C * DK_C, H_C * DK_C,
                  WIDTH_C, WIDTH_C, GATE_RANK, GATE_RANK):
        offs.append(offs[-1] + width)
    col = lambda i: w_in[:, offs[i]:offs[i + 1]]
    zeros = lambda n: jnp.zeros((D_MODEL, n), F32)
    krope = jnp.concatenate([zeros(ROPE_OFF), col(2), zeros(LANE - ROPE_OFF - DR_A)], axis=1)
    gates = jnp.concatenate([col(10), col(11)], axis=1)
    wp = _place(PROJ_PAD, [(col(0), Q_RANK), (col(1), KV_RANK), (krope, LANE), (col(3), 256), (col(4), 256),
                           (col(5), 256), (col(6), QK_C_PAD), (col(7), QK_C_PAD), (col(8), WIDTH_C),
                           (col(9), WIDTH_C), (gates, LANE)])

    def pad_heads(m, width, lane0=0):
        m = m.reshape(m.shape[0], H_A, width)
        m = jnp.pad(m, ((0, 0), (0, 0), (lane0, HEAD_PAD - lane0 - width)))
        return m.reshape(m.shape[0], H_A * HEAD_PAD)

    wuv = w["mla_w_uv"][l].reshape(KV_RANK, H_A, DV_A)
    wuv = jnp.stack([jnp.pad(wuv[:, h], ((0, 0), ((h % 2) * DV_A, HEAD_PAD - DV_A - (h % 2) * DV_A)))
                     for h in range(H_A)], axis=1).reshape(KV_RANK, H_A * HEAD_PAD)
    wg = jnp.zeros((LANE, 2 * QK_C_PAD), F32)
    wg = wg.at[0:GATE_RANK, 0:H_C * DK_C].set(w["gla_wg_f"][l])
    wg = wg.at[GATE_RANK:2 * GATE_RANK, QK_C_PAD:QK_C_PAD + H_C * DK_C].set(w["gla_wg_b"][l])
    bg = jnp.zeros((1, 2 * QK_C_PAD), F32)
    bg = bg.at[0, 0:H_C * DK_C].set(w["gla_bg_f"][l])
    bg = bg.at[0, QK_C_PAD:QK_C_PAD + H_C * DK_C].set(w["gla_bg_b"][l])
    row = lambda v: v.reshape(1, -1)
    return {
        "norm_ffn1": row(w["norm_ffn1"][l]), "ffn1_w13": w["ffn1_w13"][l].astype(BF16),
        "ffn1_w2": w["ffn1_w2"][l].astype(BF16),
        "norm_ffn2": row(w["norm_ffn2"][l]), "ffn2_w13": w["ffn2_w13"][l].astype(BF16),
        "ffn2_w2": w["ffn2_w2"][l].astype(BF16),
        "norm_mix": row(w["norm_mix"][l]), "wp": wp.astype(BF16),
        "q_norm": row(w["mla_q_norm"][l]),
        "wuq": pad_heads(w["mla_w_uq"][l], DN_A + DR_A).astype(BF16),
        "kv_norm": row(w["mla_kv_norm"][l]),
        "wuk": pad_heads(w["mla_w_uk"][l], DN_A).astype(BF16),
        "wuv": wuv.astype(BF16),
        "wg": wg.astype(BF16), "bg": bg,
        "lam": jnp.stack([w["diff_lq1"][l], w["diff_lk1"][l], w["diff_lq2"][l], w["diff_lk2"][l]]),
        "out_norm_a": row(w["mla_out_norm"][l]),
        "out_norm_b": row(jnp.tile(w["diff_norm"][l], H_B)),
        "out_norm_c": row(jnp.tile(w["gla_norm"][l], H_C)),
        "wo": w["w_out"][l].astype(BF16),
    }


def _rope_tables(n_pos):
    pos = jnp.arange(n_pos)
    rows = (pos // GRID_W).astype(F32)
    cols = (pos % GRID_W).astype(F32)
    half = DH_B // 2
    inv = ROPE_BASE ** (-jnp.arange(0, half, 2, dtype=F32) / half)
    ang = jnp.concatenate([rows[:, None] * inv, rows[:, None] * inv,
                           cols[:, None] * inv, cols[:, None] * inv], axis=1)
    sign = jnp.tile(jnp.concatenate([-jnp.ones(8, F32), jnp.ones(8, F32)]), 2)
    cos32, sin32 = jnp.cos(ang), jnp.sin(ang) * sign
    cos_full, sin_full = jnp.tile(cos32, (1, LANE // 32)), jnp.tile(sin32, (1, LANE // 32))
    ones = lambda n: jnp.ones((n_pos, n), F32)
    zeros = lambda n: jnp.zeros((n_pos, n), F32)
    cos_head = jnp.concatenate([ones(ROPE_OFF), cos32, ones(LANE - ROPE_OFF - DR_A)], axis=1)
    sin_head = jnp.concatenate([zeros(ROPE_OFF), sin32, zeros(LANE - ROPE_OFF - DR_A)], axis=1)
    return cos_head, sin_head, cos_full, sin_full


def _group_mean_matrix(width, group):
    idx = jnp.arange(width) // group
    return jnp.where(idx[:, None] == idx[None, :], 1.0 / group, 0.0).astype(F32)


def _state_to_blockdiag_t(s):
    out = jnp.zeros((s.shape[0], WIDTH_C, QK_C_PAD), F32)
    for h in range(H_C):
        out = out.at[:, h * DV_C:(h + 1) * DV_C, h * DK_C:(h + 1) * DK_C].set(jnp.swapaxes(s[:, h], 1, 2))
    return out


def _blockdiag_t_to_state(st):
    return jnp.stack([jnp.swapaxes(st[:, h * DV_C:(h + 1) * DV_C, h * DK_C:(h + 1) * DK_C], 1, 2)
                      for h in range(H_C)], axis=1)


def kernel(x_prompt, x_sample, cache_mla_ckv, cache_mla_krope, cache_diff_k, cache_diff_v, state_gla_fwd, state_gla_bwd, c, c_ctx, w_mod, b_mod, norm_ffn1, ffn1_w13, ffn1_w2, norm_mix, w_in, mla_q_norm, mla_w_uq, mla_kv_norm, mla_w_uk, mla_w_uv, mla_out_norm, diff_lq1, diff_lk1, diff_lq2, diff_lk2, diff_norm, gla_wg_f, gla_bg_f, gla_wg_b, gla_bg_b, gla_norm, w_out, norm_ffn2, ffn2_w13, ffn2_w2, final_norm):
    w = dict(norm_ffn1=norm_ffn1, ffn1_w13=ffn1_w13, ffn1_w2=ffn1_w2, norm_mix=norm_mix, w_in=w_in,
             mla_q_norm=mla_q_norm, mla_w_uq=mla_w_uq, mla_kv_norm=mla_kv_norm, mla_w_uk=mla_w_uk,
             mla_w_uv=mla_w_uv, mla_out_norm=mla_out_norm, diff_lq1=diff_lq1, diff_lk1=diff_lk1,
             diff_lq2=diff_lq2, diff_lk2=diff_lk2, diff_norm=diff_norm, gla_wg_f=gla_wg_f,
             gla_bg_f=gla_bg_f, gla_wg_b=gla_wg_b, gla_bg_b=gla_bg_b, gla_norm=gla_norm, w_out=w_out,
             norm_ffn2=norm_ffn2, ffn2_w13=ffn2_w13, ffn2_w2=ffn2_w2)
    n_ctx_b, n_ctx_t, _ = x_prompt.shape
    n_lat_b, n_lat_t, _ = x_sample.shape
    n_past = cache_mla_ckv.shape[2]

    cvec = jnp.concatenate([c_ctx[None, :], c, jnp.zeros((16 - 1 - n_lat_b, D_MODEL), F32)], axis=0)
    mod = _modulation(cvec, w_mod, b_mod).reshape(DEPTH, 16, N_MOD, D_MODEL)

    rope_tabs = _rope_tables(n_lat_t)
    consts = {"group_b": _group_mean_matrix(WIDTH_B, DV_B), "group_c": _group_mean_matrix(WIDTH_C, DV_C)}
    cache_kr = jnp.pad(cache_mla_krope, ((0, 0), (0, 0), (0, 0), (ROPE_OFF, LANE - ROPE_OFF - DR_A)))
    cache_dk = cache_diff_k.reshape(n_lat_b, DEPTH, n_past, WIDTH_B)
    cache_dv = cache_diff_v.reshape(n_lat_b, DEPTH, n_past, WIDTH_B)
    fin = final_norm.reshape(1, D_MODEL)

    xp = x_prompt.reshape(n_ctx_b * n_ctx_t, D_MODEL)
    xs = x_sample.reshape(n_lat_b * n_lat_t, D_MODEL)
    new = {k: [] for k in ("ckv", "kr", "dk", "dv", "sf", "sb")}
    for l in range(DEPTH):
        lw = _layer_weights(l, w)
        lam_init = 0.8 - 0.6 * math.exp(-0.3 * l)
        last = l == DEPTH - 1
        mod_ctx, mod_lat = mod[l, 0:1], mod[l, 1:1 + n_lat_b]

        def trunk(x, m, tpg, n_batch, n_tok, rope, mla_cache, diff_cache, gla_init, emit_state):
            x = _ffn(x, m, tpg, lw["norm_ffn1"], lw["ffn1_w13"], lw["ffn1_w2"], fin, mod_row=0, final=False)
            p = _project(x, m, tpg, lw, rope)
            oa = _mla(p, lw, n_batch, n_tok, mla_cache, l)
            ob = _diff(p, lw, n_batch, n_tok, diff_cache, l, lam_init)
            gla_out = _gla(p, n_batch, n_tok, gla_init, emit_state)
            x = _merge(x, m, tpg, oa, ob, gla_out[0], p["rc"], lw, consts, lam_init)
            x = _ffn(x, m, tpg, lw["norm_ffn2"], lw["ffn2_w13"], lw["ffn2_w2"], fin, mod_row=6, final=last)
            return x, p, gla_out

        xp, p, gla_out = trunk(xp, mod_ctx, n_ctx_b * n_ctx_t, n_ctx_b, n_ctx_t, None, None, None, None, True)
        new["ckv"].append(p["ckv"].reshape(n_ctx_b, n_ctx_t, KV_RANK))
        new["kr"].append(p["kr"][:, ROPE_OFF:ROPE_OFF + DR_A].reshape(n_ctx_b, n_ctx_t, DR_A))
        new["dk"].append(p["kb"].reshape(n_ctx_b, n_ctx_t, H_B, 2 * DH_B))
        new["dv"].append(p["vb"].reshape(n_ctx_b, n_ctx_t, H_B, DV_B))
        new["sf"].append(_blockdiag_t_to_state(gla_out[1]))
        new["sb"].append(_blockdiag_t_to_state(gla_out[2]))

        gla_init = (_state_to_blockdiag_t(state_gla_fwd[:, l]), _state_to_blockdiag_t(state_gla_bwd[:, l]))
        xs, _, _ = trunk(xs, mod_lat, n_lat_t, n_lat_b, n_lat_t, rope_tabs, (cache_mla_ckv, cache_kr),
                         (cache_dk, cache_dv), gla_init, False)

    stack = lambda name: jnp.stack(new[name], axis=1)
    return (xp.reshape(x_prompt.shape), xs.reshape(x_sample.shape), stack("ckv"), stack("kr"),
            stack("dk"), stack("dv"), stack("sf"), stack("sb"))
```

```python
import functools
import math

import jax
import jax.numpy as jnp
from jax import lax
from jax.experimental import pallas as pl
from jax.experimental.pallas import tpu as pltpu

F32 = jnp.float32
BF16 = jnp.bfloat16

D_MODEL = 1024
DEPTH = 2
GRID_W = 64
ROPE_BASE = 10000.0
EPS = 1e-6
H_A, DN_A, DR_A, DV_A = 6, 64, 32, 64
Q_RANK, KV_RANK = 384, 256
H_B, DH_B, DV_B = 4, 32, 64
H_C, DK_C, DV_C = 4, 48, 96
GATE_RANK = 16
GATE_NORM = 16.0
CHUNK = 64
WIDTH_A = H_A * DV_A
WIDTH_B = H_B * DV_B
WIDTH_C = H_C * DV_C
D_FF = 2816
N_MOD = 9

LANE = 128
FF_CHUNK = 256
TOKEN_TILE = 512
Q_TILE = 256
KEY_BLOCK = 256
ONES_ROWS = 16
MOD_TILE = 1152
GLA_UNROLL = 4
GLA_SEQS = 2
VMEM_LIMIT = 56 * 1024 * 1024

HEAD_PAD = LANE
ROPE_OFF = DN_A
QK_C_PAD = 256
SEG = {}
_off = 0
for _name, _w in (("cq", Q_RANK), ("ckv", KV_RANK), ("krope", LANE), ("qb", 256), ("kb", 256),
                  ("vb", 256), ("qc", QK_C_PAD), ("kc", QK_C_PAD), ("vc", WIDTH_C), ("rc", WIDTH_C),
                  ("gl", LANE)):
    SEG[_name] = (_off, _off + _w)
    _off += _w
PROJ_PAD = _off


def _dot(a, b):
    return jnp.dot(a, b, preferred_element_type=F32)


def _dot_nt(a, b):
    return lax.dot_general(a, b, (((1,), (1,)), ((), ())), preferred_element_type=F32)


def _dot_tn(a, b):
    return lax.dot_general(a, b, (((0,), (0,)), ((), ())), preferred_element_type=F32)


def _dot_f32(a, b):
    return jnp.dot(a, b, preferred_element_type=F32, precision=lax.Precision.HIGHEST)


def _rms(x, gain):
    ms = jnp.mean(x * x, axis=-1, keepdims=True)
    return (x * lax.rsqrt(ms + EPS)) * gain


def _silu(x):
    return x * jax.nn.sigmoid(x)


def _const_spec(shape):
    nd = len(shape)
    return pl.BlockSpec(shape, lambda *_: (0,) * nd)


def _params(n_axes):
    return pltpu.CompilerParams(dimension_semantics=("arbitrary",) * n_axes,
                                vmem_limit_bytes=VMEM_LIMIT)


def _mod_kernel(c_ref, w_ref, b_ref, o_ref):
    s = _silu(c_ref[...]).astype(BF16)
    o_ref[0] = _dot(s, w_ref[0].astype(BF16)) + b_ref[0]


def _modulation(cvec, w_mod, b_mod):
    n_rows = cvec.shape[0]
    width = N_MOD * D_MODEL
    return pl.pallas_call(
        _mod_kernel,
        grid=(DEPTH, width // MOD_TILE),
        in_specs=[
            pl.BlockSpec((n_rows, D_MODEL), lambda l, j: (0, 0)),
            pl.BlockSpec((1, D_MODEL, MOD_TILE), lambda l, j: (l, 0, j)),
            pl.BlockSpec((1, 1, MOD_TILE), lambda l, j: (l, 0, j)),
        ],
        out_specs=pl.BlockSpec((1, n_rows, MOD_TILE), lambda l, j: (l, 0, j)),
        out_shape=jax.ShapeDtypeStruct((DEPTH, n_rows, width), F32),
        compiler_params=_params(2),
    )(cvec, w_mod, b_mod.reshape(DEPTH, 1, width))


def _mod_spec(tokens_per_group):
    return pl.BlockSpec((1, N_MOD, D_MODEL), lambda i: ((i * TOKEN_TILE) // tokens_per_group, 0, 0))


def _ada_norm(x, gain, mod_ref, first_row):
    shift = mod_ref[0, first_row:first_row + 1, :]
    scale = mod_ref[0, first_row + 1:first_row + 2, :]
    return _rms(x, gain) * (1.0 + scale) + shift


def _ffn_kernel(x_ref, mod_ref, gain_ref, w13_ref, w2_ref, fin_ref, o_ref, acc_ref, *, mod_row, final):
    x = x_ref[...]
    u = _ada_norm(x, gain_ref[...], mod_ref, mod_row).astype(BF16)
    for c in range(D_FF // FF_CHUNK):
        lo, hi = c * FF_CHUNK, (c + 1) * FF_CHUNK
        a = _dot(u, w13_ref[:, lo:hi])
        b = _dot(u, w13_ref[:, D_FF + lo:D_FF + hi])
        t = _dot((_silu(a) * b).astype(BF16), w2_ref[lo:hi, :])
        if c == 0:
            acc_ref[...] = t
        else:
            acc_ref[...] += t
    gate = mod_ref[0, mod_row + 2:mod_row + 3, :]
    y = x + (0.5 * gate) * acc_ref[...]
    if final:
        y = _rms(y, fin_ref[...])
    o_ref[...] = y


def _ffn(x, mod, tokens_per_group, gain, w13, w2, fin, *, mod_row, final):
    n = x.shape[0]
    tile = pl.BlockSpec((TOKEN_TILE, D_MODEL), lambda i: (i, 0))
    return pl.pallas_call(
        functools.partial(_ffn_kernel, mod_row=mod_row, final=final),
        grid=(n // TOKEN_TILE,),
        in_specs=[tile, _mod_spec(tokens_per_group), _const_spec((1, D_MODEL)),
                  _const_spec((D_MODEL, 2 * D_FF)), _const_spec((D_FF, D_MODEL)),
                  _const_spec((1, D_MODEL))],
        out_specs=tile,
        out_shape=jax.ShapeDtypeStruct((n, D_MODEL), F32),
        scratch_shapes=[pltpu.VMEM((TOKEN_TILE, D_MODEL), F32)],
        compiler_params=_params(1),
    )(x, mod, gain, w13, w2, fin)


def _rope(x, cos, sin_signed):
    first = (lax.broadcasted_iota(jnp.int32, (x.shape[0], LANE), 1) % 16) < 8
    outs = []
    for j in range(x.shape[1] // LANE):
        xj = x[:, j * LANE:(j + 1) * LANE]
        partner = jnp.where(first, pltpu.roll(xj, LANE - 8, 1), pltpu.roll(xj, 8, 1))
        outs.append(xj * cos + partner * sin_signed)
    return outs[0] if len(outs) == 1 else jnp.concatenate(outs, axis=1)


def _log_sigmoid(z):
    return jnp.minimum(z, 0.0) - jnp.log1p(jnp.exp(-jnp.abs(z)))


def _proj_kernel(*refs, rope):
    (x_ref, mod_ref, gain_ref, wp_ref, qn_ref, wuq_ref, kvn_ref, wg_ref, bg_ref) = refs[:9]
    refs = refs[9:]
    if rope:
        ch_ref, sh_ref, cf_ref, sf_ref = refs[:4]
        refs = refs[4:]
    (q_o, ckv_o, kr_o, qb_o, kb_o, vb_o, qc_o, kc_o, vc_o, rc_o, gf_o, gb_o) = refs

    u = _ada_norm(x_ref[...], gain_ref[...], mod_ref, 3).astype(BF16)

    def seg(name):
        lo, hi = SEG[name]
        return _dot(u, wp_ref[:, lo:hi])

    q = _dot(_rms(seg("cq"), qn_ref[...]).astype(BF16), wuq_ref[...])
    kr = seg("krope")
    qb = seg("qb")
    kb = seg("kb")
    if rope:
        q = _rope(q, ch_ref[...], sh_ref[...])
        kr = _rope(kr, ch_ref[...], sh_ref[...])
        qb = _rope(qb, cf_ref[...], sf_ref[...])
        kb = _rope(kb, cf_ref[...], sf_ref[...])
    q_o[...] = (q * (DN_A + DR_A) ** -0.5).astype(BF16)
    ckv_o[...] = _rms(seg("ckv"), kvn_ref[...])
    kr_o[...] = kr
    qb_o[...] = (qb * DH_B ** -0.5).astype(BF16)
    kb_o[...] = kb
    vb_o[...] = seg("vb")
    qc_o[...] = seg("qc") * (DK_C ** -0.5)
    kc_o[...] = seg("kc")
    vc_o[...] = seg("vc")
    rc_o[...] = seg("rc")
    z = _dot(seg("gl").astype(BF16), wg_ref[...]) + bg_ref[...]
    g = _log_sigmoid(z) / GATE_NORM
    gf_o[...] = g[:, :QK_C_PAD]
    gb_o[...] = g[:, QK_C_PAD:]


PROJ_OUT = (("q", H_A * HEAD_PAD, BF16), ("ckv", KV_RANK, F32), ("kr", LANE, F32), ("qb", 256, BF16),
            ("kb", 256, F32), ("vb", 256, F32), ("qc", QK_C_PAD, F32), ("kc", QK_C_PAD, F32),
            ("vc", WIDTH_C, F32), ("rc", WIDTH_C, F32), ("gf", QK_C_PAD, F32), ("gb", QK_C_PAD, F32))


def _project(x, mod, tokens_per_group, lw, rope_tabs):
    n = x.shape[0]
    rope = rope_tabs is not None
    row = lambda w: pl.BlockSpec((TOKEN_TILE, w), lambda i: (i, 0))
    in_specs = [row(D_MODEL), _mod_spec(tokens_per_group), _const_spec((1, D_MODEL)),
                _const_spec((D_MODEL, PROJ_PAD)), _const_spec((1, Q_RANK)),
                _const_spec((Q_RANK, H_A * HEAD_PAD)), _const_spec((1, KV_RANK)),
                _const_spec((LANE, 2 * QK_C_PAD)), _const_spec((1, 2 * QK_C_PAD))]
    args = [x, mod, lw["norm_mix"], lw["wp"], lw["q_norm"], lw["wuq"], lw["kv_norm"], lw["wg"], lw["bg"]]
    if rope:
        n_pos = rope_tabs[0].shape[0]
        tab = pl.BlockSpec((TOKEN_TILE, LANE), lambda i: (i % (n_pos // TOKEN_TILE), 0))
        in_specs += [tab] * 4
        args += list(rope_tabs)
    outs = pl.pallas_call(
        functools.partial(_proj_kernel, rope=rope),
        grid=(n // TOKEN_TILE,),
        in_specs=in_specs,
        out_specs=[row(w) for _, w, _ in PROJ_OUT],
        out_shape=[jax.ShapeDtypeStruct((n, w), dt) for _, w, dt in PROJ_OUT],
        compiler_params=_params(1),
    )(*args)
    return {name: o for (name, _, _), o in zip(PROJ_OUT, outs)}


def _lane_mask(width, lo, hi):
    lane = lax.broadcasted_iota(jnp.int32, (1, width), 1)
    return jnp.where((lane >= lo) & (lane < hi), 1.0, 0.0).astype(F32)


def _attend_t(units, dv):
    outs = []
    q, keys, vt_ext = units[0]()
    s = _dot_nt(q, keys)
    for u in range(len(units)):
        s_now, vt_now = s, vt_ext
        if u + 1 < len(units):
            q, keys, vt_ext = units[u + 1]()
            s = _dot_nt(q, keys)
        e = jnp.exp(s_now - jnp.max(s_now, axis=-1, keepdims=True)).astype(BF16)
        r = _dot_nt(vt_now, e)
        outs.append(r[0:dv] * (1.0 / r[dv:dv + 1]))
    return outs


def _mla_kernel(*refs, n_cache):
    if n_cache:
        q_ref, ckv_ref, kr_ref, cckv_ref, ckr_ref, wuk_ref, wuvt_ref, o_ref, kf_scr, vt_scr = refs
    else:
        q_ref, ckv_ref, kr_ref, wuk_ref, wuvt_ref, o_ref, kf_scr, vt_scr = refs

    @pl.when(pl.program_id(1) == 0)
    def _():
        def fill(row0, ckv, kr):
            rows = ckv.shape[0]
            cb = ckv.astype(BF16)
            kr_all = jnp.concatenate([kr] * H_A, axis=1)
            kf_scr[row0:row0 + rows, :] = (_dot(cb, wuk_ref[...]) + kr_all).astype(BF16)
            vt = _dot_nt(wuvt_ref[...], cb)
            for h in range(H_A):
                vt_scr[h, 0:DV_A, row0:row0 + rows] = vt[h * DV_A:(h + 1) * DV_A].astype(BF16)

        if n_cache:
            fill(0, cckv_ref[...], ckr_ref[...])
        fill(n_cache, ckv_ref[...], kr_ref[...])
        for h in range(H_A):
            vt_scr[h, DV_A:, :] = jnp.ones((ONES_ROWS, vt_scr.shape[2]), BF16)

    def unit(h):
        sl = slice(h * HEAD_PAD, (h + 1) * HEAD_PAD)
        return lambda: (q_ref[:, sl], kf_scr[:, sl], vt_scr[h])

    heads = _attend_t([unit(h) for h in range(H_A)], DV_A)
    o_ref[...] = jnp.concatenate(heads, axis=0).T


def _mla(p, lw, n_batch, n_tok, cache, layer):
    n_cache = 0 if cache is None else cache[0].shape[2]
    nqt = n_tok // Q_TILE
    in_specs = [pl.BlockSpec((Q_TILE, H_A * HEAD_PAD), lambda b, i: (b * nqt + i, 0)),
                pl.BlockSpec((n_tok, KV_RANK), lambda b, i: (b, 0)),
                pl.BlockSpec((n_tok, LANE), lambda b, i: (b, 0))]
    args = [p["q"], p["ckv"], p["kr"]]
    if n_cache:
        in_specs += [pl.BlockSpec((None, None, n_cache, KV_RANK), lambda b, i: (b, layer, 0, 0)),
                     pl.BlockSpec((None, None, n_cache, LANE), lambda b, i: (b, layer, 0, 0))]
        args += list(cache)
    in_specs += [_const_spec((KV_RANK, H_A * HEAD_PAD)), _const_spec((WIDTH_A, KV_RANK))]
    args += [lw["wuk"], lw["wuvt"]]
    n_keys = n_cache + n_tok
    return pl.pallas_call(
        functools.partial(_mla_kernel, n_cache=n_cache),
        grid=(n_batch, nqt),
        in_specs=in_specs,
        out_specs=pl.BlockSpec((Q_TILE, WIDTH_A), lambda b, i: (b * nqt + i, 0)),
        out_shape=jax.ShapeDtypeStruct((n_batch * n_tok, WIDTH_A), F32),
        scratch_shapes=[pltpu.VMEM((n_keys, H_A * HEAD_PAD), BF16),
                        pltpu.VMEM((H_A, DV_A + ONES_ROWS, n_keys), BF16)],
        compiler_params=_params(2),
    )(*args)


def _diff_kernel(*refs, n_cache, lam_init):
    if n_cache:
        q_ref, k_ref, v_ref, ck_ref, cv_ref, lam_ref, o_ref, k_scr, vt_scr = refs
    else:
        q_ref, k_ref, v_ref, lam_ref, o_ref, k_scr, vt_scr = refs

    @pl.when(pl.program_id(1) == 0)
    def _():
        def fill(row0, k, v):
            rows = k.shape[0]
            k_scr[row0:row0 + rows, :] = k.astype(BF16)
            vt = v.T
            for h in range(H_B):
                vt_scr[h, 0:DV_B, row0:row0 + rows] = vt[h * DV_B:(h + 1) * DV_B].astype(BF16)

        if n_cache:
            fill(0, ck_ref[...], cv_ref[...])
        fill(n_cache, k_ref[...], v_ref[...])
        for h in range(H_B):
            vt_scr[h, DV_B:, :] = jnp.ones((ONES_ROWS, vt_scr.shape[2]), BF16)

    lv = lam_ref[...]
    lam = (jnp.exp(jnp.sum(lv[0:1] * lv[1:2], axis=-1, keepdims=True))
           - jnp.exp(jnp.sum(lv[2:3] * lv[3:4], axis=-1, keepdims=True)) + lam_init)
    lane = lax.broadcasted_iota(jnp.int32, (Q_TILE, LANE), 1)

    def unit(h, j):
        blk = slice((h // 2) * LANE, (h // 2 + 1) * LANE)
        lo = (h % 2) * 2 * DH_B + j * DH_B

        def fn():
            qh = q_ref[:, blk]
            qm = jnp.where((lane >= lo) & (lane < lo + DH_B), qh, jnp.zeros_like(qh))
            return qm, k_scr[:, blk], vt_scr[h]
        return fn

    maps = _attend_t([unit(h, j) for h in range(H_B) for j in range(2)], DV_B)
    heads = [maps[2 * h] - lam * maps[2 * h + 1] for h in range(H_B)]
    o_ref[...] = jnp.concatenate(heads, axis=0).T


def _diff(p, lw, n_batch, n_tok, cache, layer, lam_init):
    n_cache = 0 if cache is None else cache[0].shape[2]
    nqt = n_tok // Q_TILE
    in_specs = [pl.BlockSpec((Q_TILE, WIDTH_B), lambda b, i: (b * nqt + i, 0)),
                pl.BlockSpec((n_tok, WIDTH_B), lambda b, i: (b, 0)),
                pl.BlockSpec((n_tok, WIDTH_B), lambda b, i: (b, 0))]
    args = [p["qb"], p["kb"], p["vb"]]
    if n_cache:
        in_specs += [pl.BlockSpec((None, None, n_cache, WIDTH_B), lambda b, i: (b, layer, 0, 0))] * 2
        args += list(cache)
    in_specs.append(_const_spec((4, DH_B)))
    args.append(lw["lam"])
    n_keys = n_cache + n_tok
    return pl.pallas_call(
        functools.partial(_diff_kernel, n_cache=n_cache, lam_init=lam_init),
        grid=(n_batch, nqt),
        in_specs=in_specs,
        out_specs=pl.BlockSpec((Q_TILE, WIDTH_B), lambda b, i: (b * nqt + i, 0)),
        out_shape=jax.ShapeDtypeStruct((n_batch * n_tok, WIDTH_B), F32),
        scratch_shapes=[pltpu.VMEM((n_keys, WIDTH_B), BF16),
                        pltpu.VMEM((H_B, DV_B + ONES_ROWS, n_keys), BF16)],
        compiler_params=_params(2),
    )(*args)


def _split3(x):
    hi = x.astype(BF16)
    r1 = x - hi.astype(F32)
    mid = r1.astype(BF16)
    lo = (r1 - mid.astype(F32)).astype(BF16)
    return hi, mid, lo


def _dot_exact_lhs(a_bf16, x):
    hi, mid, lo = _split3(x)
    return _dot(a_bf16, hi) + _dot(a_bf16, mid) + _dot(a_bf16, lo)


def _gla_kernel(*refs, n_tok, n_seq, has_init, emit_state):
    q_ref, k_ref, v_ref, gf_ref, gb_ref = refs[:5]
    refs = refs[5:]
    if has_init:
        s0f_ref, s0b_ref = refs[:2]
        refs = refs[2:]
    o_ref = refs[0]
    refs = refs[1:]
    if emit_state:
        sf_ref, sb_ref = refs[:2]
        refs = refs[2:]
    st_scr, = refs

    n_chunks = n_tok // CHUNK
    n_groups = n_chunks // GLA_UNROLL
    t_idx = lax.broadcasted_iota(jnp.int32, (CHUNK, CHUNK), 0)
    s_idx = lax.broadcasted_iota(jnp.int32, (CHUNK, CHUNK), 1)
    tri = [jnp.where(s_idx <= t_idx, 1.0, 0.0).astype(F32), jnp.where(s_idx >= t_idx, 1.0, 0.0).astype(F32)]
    tri4 = [jnp.concatenate([t] * H_C, axis=1) for t in tri]
    tri_b = [t.astype(BF16) for t in tri]
    head_k = [_lane_mask(QK_C_PAD, h * DK_C, (h + 1) * DK_C) for h in range(H_C)]
    head_v = [_lane_mask(WIDTH_C, h * DV_C, (h + 1) * DV_C) for h in range(H_C)]
    row_v = lax.broadcasted_iota(jnp.int32, (WIDTH_C, QK_C_PAD), 0)
    col_k = lax.broadcasted_iota(jnp.int32, (WIDTH_C, QK_C_PAD), 1)
    diag = jnp.zeros((WIDTH_C, QK_C_PAD), F32)
    for h in range(H_C):
        inside = ((row_v >= h * DV_C) & (row_v < (h + 1) * DV_C)
                  & (col_k >= h * DK_C) & (col_k < (h + 1) * DK_C))
        diag = jnp.where(inside, 1.0, diag)

    for s in range(n_seq):
        for d, s0_ref in enumerate((s0f_ref, s0b_ref) if has_init else (None, None)):
            st_scr[s, d] = jnp.zeros((WIDTH_C, QK_C_PAD), F32) if s0_ref is None else s0_ref[s]
    o_ref[...] = jnp.zeros_like(o_ref)

    def chunk_steps(chains):
        loaded = []
        for s, d, c in chains:
            g_ref = gb_ref if d else gf_ref
            row0 = s * n_tok + c * CHUNK
            rows = pl.ds(row0 if isinstance(row0, int) else pl.multiple_of(row0, CHUNK), CHUNK)
            bcum = _dot_exact_lhs(tri_b[d], g_ref[rows, :])
            loaded.append((rows, q_ref[rows, :], k_ref[rows, :], v_ref[rows, :], bcum))
        scaled = []
        for (s, d, c), (rows, q, k, v, bcum) in zip(chains, loaded):
            blast = bcum[0:1, :] if d else bcum[CHUNK - 1:CHUNK, :]
            qe = (q * jnp.exp(bcum)).astype(BF16)
            ke = k * jnp.exp(-bcum)
            kd = (k * jnp.exp(blast - bcum)).astype(BF16)
            ke_heads = jnp.concatenate([ke * head_k[h] for h in range(H_C)], axis=0).astype(BF16)
            v_heads = jnp.concatenate([v * head_v[h] for h in range(H_C)], axis=0).astype(BF16)
            scaled.append((qe, ke_heads, kd, v.astype(BF16), v_heads, jnp.exp(blast)))
        products = []
        for (s, d, c), (qe, ke_heads, kd, vb, v_heads, decay) in zip(chains, scaled):
            st = st_scr[s, d]
            attn = _dot_nt(qe, ke_heads)
            inter = _dot_nt(qe, st.astype(BF16))
            upd = _dot_tn(vb, kd)
            products.append((st, attn, inter, upd))
        masked = []
        for (s, d, c), (qe, ke_heads, kd, vb, v_heads, decay), (st, attn, inter, upd) in zip(
                chains, scaled, products):
            st_scr[s, d] = st * decay + upd * diag
            masked.append((attn * tri4[d]).astype(BF16))
        for (rows, *_), (_, _, _, _, v_heads, _), (_, _, inter, _), attn_b in zip(
                loaded, scaled, products, masked):
            o_ref[rows, :] += _dot(attn_b, v_heads) + inter

    def group(i):
        for j in range(GLA_UNROLL):
            fwd = i * GLA_UNROLL + j
            chunk_steps([(s, d, n_chunks - 1 - fwd if d else fwd) for s in range(n_seq) for d in (0, 1)])

    if n_groups == 1:
        group(0)
    else:
        def body(i, carry):
            group(i)
            return carry
        lax.fori_loop(0, n_groups, body, 0)

    if emit_state:
        for s in range(n_seq):
            sf_ref[s] = st_scr[s, 0]
            sb_ref[s] = st_scr[s, 1]


def _gla(p, n_batch, n_tok, init, emit_state):
    n_seq = GLA_SEQS
    tok = lambda w: pl.BlockSpec((n_seq * n_tok, w), lambda b: (b, 0))
    st_spec = pl.BlockSpec((n_seq, WIDTH_C, QK_C_PAD), lambda b: (b, 0, 0))
    in_specs = [tok(QK_C_PAD), tok(QK_C_PAD), tok(WIDTH_C), tok(QK_C_PAD), tok(QK_C_PAD)]
    args = [p["qc"], p["kc"], p["vc"], p["gf"], p["gb"]]
    if init is not None:
        in_specs += [st_spec] * 2
        args += list(init)
    out_specs = [tok(WIDTH_C)]
    out_shape = [jax.ShapeDtypeStruct((n_batch * n_tok, WIDTH_C), F32)]
    if emit_state:
        out_specs += [st_spec] * 2
        out_shape += [jax.ShapeDtypeStruct((n_batch, WIDTH_C, QK_C_PAD), F32)] * 2
    return pl.pallas_call(
        functools.partial(_gla_kernel, n_tok=n_tok, n_seq=n_seq, has_init=init is not None,
                          emit_state=emit_state),
        grid=(n_batch // n_seq,),
        in_specs=in_specs,
        out_specs=out_specs,
        out_shape=out_shape,
        scratch_shapes=[pltpu.VMEM((n_seq, 2, WIDTH_C, QK_C_PAD), F32)],
        compiler_params=_params(1),
    )(*args)


def _merge_kernel(x_ref, mod_ref, oa_ref, ob_ref, oc_ref, rc_ref, na_ref, nb_ref, nc_ref,
                  gb_ref, gc_ref, wo_ref, o_ref, *, lam_init):
    a = _rms(oa_ref[...], na_ref[...])
    ob = ob_ref[...]
    b = ob * lax.rsqrt(_dot_f32(ob * ob, gb_ref[...]) + EPS) * nb_ref[...] * (1.0 - lam_init)
    oc = oc_ref[...]
    c = oc * lax.rsqrt(_dot_f32(oc * oc, gc_ref[...]) + EPS) * nc_ref[...] * _silu(rc_ref[...])
    mixed = (_dot(a.astype(BF16), wo_ref[0:WIDTH_A, :])
             + _dot(b.astype(BF16), wo_ref[WIDTH_A:WIDTH_A + WIDTH_B, :])
             + _dot(c.astype(BF16), wo_ref[WIDTH_A + WIDTH_B:, :]))
    o_ref[...] = x_ref[...] + mod_ref[0, 5:6, :] * mixed


def _merge(x, mod, tokens_per_group, oa, ob, oc, rc, lw, consts, lam_init):
    n = x.shape[0]
    row = lambda w: pl.BlockSpec((TOKEN_TILE, w), lambda i: (i, 0))
    return pl.pallas_call(
        functools.partial(_merge_kernel, lam_init=lam_init),
        grid=(n // TOKEN_TILE,),
        in_specs=[row(D_MODEL), _mod_spec(tokens_per_group), row(WIDTH_A), row(WIDTH_B), row(WIDTH_C),
                  row(WIDTH_C), _const_spec((1, WIDTH_A)), _const_spec((1, WIDTH_B)),
                  _const_spec((1, WIDTH_C)), _const_spec((WIDTH_B, WIDTH_B)),
                  _const_spec((WIDTH_C, WIDTH_C)), _const_spec((D_MODEL, D_MODEL))],
        out_specs=row(D_MODEL),
        out_shape=jax.ShapeDtypeStruct((n, D_MODEL), F32),
        compiler_params=_params(1),
    )(x, mod, oa, ob, oc, rc, lw["out_norm_a"], lw["out_norm_b"], lw["out_norm_c"],
      consts["group_b"], consts["group_c"], lw["wo"])


def _place(dst_width, pieces):
    cols = []
    for arr, width in pieces:
        pad = width - arr.shape[-1]
        cols.append(jnp.pad(arr, ((0, 0), (0, pad))) if pad else arr)
    out = jnp.concatenate(cols, axis=-1)
    assert out.shape[-1] == dst_width
    return out


def _layer_weights(l, w):
    w_in = w["w_in"][l]
    offs = [0]
    for width in (Q_RANK, KV_RANK, DR_A, 2 * H_B * DH_B, 2 * H_B * DH_B, H_B * DV_B, H_C * DK_C, H_C * DK_C,
                  WIDTH_C, WIDTH_C, GATE_RANK, GATE_RANK):
        offs.append(offs[-1] + width)
    col = lambda i: w_in[:, offs[i]:offs[i + 1]]
    zeros = lambda n: jnp.zeros((D_MODEL, n), F32)
    krope = jnp.concatenate([zeros(ROPE_OFF), col(2), zeros(LANE - ROPE_OFF - DR_A)], axis=1)
    gates = jnp.concatenate([col(10), col(11)], axis=1)
    wp = _place(PROJ_PAD, [(col(0), Q_RANK), (col(1), KV_RANK), (krope, LANE), (col(3), 256), (col(4), 256),
                           (col(5), 256), (col(6), QK_C_PAD), (col(7), QK_C_PAD), (col(8), WIDTH_C),
                           (col(9), WIDTH_C), (gates, LANE)])

    def pad_heads(m, width, lane0=0):
        m = m.reshape(m.shape[0], H_A, width)
        m = jnp.pad(m, ((0, 0), (0, 0), (lane0, HEAD_PAD - lane0 - width)))
        return m.reshape(m.shape[0], H_A * HEAD_PAD)

    wg = jnp.zeros((LANE, 2 * QK_C_PAD), F32)
    wg = wg.at[0:GATE_RANK, 0:H_C * DK_C].set(w["gla_wg_f"][l])
    wg = wg.at[GATE_RANK:2 * GATE_RANK, QK_C_PAD:QK_C_PAD + H_C * DK_C].set(w["gla_wg_b"][l])
    bg = jnp.zeros((1, 2 * QK_C_PAD), F32)
    bg = bg.at[0, 0:H_C * DK_C].set(w["gla_bg_f"][l])
    bg = bg.at[0, QK_C_PAD:QK_C_PAD + H_C * DK_C].set(w["gla_bg_b"][l])
    row = lambda v: v.reshape(1, -1)
    return {
        "norm_ffn1": row(w["norm_ffn1"][l]), "ffn1_w13": w["ffn1_w13"][l].astype(BF16),
        "ffn1_w2": w["ffn1_w2"][l].astype(BF16),
        "norm_ffn2": row(w["norm_ffn2"][l]), "ffn2_w13": w["ffn2_w13"][l].astype(BF16),
        "ffn2_w2": w["ffn2_w2"][l].astype(BF16),
        "norm_mix": row(w["norm_mix"][l]), "wp": wp.astype(BF16),
        "q_norm": row(w["mla_q_norm"][l]),
        "wuq": pad_heads(w["mla_w_uq"][l], DN_A + DR_A).astype(BF16),
        "kv_norm": row(w["mla_kv_norm"][l]),
        "wuk": pad_heads(w["mla_w_uk"][l], DN_A).astype(BF16),
        "wuvt": w["mla_w_uv"][l].T.astype(BF16),
        "wg": wg.astype(BF16), "bg": bg,
        "lam": jnp.stack([w["diff_lq1"][l], w["diff_lk1"][l], w["diff_lq2"][l], w["diff_lk2"][l]]),
        "out_norm_a": row(w["mla_out_norm"][l]),
        "out_norm_b": row(jnp.tile(w["diff_norm"][l], H_B)),
        "out_norm_c": row(jnp.tile(w["gla_norm"][l], H_C)),
        "wo": w["w_out"][l].astype(BF16),
    }


def _rope_tables(n_pos):
    pos = jnp.arange(n_pos)
    rows = (pos // GRID_W).astype(F32)
    cols = (pos % GRID_W).astype(F32)
    half = DH_B // 2
    inv = ROPE_BASE ** (-jnp.arange(0, half, 2, dtype=F32) / half)
    ang = jnp.concatenate([rows[:, None] * inv, rows[:, None] * inv,
                           cols[:, None] * inv, cols[:, None] * inv], axis=1)
    sign = jnp.tile(jnp.concatenate([-jnp.ones(8, F32), jnp.ones(8, F32)]), 2)
    cos32, sin32 = jnp.cos(ang), jnp.sin(ang) * sign
    cos_full, sin_full = jnp.tile(cos32, (1, LANE // 32)), jnp.tile(sin32, (1, LANE // 32))
    ones = lambda n: jnp.ones((n_pos, n), F32)
    zeros = lambda n: jnp.zeros((n_pos, n), F32)
    cos_head = jnp.concatenate([ones(ROPE_OFF), cos32, ones(LANE - ROPE_OFF - DR_A)], axis=1)
    sin_head = jnp.concatenate([zeros(ROPE_OFF), sin32, zeros(LANE - ROPE_OFF - DR_A)], axis=1)
    return cos_head, sin_head, cos_full, sin_full


def _group_mean_matrix(width, group):
    idx = jnp.arange(width) // group
    return jnp.where(idx[:, None] == idx[None, :], 1.0 / group, 0.0).astype(F32)


def _state_to_blockdiag_t(s):
    out = jnp.zeros((s.shape[0], WIDTH_C, QK_C_PAD), F32)
    for h in range(H_C):
        out = out.at[:, h * DV_C:(h + 1) * DV_C, h * DK_C:(h + 1) * DK_C].set(jnp.swapaxes(s[:, h], 1, 2))
    return out


def _blockdiag_t_to_state(st):
    return jnp.stack([jnp.swapaxes(st[:, h * DV_C:(h + 1) * DV_C, h * DK_C:(h + 1) * DK_C], 1, 2)
                      for h in range(H_C)], axis=1)


def kernel(x_prompt, x_sample, cache_mla_ckv, cache_mla_krope, cache_diff_k, cache_diff_v, state_gla_fwd, state_gla_bwd, c, c_ctx, w_mod, b_mod, norm_ffn1, ffn1_w13, ffn1_w2, norm_mix, w_in, mla_q_norm, mla_w_uq, mla_kv_norm, mla_w_uk, mla_w_uv, mla_out_norm, diff_lq1, diff_lk1, diff_lq2, diff_lk2, diff_norm, gla_wg_f, gla_bg_f, gla_wg_b, gla_bg_b, gla_norm, w_out, norm_ffn2, ffn2_w13, ffn2_w2, final_norm):
    w = dict(norm_ffn1=norm_ffn1, ffn1_w13=ffn1_w13, ffn1_w2=ffn1_w2, norm_mix=norm_mix, w_in=w_in,
             mla_q_norm=mla_q_norm, mla_w_uq=mla_w_uq, mla_kv_norm=mla_kv_norm, mla_w_uk=mla_w_uk,
             mla_w_uv=mla_w_uv, mla_out_norm=mla_out_norm, diff_lq1=diff_lq1, diff_lk1=diff_lk1,
             diff_lq2=diff_lq2, diff_lk2=diff_lk2, diff_norm=diff_norm, gla_wg_f=gla_wg_f,
             gla_bg_f=gla_bg_f, gla_wg_b=gla_wg_b, gla_bg_b=gla_bg_b, gla_norm=gla_norm, w_out=w_out,
             norm_ffn2=norm_ffn2, ffn2_w13=ffn2_w13, ffn2_w2=ffn2_w2)
    n_ctx_b, n_ctx_t, _ = x_prompt.shape
    n_lat_b, n_lat_t, _ = x_sample.shape
    n_past = cache_mla_ckv.shape[2]

    cvec = jnp.concatenate([c_ctx[None, :], c, jnp.zeros((16 - 1 - n_lat_b, D_MODEL), F32)], axis=0)
    mod = _modulation(cvec, w_mod, b_mod).reshape(DEPTH, 16, N_MOD, D_MODEL)

    rope_tabs = _rope_tables(n_lat_t)
    consts = {"group_b": _group_mean_matrix(WIDTH_B, DV_B), "group_c": _group_mean_matrix(WIDTH_C, DV_C)}
    cache_kr = jnp.pad(cache_mla_krope, ((0, 0), (0, 0), (0, 0), (ROPE_OFF, LANE - ROPE_OFF - DR_A)))
    cache_dk = cache_diff_k.reshape(n_lat_b, DEPTH, n_past, WIDTH_B)
    cache_dv = cache_diff_v.reshape(n_lat_b, DEPTH, n_past, WIDTH_B)
    fin = final_norm.reshape(1, D_MODEL)

    xp = x_prompt.reshape(n_ctx_b * n_ctx_t, D_MODEL)
    xs = x_sample.reshape(n_lat_b * n_lat_t, D_MODEL)
    new = {k: [] for k in ("ckv", "kr", "dk", "dv", "sf", "sb")}
    for l in range(DEPTH):
        lw = _layer_weights(l, w)
        lam_init = 0.8 - 0.6 * math.exp(-0.3 * l)
        last = l == DEPTH - 1
        mod_ctx, mod_lat = mod[l, 0:1], mod[l, 1:1 + n_lat_b]

        def trunk(x, m, tpg, n_batch, n_tok, rope, mla_cache, diff_cache, gla_init, emit_state):
            x = _ffn(x, m, tpg, lw["norm_ffn1"], lw["ffn1_w13"], lw["ffn1_w2"], fin, mod_row=0, final=False)
            p = _project(x, m, tpg, lw, rope)
            oa = _mla(p, lw, n_batch, n_tok, mla_cache, l)
            ob = _diff(p, lw, n_batch, n_tok, diff_cache, l, lam_init)
            gla_out = _gla(p, n_batch, n_tok, gla_init, emit_state)
            x = _merge(x, m, tpg, oa, ob, gla_out[0], p["rc"], lw, consts, lam_init)
            x = _ffn(x, m, tpg, lw["norm_ffn2"], lw["ffn2_w13"], lw["ffn2_w2"], fin, mod_row=6, final=last)
            return x, p, gla_out

        xp, p, gla_out = trunk(xp, mod_ctx, n_ctx_b * n_ctx_t, n_ctx_b, n_ctx_t, None, None, None, None, True)
        new["ckv"].append(p["ckv"].reshape(n_ctx_b, n_ctx_t, KV_RANK))
        new["kr"].append(p["kr"][:, ROPE_OFF:ROPE_OFF + DR_A].reshape(n_ctx_b, n_ctx_t, DR_A))
        new["dk"].append(p["kb"].reshape(n_ctx_b, n_ctx_t, H_B, 2 * DH_B))
        new["dv"].append(p["vb"].reshape(n_ctx_b, n_ctx_t, H_B, DV_B))
        new["sf"].append(_blockdiag_t_to_state(gla_out[1]))
        new["sb"].append(_blockdiag_t_to_state(gla_out[2]))

        gla_init = (_state_to_blockdiag_t(state_gla_fwd[:, l]), _state_to_blockdiag_t(state_gla_bwd[:, l]))
        xs, _, _ = trunk(xs, mod_lat, n_lat_t, n_lat_b, n_lat_t, rope_tabs, (cache_mla_ckv, cache_kr),
                         (cache_dk, cache_dv), gla_init, False)

    stack = lambda name: jnp.stack(new[name], axis=1)
    return (xp.reshape(x_prompt.shape), xs.reshape(x_sample.shape), stack("ckv"), stack("kr"),
            stack("dk"), stack("dv"), stack("sf"), stack("sb"))
```

```python
import functools
import math

import jax
import jax.numpy as jnp
from jax import lax
from jax.experimental import pallas as pl
from jax.experimental.pallas import tpu as pltpu

F32 = jnp.float32
BF16 = jnp.bfloat16

D_MODEL = 1024
DEPTH = 2
GRID_W = 64
ROPE_BASE = 10000.0
EPS = 1e-6
H_A, DN_A, DR_A, DV_A = 6, 64, 32, 64
Q_RANK, KV_RANK = 384, 256
H_B, DH_B, DV_B = 4, 32, 64
H_C, DK_C, DV_C = 4, 48, 96
GATE_RANK = 16
GATE_NORM = 16.0
CHUNK = 64
WIDTH_A = H_A * DV_A
WIDTH_B = H_B * DV_B
WIDTH_C = H_C * DV_C
D_FF = 2816
N_MOD = 9

LANE = 128
FF_CHUNK = 256
TOKEN_TILE = 512
Q_TILE = 256
KEY_BLOCK = 256
ONES_ROWS = 16
MOD_TILE = 1152
GLA_UNROLL = 4
GLA_SEQS = 2
VMEM_LIMIT = 56 * 1024 * 1024

HEAD_PAD = LANE
ROPE_OFF = DN_A
QK_C_PAD = 256
SEG = {}
_off = 0
for _name, _w in (("cq", Q_RANK), ("ckv", KV_RANK), ("krope", LANE), ("qb", 256), ("kb", 256),
                  ("vb", 256), ("qc", QK_C_PAD), ("kc", QK_C_PAD), ("vc", WIDTH_C), ("rc", WIDTH_C),
                  ("gl", LANE)):
    SEG[_name] = (_off, _off + _w)
    _off += _w
PROJ_PAD = _off

PROJ_SPLITS = (Q_RANK, KV_RANK, DR_A, 2 * H_B * DH_B, 2 * H_B * DH_B, H_B * DV_B, H_C * DK_C, H_C * DK_C,
               WIDTH_C, WIDTH_C, GATE_RANK, GATE_RANK)
PROJ_COLS = sum(PROJ_SPLITS)
_SRC = [sum(PROJ_SPLITS[:i]) for i in range(len(PROJ_SPLITS) + 1)]


def _src(i, j=None):
    return ("src", _SRC[i], _SRC[(i if j is None else j) + 1])


PROJ_LAYOUT = (
    ("cq", (_src(0),)), ("ckv", (_src(1),)),
    ("krope", (("zero", 0, ROPE_OFF), _src(2), ("zero", 0, LANE - ROPE_OFF - DR_A))),
    ("qb", (_src(3),)), ("kb", (_src(4),)), ("vb", (_src(5),)),
    ("qc", (_src(6), ("zero", 0, QK_C_PAD - H_C * DK_C))),
    ("kc", (_src(7), ("zero", 0, QK_C_PAD - H_C * DK_C))),
    ("vc", (_src(8),)), ("rc", (_src(9),)),
    ("gl", (_src(10, 11), ("zero", 0, LANE - 2 * GATE_RANK))),
)


def _dot(a, b):
    return jnp.dot(a, b, preferred_element_type=F32)


def _dot_nt(a, b):
    return lax.dot_general(a, b, (((1,), (1,)), ((), ())), preferred_element_type=F32)


def _dot_tn(a, b):
    return lax.dot_general(a, b, (((0,), (0,)), ((), ())), preferred_element_type=F32)


def _dot_f32(a, b):
    return jnp.dot(a, b, preferred_element_type=F32, precision=lax.Precision.HIGHEST)


def _rms(x, gain):
    ms = jnp.mean(x * x, axis=-1, keepdims=True)
    return (x * lax.rsqrt(ms + EPS)) * gain


def _silu(x):
    return x * jax.nn.sigmoid(x)


def _const_spec(shape):
    nd = len(shape)
    return pl.BlockSpec(shape, lambda *_: (0,) * nd)


def _layer_spec(shape, layer, buffers=None):
    nd = len(shape)
    mode = {} if buffers is None else {"pipeline_mode": pl.Buffered(buffers)}
    return pl.BlockSpec((None,) + tuple(shape), lambda *_: (layer,) + (0,) * nd, **mode)


def _params(n_axes):
    return pltpu.CompilerParams(dimension_semantics=("arbitrary",) * n_axes,
                                vmem_limit_bytes=VMEM_LIMIT)


def _mod_kernel(c_ref, w_ref, b_ref, o_ref):
    s = _silu(c_ref[...]).astype(BF16)
    o_ref[0] = _dot(s, w_ref[0].astype(BF16)) + b_ref[0]


def _modulation(cvec, w_mod, b_mod):
    n_rows = cvec.shape[0]
    width = N_MOD * D_MODEL
    return pl.pallas_call(
        _mod_kernel,
        grid=(DEPTH, width // MOD_TILE),
        in_specs=[
            pl.BlockSpec((n_rows, D_MODEL), lambda l, j: (0, 0)),
            pl.BlockSpec((1, D_MODEL, MOD_TILE), lambda l, j: (l, 0, j)),
            pl.BlockSpec((1, 1, MOD_TILE), lambda l, j: (l, 0, j)),
        ],
        out_specs=pl.BlockSpec((1, n_rows, MOD_TILE), lambda l, j: (l, 0, j)),
        out_shape=jax.ShapeDtypeStruct((DEPTH, n_rows, width), F32),
        compiler_params=_params(2),
    )(cvec, w_mod, b_mod.reshape(DEPTH, 1, width))


def _mod_spec(tokens_per_group):
    return pl.BlockSpec((1, N_MOD, D_MODEL), lambda i: ((i * TOKEN_TILE) // tokens_per_group, 0, 0))


def _ada_norm(x, gain, mod_ref, first_row):
    shift = mod_ref[0, first_row:first_row + 1, :]
    scale = mod_ref[0, first_row + 1:first_row + 2, :]
    return _rms(x, gain) * (1.0 + scale) + shift


def _ffn_kernel(x_ref, mod_ref, gain_ref, w13_ref, w2_ref, fin_ref, o_ref, acc_ref, *, mod_row, final):
    x = x_ref[...]
    u = _ada_norm(x, gain_ref[...], mod_ref, mod_row).astype(BF16)
    for c in range(D_FF // FF_CHUNK):
        lo, hi = c * FF_CHUNK, (c + 1) * FF_CHUNK
        a = _dot(u, w13_ref[:, lo:hi])
        b = _dot(u, w13_ref[:, D_FF + lo:D_FF + hi])
        t = _dot((_silu(a) * b).astype(BF16), w2_ref[lo:hi, :])
        if c == 0:
            acc_ref[...] = t
        else:
            acc_ref[...] += t
    gate = mod_ref[0, mod_row + 2:mod_row + 3, :]
    y = x + (0.5 * gate) * acc_ref[...]
    if final:
        y = _rms(y, fin_ref[...])
    o_ref[...] = y


def _ffn(x, mod, tokens_per_group, gain, w13, w2, fin, layer, *, mod_row, final):
    n = x.shape[0]
    tile = pl.BlockSpec((TOKEN_TILE, D_MODEL), lambda i: (i, 0))
    return pl.pallas_call(
        functools.partial(_ffn_kernel, mod_row=mod_row, final=final),
        grid=(n // TOKEN_TILE,),
        in_specs=[tile, _mod_spec(tokens_per_group), _layer_spec((1, D_MODEL), layer),
                  _layer_spec((D_MODEL, 2 * D_FF), layer, 1), _layer_spec((D_FF, D_MODEL), layer, 1),
                  _const_spec((1, D_MODEL))],
        out_specs=tile,
        out_shape=jax.ShapeDtypeStruct((n, D_MODEL), F32),
        scratch_shapes=[pltpu.VMEM((TOKEN_TILE, D_MODEL), F32)],
        compiler_params=_params(1),
    )(x, mod, gain, w13, w2, fin)


def _rope(x, cos, sin_signed):
    first = (lax.broadcasted_iota(jnp.int32, (x.shape[0], LANE), 1) % 16) < 8
    outs = []
    for j in range(x.shape[1] // LANE):
        xj = x[:, j * LANE:(j + 1) * LANE]
        partner = jnp.where(first, pltpu.roll(xj, LANE - 8, 1), pltpu.roll(xj, 8, 1))
        outs.append(xj * cos + partner * sin_signed)
    return outs[0] if len(outs) == 1 else jnp.concatenate(outs, axis=1)


def _log_sigmoid(z):
    return jnp.minimum(z, 0.0) - jnp.log1p(jnp.exp(-jnp.abs(z)))


def _proj_kernel(*refs, rope):
    (x_ref, mod_ref, gain_ref, win_ref, qn_ref, wuq_ref, kvn_ref, wg_ref, bg_ref) = refs[:9]
    refs = refs[9:]
    if rope:
        ch_ref, sh_ref, cf_ref, sf_ref = refs[:4]
        refs = refs[4:]
    (q_o, ckv_o, kr_o, qb_o, kb_o, vb_o, qc_o, kc_o, vc_o, rc_o, gf_o, gb_o, wp_ref) = refs

    @pl.when(pl.program_id(0) == 0)
    def _():
        rows = 256
        for r0 in range(0, D_MODEL, rows):
            for name, pieces in PROJ_LAYOUT:
                parts = []
                for kind, a, b in pieces:
                    parts.append(jnp.zeros((rows, b - a), F32) if kind == "zero" else win_ref[r0:r0 + rows, a:b])
                blk = parts[0] if len(parts) == 1 else jnp.concatenate(parts, axis=1)
                lo, hi = SEG[name]
                wp_ref[r0:r0 + rows, lo:hi] = blk.astype(BF16)

    u = _ada_norm(x_ref[...], gain_ref[...], mod_ref, 3).astype(BF16)

    def seg(name):
        lo, hi = SEG[name]
        return _dot(u, wp_ref[:, lo:hi])

    q = _dot(_rms(seg("cq"), qn_ref[...]).astype(BF16), wuq_ref[...])
    kr = seg("krope")
    qb = seg("qb")
    kb = seg("kb")
    if rope:
        q = _rope(q, ch_ref[...], sh_ref[...])
        kr = _rope(kr, ch_ref[...], sh_ref[...])
        qb = _rope(qb, cf_ref[...], sf_ref[...])
        kb = _rope(kb, cf_ref[...], sf_ref[...])
    q_o[...] = (q * (DN_A + DR_A) ** -0.5).astype(BF16)
    ckv_o[...] = _rms(seg("ckv"), kvn_ref[...])
    kr_o[...] = kr
    qb_o[...] = (qb * DH_B ** -0.5).astype(BF16)
    kb_o[...] = kb
    vb_o[...] = seg("vb")
    qc_o[...] = seg("qc") * (DK_C ** -0.5)
    kc_o[...] = seg("kc")
    vc_o[...] = seg("vc")
    rc_o[...] = seg("rc")
    z = _dot(seg("gl").astype(BF16), wg_ref[...]) + bg_ref[...]
    g = _log_sigmoid(z) / GATE_NORM
    gf_o[...] = g[:, :QK_C_PAD]
    gb_o[...] = g[:, QK_C_PAD:]


PROJ_OUT = (("q", H_A * HEAD_PAD, BF16), ("ckv", KV_RANK, F32), ("kr", LANE, F32), ("qb", 256, BF16),
            ("kb", 256, F32), ("vb", 256, F32), ("qc", QK_C_PAD, F32), ("kc", QK_C_PAD, F32),
            ("vc", WIDTH_C, F32), ("rc", WIDTH_C, F32), ("gf", QK_C_PAD, F32), ("gb", QK_C_PAD, F32))


def _project(x, mod, tokens_per_group, lw, w_in, layer, rope_tabs):
    n = x.shape[0]
    rope = rope_tabs is not None
    row = lambda w: pl.BlockSpec((TOKEN_TILE, w), lambda i: (i, 0))
    in_specs = [row(D_MODEL), _mod_spec(tokens_per_group), _layer_spec((1, D_MODEL), layer),
                _layer_spec((D_MODEL, PROJ_COLS), layer, 1), _layer_spec((1, Q_RANK), layer),
                _layer_spec((Q_RANK, H_A * HEAD_PAD), layer), _layer_spec((1, KV_RANK), layer),
                _layer_spec((LANE, 2 * QK_C_PAD), layer), _layer_spec((1, 2 * QK_C_PAD), layer)]
    args = [x, mod, lw["norm_mix"], w_in, lw["q_norm"], lw["wuq"], lw["kv_norm"], lw["wg"], lw["bg"]]
    if rope:
        n_pos = rope_tabs[0].shape[0]
        tab = pl.BlockSpec((TOKEN_TILE, LANE), lambda i: (i % (n_pos // TOKEN_TILE), 0))
        in_specs += [tab] * 4
        args += list(rope_tabs)
    outs = pl.pallas_call(
        functools.partial(_proj_kernel, rope=rope),
        grid=(n // TOKEN_TILE,),
        in_specs=in_specs,
        out_specs=[row(w) for _, w, _ in PROJ_OUT],
        out_shape=[jax.ShapeDtypeStruct((n, w), dt) for _, w, dt in PROJ_OUT],
        scratch_shapes=[pltpu.VMEM((D_MODEL, PROJ_PAD), BF16)],
        compiler_params=_params(1),
    )(*args)
    return {name: o for (name, _, _), o in zip(PROJ_OUT, outs)}


def _lane_mask(width, lo, hi):
    lane = lax.broadcasted_iota(jnp.int32, (1, width), 1)
    return jnp.where((lane >= lo) & (lane < hi), 1.0, 0.0).astype(F32)


def _attend_t(units, dv):
    outs = []
    q, keys, vt_ext = units[0]()
    s = _dot_nt(q, keys)
    for u in range(len(units)):
        s_now, vt_now = s, vt_ext
        if u + 1 < len(units):
            q, keys, vt_ext = units[u + 1]()
            s = _dot_nt(q, keys)
        e = jnp.exp(s_now - jnp.max(s_now, axis=-1, keepdims=True)).astype(BF16)
        r = _dot_nt(vt_now, e)
        outs.append(r[0:dv] * (1.0 / r[dv:dv + 1]))
    return outs


def _mla_kernel(*refs, n_cache):
    if n_cache:
        q_ref, ckv_ref, kr_ref, cckv_ref, ckr_ref, wuk_ref, wuvt_ref, o_ref, kf_scr, vt_scr = refs
    else:
        q_ref, ckv_ref, kr_ref, wuk_ref, wuvt_ref, o_ref, kf_scr, vt_scr = refs

    @pl.when(pl.program_id(1) == 0)
    def _():
        def fill(row0, ckv, kr):
            rows = ckv.shape[0]
            cb = ckv.astype(BF16)
            kr_all = jnp.concatenate([kr] * H_A, axis=1)
            kf_scr[row0:row0 + rows, :] = (_dot(cb, wuk_ref[...]) + kr_all).astype(BF16)
            vt = _dot_nt(wuvt_ref[...], cb)
            for h in range(H_A):
                vt_scr[h, 0:DV_A, row0:row0 + rows] = vt[h * DV_A:(h + 1) * DV_A].astype(BF16)

        if n_cache:
            fill(0, cckv_ref[...], ckr_ref[...])
        fill(n_cache, ckv_ref[...], kr_ref[...])
        for h in range(H_A):
            vt_scr[h, DV_A:, :] = jnp.ones((ONES_ROWS, vt_scr.shape[2]), BF16)

    def unit(h):
        sl = slice(h * HEAD_PAD, (h + 1) * HEAD_PAD)
        return lambda: (q_ref[:, sl], kf_scr[:, sl], vt_scr[h])

    heads = _attend_t([unit(h) for h in range(H_A)], DV_A)
    o_ref[...] = jnp.concatenate(heads, axis=0).T


def _mla(p, lw, n_batch, n_tok, cache, layer):
    n_cache = 0 if cache is None else cache[0].shape[2]
    nqt = n_tok // Q_TILE
    in_specs = [pl.BlockSpec((Q_TILE, H_A * HEAD_PAD), lambda b, i: (b * nqt + i, 0)),
                pl.BlockSpec((n_tok, KV_RANK), lambda b, i: (b, 0)),
                pl.BlockSpec((n_tok, LANE), lambda b, i: (b, 0))]
    args = [p["q"], p["ckv"], p["kr"]]
    if n_cache:
        in_specs += [pl.BlockSpec((None, None, n_cache, KV_RANK), lambda b, i: (b, layer, 0, 0)),
                     pl.BlockSpec((None, None, n_cache, LANE), lambda b, i: (b, layer, 0, 0))]
        args += list(cache)
    in_specs += [_layer_spec((KV_RANK, H_A * HEAD_PAD), layer), _layer_spec((WIDTH_A, KV_RANK), layer)]
    args += [lw["wuk"], lw["wuvt"]]
    n_keys = n_cache + n_tok
    return pl.pallas_call(
        functools.partial(_mla_kernel, n_cache=n_cache),
        grid=(n_batch, nqt),
        in_specs=in_specs,
        out_specs=pl.BlockSpec((Q_TILE, WIDTH_A), lambda b, i: (b * nqt + i, 0)),
        out_shape=jax.ShapeDtypeStruct((n_batch * n_tok, WIDTH_A), F32),
        scratch_shapes=[pltpu.VMEM((n_keys, H_A * HEAD_PAD), BF16),
                        pltpu.VMEM((H_A, DV_A + ONES_ROWS, n_keys), BF16)],
        compiler_params=_params(2),
    )(*args)


def _diff_kernel(*refs, n_cache, lam_init):
    if n_cache:
        q_ref, k_ref, v_ref, ck_ref, cv_ref, lam_ref, o_ref, k_scr, vt_scr = refs
    else:
        q_ref, k_ref, v_ref, lam_ref, o_ref, k_scr, vt_scr = refs

    @pl.when(pl.program_id(1) == 0)
    def _():
        def fill(row0, k, v):
            rows = k.shape[0]
            k_scr[row0:row0 + rows, :] = k.astype(BF16)
            vt = v.T
            for h in range(H_B):
                vt_scr[h, 0:DV_B, row0:row0 + rows] = vt[h * DV_B:(h + 1) * DV_B].astype(BF16)

        if n_cache:
            fill(0, ck_ref[...], cv_ref[...])
        fill(n_cache, k_ref[...], v_ref[...])
        for h in range(H_B):
            vt_scr[h, DV_B:, :] = jnp.ones((ONES_ROWS, vt_scr.shape[2]), BF16)

    lv = lam_ref[...]
    lam = (jnp.exp(jnp.sum(lv[0:1] * lv[1:2], axis=-1, keepdims=True))
           - jnp.exp(jnp.sum(lv[2:3] * lv[3:4], axis=-1, keepdims=True)) + lam_init)
    lane = lax.broadcasted_iota(jnp.int32, (Q_TILE, LANE), 1)

    def unit(h, j):
        blk = slice((h // 2) * LANE, (h // 2 + 1) * LANE)
        lo = (h % 2) * 2 * DH_B + j * DH_B

        def fn():
            qh = q_ref[:, blk]
            qm = jnp.where((lane >= lo) & (lane < lo + DH_B), qh, jnp.zeros_like(qh))
            return qm, k_scr[:, blk], vt_scr[h]
        return fn

    maps = _attend_t([unit(h, j) for h in range(H_B) for j in range(2)], DV_B)
    heads = [maps[2 * h] - lam * maps[2 * h + 1] for h in range(H_B)]
    o_ref[...] = jnp.concatenate(heads, axis=0).T


def _diff(p, lw, n_batch, n_tok, cache, layer, lam_init):
    n_cache = 0 if cache is None else cache[0].shape[2]
    nqt = n_tok // Q_TILE
    in_specs = [pl.BlockSpec((Q_TILE, WIDTH_B), lambda b, i: (b * nqt + i, 0)),
                pl.BlockSpec((n_tok, WIDTH_B), lambda b, i: (b, 0)),
                pl.BlockSpec((n_tok, WIDTH_B), lambda b, i: (b, 0))]
    args = [p["qb"], p["kb"], p["vb"]]
    if n_cache:
        in_specs += [pl.BlockSpec((None, None, n_cache, WIDTH_B), lambda b, i: (b, layer, 0, 0))] * 2
        args += list(cache)
    in_specs.append(_layer_spec((4, DH_B), layer))
    args.append(lw["lam"])
    n_keys = n_cache + n_tok
    return pl.pallas_call(
        functools.partial(_diff_kernel, n_cache=n_cache, lam_init=lam_init),
        grid=(n_batch, nqt),
        in_specs=in_specs,
        out_specs=pl.BlockSpec((Q_TILE, WIDTH_B), lambda b, i: (b * nqt + i, 0)),
        out_shape=jax.ShapeDtypeStruct((n_batch * n_tok, WIDTH_B), F32),
        scratch_shapes=[pltpu.VMEM((n_keys, WIDTH_B), BF16),
                        pltpu.VMEM((H_B, DV_B + ONES_ROWS, n_keys), BF16)],
        compiler_params=_params(2),
    )(*args)


def _split3(x):
    hi = x.astype(BF16)
    r1 = x - hi.astype(F32)
    mid = r1.astype(BF16)
    lo = (r1 - mid.astype(F32)).astype(BF16)
    return hi, mid, lo


def _dot_exact_lhs(a_bf16, x):
    hi, mid, lo = _split3(x)
    return _dot(a_bf16, hi) + _dot(a_bf16, mid) + _dot(a_bf16, lo)


def _gla_kernel(*refs, n_tok, n_seq, has_init, emit_state):
    q_ref, k_ref, v_ref, gf_ref, gb_ref = refs[:5]
    refs = refs[5:]
    if has_init:
        s0f_ref, s0b_ref = refs[:2]
        refs = refs[2:]
    o_ref = refs[0]
    refs = refs[1:]
    if emit_state:
        sf_ref, sb_ref = refs[:2]
        refs = refs[2:]
    st_scr, = refs

    n_chunks = n_tok // CHUNK
    n_groups = n_chunks // GLA_UNROLL
    t_idx = lax.broadcasted_iota(jnp.int32, (CHUNK, CHUNK), 0)
    s_idx = lax.broadcasted_iota(jnp.int32, (CHUNK, CHUNK), 1)
    tri = [jnp.where(s_idx <= t_idx, 1.0, 0.0).astype(F32), jnp.where(s_idx >= t_idx, 1.0, 0.0).astype(F32)]
    tri4 = [jnp.concatenate([t] * H_C, axis=1) for t in tri]
    tri_b = [t.astype(BF16) for t in tri]
    head_k = [_lane_mask(QK_C_PAD, h * DK_C, (h + 1) * DK_C) for h in range(H_C)]
    head_v = [_lane_mask(WIDTH_C, h * DV_C, (h + 1) * DV_C) for h in range(H_C)]
    row_v = lax.broadcasted_iota(jnp.int32, (WIDTH_C, QK_C_PAD), 0)
    col_k = lax.broadcasted_iota(jnp.int32, (WIDTH_C, QK_C_PAD), 1)
    diag = jnp.zeros((WIDTH_C, QK_C_PAD), F32)
    for h in range(H_C):
        inside = ((row_v >= h * DV_C) & (row_v < (h + 1) * DV_C)
                  & (col_k >= h * DK_C) & (col_k < (h + 1) * DK_C))
        diag = jnp.where(inside, 1.0, diag)

    for s in range(n_seq):
        for d, s0_ref in enumerate((s0f_ref, s0b_ref) if has_init else (None, None)):
            st_scr[s, d] = jnp.zeros((WIDTH_C, QK_C_PAD), F32) if s0_ref is None else s0_ref[s]
    o_ref[...] = jnp.zeros_like(o_ref)

    def chunk_steps(chains):
        loaded = []
        for s, d, c in chains:
            g_ref = gb_ref if d else gf_ref
            row0 = s * n_tok + c * CHUNK
            rows = pl.ds(row0 if isinstance(row0, int) else pl.multiple_of(row0, CHUNK), CHUNK)
            bcum = _dot_exact_lhs(tri_b[d], g_ref[rows, :])
            loaded.append((rows, q_ref[rows, :], k_ref[rows, :], v_ref[rows, :], bcum))
        scaled = []
        for (s, d, c), (rows, q, k, v, bcum) in zip(chains, loaded):
            blast = bcum[0:1, :] if d else bcum[CHUNK - 1:CHUNK, :]
            qe = (q * jnp.exp(bcum)).astype(BF16)
            ke = k * jnp.exp(-bcum)
            kd = (k * jnp.exp(blast - bcum)).astype(BF16)
            ke_heads = jnp.concatenate([ke * head_k[h] for h in range(H_C)], axis=0).astype(BF16)
            v_heads = jnp.concatenate([v * head_v[h] for h in range(H_C)], axis=0).astype(BF16)
            scaled.append((qe, ke_heads, kd, v.astype(BF16), v_heads, jnp.exp(blast)))
        products = []
        for (s, d, c), (qe, ke_heads, kd, vb, v_heads, decay) in zip(chains, scaled):
            st = st_scr[s, d]
            attn = _dot_nt(qe, ke_heads)
            inter = _dot_nt(qe, st.astype(BF16))
            upd = _dot_tn(vb, kd)
            products.append((st, attn, inter, upd))
        masked = []
        for (s, d, c), (qe, ke_heads, kd, vb, v_heads, decay), (st, attn, inter, upd) in zip(
                chains, scaled, products):
            st_scr[s, d] = st * decay + upd * diag
            masked.append((attn * tri4[d]).astype(BF16))
        for (rows, *_), (_, _, _, _, v_heads, _), (_, _, inter, _), attn_b in zip(
                loaded, scaled, products, masked):
            o_ref[rows, :] += _dot(attn_b, v_heads) + inter

    def group(i):
        for j in range(GLA_UNROLL):
            fwd = i * GLA_UNROLL + j
            chunk_steps([(s, d, n_chunks - 1 - fwd if d else fwd) for s in range(n_seq) for d in (0, 1)])

    if n_groups == 1:
        group(0)
    else:
        def body(i, carry):
            group(i)
            return carry
        lax.fori_loop(0, n_groups, body, 0)

    if emit_state:
        for s in range(n_seq):
            sf_ref[s] = st_scr[s, 0]
            sb_ref[s] = st_scr[s, 1]


def _gla(p, n_batch, n_tok, init, emit_state):
    n_seq = GLA_SEQS
    tok = lambda w: pl.BlockSpec((n_seq * n_tok, w), lambda b: (b, 0))
    st_spec = pl.BlockSpec((n_seq, WIDTH_C, QK_C_PAD), lambda b: (b, 0, 0))
    in_specs = [tok(QK_C_PAD), tok(QK_C_PAD), tok(WIDTH_C), tok(QK_C_PAD), tok(QK_C_PAD)]
    args = [p["qc"], p["kc"], p["vc"], p["gf"], p["gb"]]
    if init is not None:
        in_specs += [st_spec] * 2
        args += list(init)
    out_specs = [tok(WIDTH_C)]
    out_shape = [jax.ShapeDtypeStruct((n_batch * n_tok, WIDTH_C), F32)]
    if emit_state:
        out_specs += [st_spec] * 2
        out_shape += [jax.ShapeDtypeStruct((n_batch, WIDTH_C, QK_C_PAD), F32)] * 2
    return pl.pallas_call(
        functools.partial(_gla_kernel, n_tok=n_tok, n_seq=n_seq, has_init=init is not None,
                          emit_state=emit_state),
        grid=(n_batch // n_seq,),
        in_specs=in_specs,
        out_specs=out_specs,
        out_shape=out_shape,
        scratch_shapes=[pltpu.VMEM((n_seq, 2, WIDTH_C, QK_C_PAD), F32)],
        compiler_params=_params(1),
    )(*args)


def _merge_kernel(x_ref, mod_ref, oa_ref, ob_ref, oc_ref, rc_ref, na_ref, nb_ref, nc_ref,
                  gb_ref, gc_ref, wo_ref, o_ref, *, lam_init):
    a = _rms(oa_ref[...], na_ref[...])
    ob = ob_ref[...]
    b = ob * lax.rsqrt(_dot_f32(ob * ob, gb_ref[...]) + EPS) * nb_ref[...] * (1.0 - lam_init)
    oc = oc_ref[...]
    c = oc * lax.rsqrt(_dot_f32(oc * oc, gc_ref[...]) + EPS) * nc_ref[...] * _silu(rc_ref[...])
    mixed = (_dot(a.astype(BF16), wo_ref[0:WIDTH_A, :])
             + _dot(b.astype(BF16), wo_ref[WIDTH_A:WIDTH_A + WIDTH_B, :])
             + _dot(c.astype(BF16), wo_ref[WIDTH_A + WIDTH_B:, :]))
    o_ref[...] = x_ref[...] + mod_ref[0, 5:6, :] * mixed


def _merge(x, mod, tokens_per_group, oa, ob, oc, rc, lw, consts, layer, lam_init):
    n = x.shape[0]
    row = lambda w: pl.BlockSpec((TOKEN_TILE, w), lambda i: (i, 0))
    return pl.pallas_call(
        functools.partial(_merge_kernel, lam_init=lam_init),
        grid=(n // TOKEN_TILE,),
        in_specs=[row(D_MODEL), _mod_spec(tokens_per_group), row(WIDTH_A), row(WIDTH_B), row(WIDTH_C),
                  row(WIDTH_C), _layer_spec((1, WIDTH_A), layer), _layer_spec((1, WIDTH_B), layer),
                  _layer_spec((1, WIDTH_C), layer), _const_spec((WIDTH_B, WIDTH_B)),
                  _const_spec((WIDTH_C, WIDTH_C)), _layer_spec((D_MODEL, D_MODEL), layer)],
        out_specs=row(D_MODEL),
        out_shape=jax.ShapeDtypeStruct((n, D_MODEL), F32),
        compiler_params=_params(1),
    )(x, mod, oa, ob, oc, rc, lw["out_norm_a"], lw["out_norm_b"], lw["out_norm_c"],
      consts["group_b"], consts["group_c"], lw["wo"])


def _prepare_weights(w):
    n_layers = w["w_in"].shape[0]
    row = lambda v: v.reshape(n_layers, 1, -1)

    def pad_heads(m, width):
        m = m.reshape(n_layers, m.shape[1], H_A, width)
        m = jnp.pad(m, ((0, 0), (0, 0), (0, 0), (0, HEAD_PAD - width)))
        return m.reshape(n_layers, m.shape[1], H_A * HEAD_PAD)

    def gate_block(wg, first):
        lo = 0 if first else QK_C_PAD
        return jnp.pad(wg, ((0, 0), (0, 0), (lo, 2 * QK_C_PAD - lo - H_C * DK_C)))

    wg = jnp.concatenate([gate_block(w["gla_wg_f"], True), gate_block(w["gla_wg_b"], False),
                          jnp.zeros((n_layers, LANE - 2 * GATE_RANK, 2 * QK_C_PAD), F32)], axis=1)
    bg = gate_block(w["gla_bg_f"][:, None, :], True) + gate_block(w["gla_bg_b"][:, None, :], False)
    return {
        "norm_ffn1": row(w["norm_ffn1"]), "ffn1_w13": w["ffn1_w13"].astype(BF16),
        "ffn1_w2": w["ffn1_w2"].astype(BF16),
        "norm_ffn2": row(w["norm_ffn2"]), "ffn2_w13": w["ffn2_w13"].astype(BF16),
        "ffn2_w2": w["ffn2_w2"].astype(BF16),
        "norm_mix": row(w["norm_mix"]), "w_in": w["w_in"],
        "q_norm": row(w["mla_q_norm"]),
        "wuq": pad_heads(w["mla_w_uq"], DN_A + DR_A).astype(BF16),
        "kv_norm": row(w["mla_kv_norm"]),
        "wuk": pad_heads(w["mla_w_uk"], DN_A).astype(BF16),
        "wuvt": jnp.swapaxes(w["mla_w_uv"], 1, 2).astype(BF16),
        "wg": wg.astype(BF16), "bg": bg,
        "lam": jnp.stack([w["diff_lq1"], w["diff_lk1"], w["diff_lq2"], w["diff_lk2"]], axis=1),
        "out_norm_a": row(w["mla_out_norm"]),
        "out_norm_b": row(jnp.tile(w["diff_norm"], (1, H_B))),
        "out_norm_c": row(jnp.tile(w["gla_norm"], (1, H_C))),
        "wo": w["w_out"].astype(BF16),
    }


def _rope_tables(n_pos):
    pos = jnp.arange(n_pos)
    rows = (pos // GRID_W).astype(F32)
    cols = (pos % GRID_W).astype(F32)
    half = DH_B // 2
    inv = ROPE_BASE ** (-jnp.arange(0, half, 2, dtype=F32) / half)
    ang = jnp.concatenate([rows[:, None] * inv, rows[:, None] * inv,
                           cols[:, None] * inv, cols[:, None] * inv], axis=1)
    sign = jnp.tile(jnp.concatenate([-jnp.ones(8, F32), jnp.ones(8, F32)]), 2)
    cos32, sin32 = jnp.cos(ang), jnp.sin(ang) * sign
    cos_full, sin_full = jnp.tile(cos32, (1, LANE // 32)), jnp.tile(sin32, (1, LANE // 32))
    ones = lambda n: jnp.ones((n_pos, n), F32)
    zeros = lambda n: jnp.zeros((n_pos, n), F32)
    cos_head = jnp.concatenate([ones(ROPE_OFF), cos32, ones(LANE - ROPE_OFF - DR_A)], axis=1)
    sin_head = jnp.concatenate([zeros(ROPE_OFF), sin32, zeros(LANE - ROPE_OFF - DR_A)], axis=1)
    return cos_head, sin_head, cos_full, sin_full


def _group_mean_matrix(width, group):
    idx = jnp.arange(width) // group
    return jnp.where(idx[:, None] == idx[None, :], 1.0 / group, 0.0).astype(F32)


def _state_to_blockdiag_t(s):
    st = jnp.swapaxes(s, 2, 3)
    blocks = [jnp.pad(st[:, h], ((0, 0), (0, 0), (h * DK_C, QK_C_PAD - (h + 1) * DK_C))) for h in range(H_C)]
    return jnp.concatenate(blocks, axis=1)


def _blockdiag_t_to_state(st):
    return jnp.stack([jnp.swapaxes(st[:, h * DV_C:(h + 1) * DV_C, h * DK_C:(h + 1) * DK_C], 1, 2)
                      for h in range(H_C)], axis=1)


def kernel(x_prompt, x_sample, cache_mla_ckv, cache_mla_krope, cache_diff_k, cache_diff_v, state_gla_fwd, state_gla_bwd, c, c_ctx, w_mod, b_mod, norm_ffn1, ffn1_w13, ffn1_w2, norm_mix, w_in, mla_q_norm, mla_w_uq, mla_kv_norm, mla_w_uk, mla_w_uv, mla_out_norm, diff_lq1, diff_lk1, diff_lq2, diff_lk2, diff_norm, gla_wg_f, gla_bg_f, gla_wg_b, gla_bg_b, gla_norm, w_out, norm_ffn2, ffn2_w13, ffn2_w2, final_norm):
    w = dict(norm_ffn1=norm_ffn1, ffn1_w13=ffn1_w13, ffn1_w2=ffn1_w2, norm_mix=norm_mix, w_in=w_in,
             mla_q_norm=mla_q_norm, mla_w_uq=mla_w_uq, mla_kv_norm=mla_kv_norm, mla_w_uk=mla_w_uk,
             mla_w_uv=mla_w_uv, mla_out_norm=mla_out_norm, diff_lq1=diff_lq1, diff_lk1=diff_lk1,
             diff_lq2=diff_lq2, diff_lk2=diff_lk2, diff_norm=diff_norm, gla_wg_f=gla_wg_f,
             gla_bg_f=gla_bg_f, gla_wg_b=gla_wg_b, gla_bg_b=gla_bg_b, gla_norm=gla_norm, w_out=w_out,
             norm_ffn2=norm_ffn2, ffn2_w13=ffn2_w13, ffn2_w2=ffn2_w2)
    n_ctx_b, n_ctx_t, _ = x_prompt.shape
    n_lat_b, n_lat_t, _ = x_sample.shape
    n_past = cache_mla_ckv.shape[2]

    cvec = jnp.concatenate([c_ctx[None, :], c, jnp.zeros((16 - 1 - n_lat_b, D_MODEL), F32)], axis=0)
    mod = _modulation(cvec, w_mod, b_mod).reshape(DEPTH, 16, N_MOD, D_MODEL)

    rope_tabs = _rope_tables(n_lat_t)
    consts = {"group_b": _group_mean_matrix(WIDTH_B, DV_B), "group_c": _group_mean_matrix(WIDTH_C, DV_C)}
    cache_kr = jnp.pad(cache_mla_krope, ((0, 0), (0, 0), (0, 0), (ROPE_OFF, LANE - ROPE_OFF - DR_A)))
    cache_dk = cache_diff_k.reshape(n_lat_b, DEPTH, n_past, WIDTH_B)
    cache_dv = cache_diff_v.reshape(n_lat_b, DEPTH, n_past, WIDTH_B)
    fin = final_norm.reshape(1, D_MODEL)

    xp = x_prompt.reshape(n_ctx_b * n_ctx_t, D_MODEL)
    xs = x_sample.reshape(n_lat_b * n_lat_t, D_MODEL)
    new = {k: [] for k in ("ckv", "kr", "dk", "dv", "sf", "sb")}
    lw = _prepare_weights(w)
    for l in range(DEPTH):
        lam_init = 0.8 - 0.6 * math.exp(-0.3 * l)
        last = l == DEPTH - 1
        mod_ctx, mod_lat = mod[l, 0:1], mod[l, 1:1 + n_lat_b]

        def trunk(x, m, tpg, n_batch, n_tok, rope, mla_cache, diff_cache, gla_init, emit_state):
            x = _ffn(x, m, tpg, lw["norm_ffn1"], lw["ffn1_w13"], lw["ffn1_w2"], fin, l, mod_row=0, final=False)
            p = _project(x, m, tpg, lw, lw["w_in"], l, rope)
            oa = _mla(p, lw, n_batch, n_tok, mla_cache, l)
            ob = _diff(p, lw, n_batch, n_tok, diff_cache, l, lam_init)
            gla_out = _gla(p, n_batch, n_tok, gla_init, emit_state)
            x = _merge(x, m, tpg, oa, ob, gla_out[0], p["rc"], lw, consts, l, lam_init)
            x = _ffn(x, m, tpg, lw["norm_ffn2"], lw["ffn2_w13"], lw["ffn2_w2"], fin, l, mod_row=6, final=last)
            return x, p, gla_out

        xp, p, gla_out = trunk(xp, mod_ctx, n_ctx_b * n_ctx_t, n_ctx_b, n_ctx_t, None, None, None, None, True)
        new["ckv"].append(p["ckv"].reshape(n_ctx_b, n_ctx_t, KV_RANK))
        new["kr"].append(p["kr"][:, ROPE_OFF:ROPE_OFF + DR_A].reshape(n_ctx_b, n_ctx_t, DR_A))
        new["dk"].append(p["kb"].reshape(n_ctx_b, n_ctx_t, H_B, 2 * DH_B))
        new["dv"].append(p["vb"].reshape(n_ctx_b, n_ctx_t, H_B, DV_B))
        new["sf"].append(_blockdiag_t_to_state(gla_out[1]))
        new["sb"].append(_blockdiag_t_to_state(gla_out[2]))

        gla_init = (_state_to_blockdiag_t(state_gla_fwd[:, l]), _state_to_blockdiag_t(state_gla_bwd[:, l]))
        xs, _, _ = trunk(xs, mod_lat, n_lat_t, n_lat_b, n_lat_t, rope_tabs, (cache_mla_ckv, cache_kr),
                         (cache_dk, cache_dv), gla_init, False)

    stack = lambda name: jnp.stack(new[name], axis=1)
    return (xp.reshape(x_prompt.shape), xs.reshape(x_sample.shape), stack("ckv"), stack("kr"),
            stack("dk"), stack("dv"), stack("sf"), stack("sb"))
```

```python
import functools
import math

import jax
import jax.numpy as jnp
from jax import lax
from jax.experimental import pallas as pl
from jax.experimental.pallas import tpu as pltpu

F32 = jnp.float32
BF16 = jnp.bfloat16

D_MODEL = 1024
DEPTH = 2
GRID_W = 64
ROPE_BASE = 10000.0
EPS = 1e-6
H_A, DN_A, DR_A, DV_A = 6, 64, 32, 64
Q_RANK, KV_RANK = 384, 256
H_B, DH_B, DV_B = 4, 32, 64
H_C, DK_C, DV_C = 4, 48, 96
GATE_RANK = 16
GATE_NORM = 16.0
CHUNK = 64
WIDTH_A = H_A * DV_A
WIDTH_B = H_B * DV_B
WIDTH_C = H_C * DV_C
D_FF = 2816
N_MOD = 9

LANE = 128
FF_CHUNK = 256
TOKEN_TILE = 512
Q_TILE = 256
KEY_BLOCK = 256
ONES_ROWS = 16
MOD_TILE = 1152
GLA_UNROLL = 4
GLA_SEQS = 2
VMEM_LIMIT = 56 * 1024 * 1024

HEAD_PAD = LANE
ROPE_OFF = DN_A
QK_C_PAD = 256
SEG = {}
_off = 0
for _name, _w in (("cq", Q_RANK), ("ckv", KV_RANK), ("krope", LANE), ("qb", 256), ("kb", 256),
                  ("vb", 256), ("qc", QK_C_PAD), ("kc", QK_C_PAD), ("vc", WIDTH_C), ("rc", WIDTH_C),
                  ("gl", LANE)):
    SEG[_name] = (_off, _off + _w)
    _off += _w
PROJ_PAD = _off

PROJ_SPLITS = (Q_RANK, KV_RANK, DR_A, 2 * H_B * DH_B, 2 * H_B * DH_B, H_B * DV_B, H_C * DK_C, H_C * DK_C,
               WIDTH_C, WIDTH_C, GATE_RANK, GATE_RANK)
PROJ_COLS = sum(PROJ_SPLITS)
_SRC = [sum(PROJ_SPLITS[:i]) for i in range(len(PROJ_SPLITS) + 1)]


def _src(i, j=None):
    return ("src", _SRC[i], _SRC[(i if j is None else j) + 1])


PROJ_LAYOUT = (
    ("cq", (_src(0),)), ("ckv", (_src(1),)),
    ("krope", (("zero", 0, ROPE_OFF), _src(2), ("zero", 0, LANE - ROPE_OFF - DR_A))),
    ("qb", (_src(3),)), ("kb", (_src(4),)), ("vb", (_src(5),)),
    ("qc", (_src(6), ("zero", 0, QK_C_PAD - H_C * DK_C))),
    ("kc", (_src(7), ("zero", 0, QK_C_PAD - H_C * DK_C))),
    ("vc", (_src(8),)), ("rc", (_src(9),)),
    ("gl", (_src(10, 11), ("zero", 0, LANE - 2 * GATE_RANK))),
)


def _dot(a, b):
    return jnp.dot(a, b, preferred_element_type=F32)


def _dot_nt(a, b):
    return lax.dot_general(a, b, (((1,), (1,)), ((), ())), preferred_element_type=F32)


def _dot_tn(a, b):
    return lax.dot_general(a, b, (((0,), (0,)), ((), ())), preferred_element_type=F32)


def _rms(x, gain):
    ms = jnp.mean(x * x, axis=-1, keepdims=True)
    return (x * lax.rsqrt(ms + EPS)) * gain


def _silu(x):
    return x * jax.nn.sigmoid(x)


def _const_spec(shape):
    nd = len(shape)
    return pl.BlockSpec(shape, lambda *_: (0,) * nd)


def _layer_spec(shape, layer, buffers=None):
    nd = len(shape)
    mode = {} if buffers is None else {"pipeline_mode": pl.Buffered(buffers)}
    return pl.BlockSpec((None,) + tuple(shape), lambda *_: (layer,) + (0,) * nd, **mode)


def _params(n_axes):
    return pltpu.CompilerParams(dimension_semantics=("arbitrary",) * n_axes,
                                vmem_limit_bytes=VMEM_LIMIT)


def _mod_kernel(c_ref, w_ref, b_ref, o_ref):
    s = _silu(c_ref[...]).astype(BF16)
    o_ref[0] = _dot(s, w_ref[0].astype(BF16)) + b_ref[0]


def _modulation(cvec, w_mod, b_mod):
    n_rows = cvec.shape[0]
    width = N_MOD * D_MODEL
    return pl.pallas_call(
        _mod_kernel,
        grid=(DEPTH, width // MOD_TILE),
        in_specs=[
            pl.BlockSpec((n_rows, D_MODEL), lambda l, j: (0, 0)),
            pl.BlockSpec((1, D_MODEL, MOD_TILE), lambda l, j: (l, 0, j)),
            pl.BlockSpec((1, 1, MOD_TILE), lambda l, j: (l, 0, j)),
        ],
        out_specs=pl.BlockSpec((1, n_rows, MOD_TILE), lambda l, j: (l, 0, j)),
        out_shape=jax.ShapeDtypeStruct((DEPTH, n_rows, width), F32),
        compiler_params=_params(2),
    )(cvec, w_mod, b_mod.reshape(DEPTH, 1, width))


def _mod_spec(tokens_per_group):
    return pl.BlockSpec((1, N_MOD, D_MODEL), lambda i: ((i * TOKEN_TILE) // tokens_per_group, 0, 0))


def _ada_norm(x, gain, mod_ref, first_row):
    shift = mod_ref[0, first_row:first_row + 1, :]
    scale = mod_ref[0, first_row + 1:first_row + 2, :]
    return _rms(x, gain) * (1.0 + scale) + shift


def _ffn_block(x, mod_ref, gain, w13_ref, w2_ref, acc_ref, mod_row):
    u = _ada_norm(x, gain, mod_ref, mod_row).astype(BF16)
    for c in range(D_FF // FF_CHUNK):
        lo, hi = c * FF_CHUNK, (c + 1) * FF_CHUNK
        a = _dot(u, w13_ref[:, lo:hi])
        b = _dot(u, w13_ref[:, D_FF + lo:D_FF + hi])
        t = _dot((_silu(a) * b).astype(BF16), w2_ref[lo:hi, :])
        if c == 0:
            acc_ref[...] = t
        else:
            acc_ref[...] += t
    gate = mod_ref[0, mod_row + 2:mod_row + 3, :]
    return x + (0.5 * gate) * acc_ref[...]


def _ffn_kernel(x_ref, mod_ref, gain_ref, w13_ref, w2_ref, o_ref, acc_ref, *, mod_row):
    o_ref[...] = _ffn_block(x_ref[...], mod_ref, gain_ref[...], w13_ref, w2_ref, acc_ref, mod_row)


def _ffn(x, mod, tokens_per_group, gain, w13, w2, layer, *, mod_row):
    n = x.shape[0]
    tile = pl.BlockSpec((TOKEN_TILE, D_MODEL), lambda i: (i, 0))
    return pl.pallas_call(
        functools.partial(_ffn_kernel, mod_row=mod_row),
        grid=(n // TOKEN_TILE,),
        in_specs=[tile, _mod_spec(tokens_per_group), _layer_spec((1, D_MODEL), layer),
                  _layer_spec((D_MODEL, 2 * D_FF), layer, 1), _layer_spec((D_FF, D_MODEL), layer, 1)],
        out_specs=tile,
        out_shape=jax.ShapeDtypeStruct((n, D_MODEL), F32),
        scratch_shapes=[pltpu.VMEM((TOKEN_TILE, D_MODEL), F32)],
        compiler_params=_params(1),
    )(x, mod, gain, w13, w2)


def _rope(x, cos, sin_signed):
    first = (lax.broadcasted_iota(jnp.int32, (x.shape[0], LANE), 1) % 16) < 8
    outs = []
    for j in range(x.shape[1] // LANE):
        xj = x[:, j * LANE:(j + 1) * LANE]
        partner = jnp.where(first, pltpu.roll(xj, LANE - 8, 1), pltpu.roll(xj, 8, 1))
        outs.append(xj * cos + partner * sin_signed)
    return outs[0] if len(outs) == 1 else jnp.concatenate(outs, axis=1)


def _log_sigmoid(z):
    return jnp.minimum(z, 0.0) - jnp.log1p(jnp.exp(-jnp.abs(z)))


def _proj_kernel(*refs, rope):
    (x_ref, mod_ref, gain_ref, win_ref, qn_ref, wuq_ref, kvn_ref, wg_ref, bg_ref) = refs[:9]
    refs = refs[9:]
    if rope:
        ch_ref, sh_ref, cf_ref, sf_ref = refs[:4]
        refs = refs[4:]
    (q_o, ckv_o, kr_o, qb_o, kb_o, vb_o, qc_o, kc_o, vc_o, rc_o, gf_o, gb_o, wp_ref) = refs

    @pl.when(pl.program_id(0) == 0)
    def _():
        rows = 256
        for r0 in range(0, D_MODEL, rows):
            for name, pieces in PROJ_LAYOUT:
                parts = []
                for kind, a, b in pieces:
                    parts.append(jnp.zeros((rows, b - a), F32) if kind == "zero" else win_ref[r0:r0 + rows, a:b])
                blk = parts[0] if len(parts) == 1 else jnp.concatenate(parts, axis=1)
                lo, hi = SEG[name]
                wp_ref[r0:r0 + rows, lo:hi] = blk.astype(BF16)

    u = _ada_norm(x_ref[...], gain_ref[...], mod_ref, 3).astype(BF16)

    def seg(name):
        lo, hi = SEG[name]
        return _dot(u, wp_ref[:, lo:hi])

    q = _dot(_rms(seg("cq"), qn_ref[...]).astype(BF16), wuq_ref[...])
    kr = seg("krope")
    qb = seg("qb")
    kb = seg("kb")
    if rope:
        q = _rope(q, ch_ref[...], sh_ref[...])
        kr = _rope(kr, ch_ref[...], sh_ref[...])
        qb = _rope(qb, cf_ref[...], sf_ref[...])
        kb = _rope(kb, cf_ref[...], sf_ref[...])
    q_o[...] = (q * (DN_A + DR_A) ** -0.5).astype(BF16)
    ckv_o[...] = _rms(seg("ckv"), kvn_ref[...])
    kr_o[...] = kr
    qb_o[...] = (qb * DH_B ** -0.5).astype(BF16)
    kb_o[...] = kb
    vb_o[...] = seg("vb")
    qc_o[...] = seg("qc") * (DK_C ** -0.5)
    kc_o[...] = seg("kc")
    vc_o[...] = seg("vc")
    rc_o[...] = seg("rc")
    z = _dot(seg("gl").astype(BF16), wg_ref[...]) + bg_ref[...]
    g = _log_sigmoid(z) / GATE_NORM
    gf_o[...] = g[:, :QK_C_PAD]
    gb_o[...] = g[:, QK_C_PAD:]


PROJ_OUT = (("q", H_A * HEAD_PAD, BF16), ("ckv", KV_RANK, F32), ("kr", LANE, F32), ("qb", 256, BF16),
            ("kb", 256, F32), ("vb", 256, F32), ("qc", QK_C_PAD, F32), ("kc", QK_C_PAD, F32),
            ("vc", WIDTH_C, F32), ("rc", WIDTH_C, F32), ("gf", QK_C_PAD, F32), ("gb", QK_C_PAD, F32))


def _project(x, mod, tokens_per_group, lw, w_in, layer, rope_tabs):
    n = x.shape[0]
    rope = rope_tabs is not None
    row = lambda w: pl.BlockSpec((TOKEN_TILE, w), lambda i: (i, 0))
    in_specs = [row(D_MODEL), _mod_spec(tokens_per_group), _layer_spec((1, D_MODEL), layer),
                _layer_spec((D_MODEL, PROJ_COLS), layer, 1), _layer_spec((1, Q_RANK), layer),
                _layer_spec((Q_RANK, H_A * HEAD_PAD), layer), _layer_spec((1, KV_RANK), layer),
                _layer_spec((LANE, 2 * QK_C_PAD), layer), _layer_spec((1, 2 * QK_C_PAD), layer)]
    args = [x, mod, lw["norm_mix"], w_in, lw["q_norm"], lw["wuq"], lw["kv_norm"], lw["wg"], lw["bg"]]
    if rope:
        n_pos = rope_tabs[0].shape[0]
        tab = pl.BlockSpec((TOKEN_TILE, LANE), lambda i: (i % (n_pos // TOKEN_TILE), 0))
        in_specs += [tab] * 4
        args += list(rope_tabs)
    outs = pl.pallas_call(
        functools.partial(_proj_kernel, rope=rope),
        grid=(n // TOKEN_TILE,),
        in_specs=in_specs,
        out_specs=[row(w) for _, w, _ in PROJ_OUT],
        out_shape=[jax.ShapeDtypeStruct((n, w), dt) for _, w, dt in PROJ_OUT],
        scratch_shapes=[pltpu.VMEM((D_MODEL, PROJ_PAD), BF16)],
        compiler_params=_params(1),
    )(*args)
    return {name: o for (name, _, _), o in zip(PROJ_OUT, outs)}


def _lane_mask(width, lo, hi):
    lane = lax.broadcasted_iota(jnp.int32, (1, width), 1)
    return jnp.where((lane >= lo) & (lane < hi), 1.0, 0.0).astype(F32)


def _attend_t(units, dv):
    outs = []
    q, keys, vt_ext = units[0]()
    s = _dot_nt(q, keys)
    for u in range(len(units)):
        s_now, vt_now = s, vt_ext
        if u + 1 < len(units):
            q, keys, vt_ext = units[u + 1]()
            s = _dot_nt(q, keys)
        e = jnp.exp(s_now - jnp.max(s_now, axis=-1, keepdims=True)).astype(BF16)
        r = _dot_nt(vt_now, e)
        outs.append(r[0:dv] * (1.0 / r[dv:dv + 1]))
    return outs


def _mla_kernel(*refs, n_cache):
    if n_cache:
        q_ref, ckv_ref, kr_ref, cckv_ref, ckr_ref, wuk_ref, wuvt_ref, o_ref, kf_scr, vt_scr = refs
    else:
        q_ref, ckv_ref, kr_ref, wuk_ref, wuvt_ref, o_ref, kf_scr, vt_scr = refs

    @pl.when(pl.program_id(1) == 0)
    def _():
        def fill(row0, ckv, kr):
            rows = ckv.shape[0]
            cb = ckv.astype(BF16)
            kr_all = jnp.concatenate([kr] * H_A, axis=1)
            kf_scr[row0:row0 + rows, :] = (_dot(cb, wuk_ref[...]) + kr_all).astype(BF16)
            vt = _dot_nt(wuvt_ref[...], cb)
            for h in range(H_A):
                vt_scr[h, 0:DV_A, row0:row0 + rows] = vt[h * DV_A:(h + 1) * DV_A].astype(BF16)

        if n_cache:
            fill(0, cckv_ref[...], ckr_ref[...])
        fill(n_cache, ckv_ref[...], kr_ref[...])
        for h in range(H_A):
            vt_scr[h, DV_A:, :] = jnp.ones((ONES_ROWS, vt_scr.shape[2]), BF16)

    def unit(h):
        sl = slice(h * HEAD_PAD, (h + 1) * HEAD_PAD)
        return lambda: (q_ref[:, sl], kf_scr[:, sl], vt_scr[h])

    heads = _attend_t([unit(h) for h in range(H_A)], DV_A)
    o_ref[...] = jnp.concatenate(heads, axis=0).T


def _mla(p, lw, n_batch, n_tok, cache, layer):
    n_cache = 0 if cache is None else cache[0].shape[2]
    nqt = n_tok // Q_TILE
    in_specs = [pl.BlockSpec((Q_TILE, H_A * HEAD_PAD), lambda b, i: (b * nqt + i, 0)),
                pl.BlockSpec((n_tok, KV_RANK), lambda b, i: (b, 0)),
                pl.BlockSpec((n_tok, LANE), lambda b, i: (b, 0))]
    args = [p["q"], p["ckv"], p["kr"]]
    if n_cache:
        in_specs += [pl.BlockSpec((None, None, n_cache, KV_RANK), lambda b, i: (b, layer, 0, 0)),
                     pl.BlockSpec((None, None, n_cache, LANE), lambda b, i: (b, layer, 0, 0))]
        args += list(cache)
    in_specs += [_layer_spec((KV_RANK, H_A * HEAD_PAD), layer), _layer_spec((WIDTH_A, KV_RANK), layer)]
    args += [lw["wuk"], lw["wuvt"]]
    n_keys = n_cache + n_tok
    return pl.pallas_call(
        functools.partial(_mla_kernel, n_cache=n_cache),
        grid=(n_batch, nqt),
        in_specs=in_specs,
        out_specs=pl.BlockSpec((Q_TILE, WIDTH_A), lambda b, i: (b * nqt + i, 0)),
        out_shape=jax.ShapeDtypeStruct((n_batch * n_tok, WIDTH_A), F32),
        scratch_shapes=[pltpu.VMEM((n_keys, H_A * HEAD_PAD), BF16),
                        pltpu.VMEM((H_A, DV_A + ONES_ROWS, n_keys), BF16)],
        compiler_params=_params(2),
    )(*args)


def _diff_kernel(*refs, n_cache, lam_init):
    if n_cache:
        q_ref, k_ref, v_ref, ck_ref, cv_ref, lam_ref, o_ref, k_scr, vt_scr = refs
    else:
        q_ref, k_ref, v_ref, lam_ref, o_ref, k_scr, vt_scr = refs

    @pl.when(pl.program_id(1) == 0)
    def _():
        def fill(row0, k, v):
            rows = k.shape[0]
            k_scr[row0:row0 + rows, :] = k.astype(BF16)
            vt = v.T
            for h in range(H_B):
                vt_scr[h, 0:DV_B, row0:row0 + rows] = vt[h * DV_B:(h + 1) * DV_B].astype(BF16)

        if n_cache:
            fill(0, ck_ref[...], cv_ref[...])
        fill(n_cache, k_ref[...], v_ref[...])
        for h in range(H_B):
            vt_scr[h, DV_B:, :] = jnp.ones((ONES_ROWS, vt_scr.shape[2]), BF16)

    lv = lam_ref[...]
    lam = (jnp.exp(jnp.sum(lv[0:1] * lv[1:2], axis=-1, keepdims=True))
           - jnp.exp(jnp.sum(lv[2:3] * lv[3:4], axis=-1, keepdims=True)) + lam_init)
    lane = lax.broadcasted_iota(jnp.int32, (Q_TILE, LANE), 1)

    def unit(h, j):
        blk = slice((h // 2) * LANE, (h // 2 + 1) * LANE)
        lo = (h % 2) * 2 * DH_B + j * DH_B

        def fn():
            qh = q_ref[:, blk]
            qm = jnp.where((lane >= lo) & (lane < lo + DH_B), qh, jnp.zeros_like(qh))
            return qm, k_scr[:, blk], vt_scr[h]
        return fn

    maps = _attend_t([unit(h, j) for h in range(H_B) for j in range(2)], DV_B)
    heads = [maps[2 * h] - lam * maps[2 * h + 1] for h in range(H_B)]
    o_ref[...] = jnp.concatenate(heads, axis=0).T


def _diff(p, lw, n_batch, n_tok, cache, layer, lam_init):
    n_cache = 0 if cache is None else cache[0].shape[2]
    nqt = n_tok // Q_TILE
    in_specs = [pl.BlockSpec((Q_TILE, WIDTH_B), lambda b, i: (b * nqt + i, 0)),
                pl.BlockSpec((n_tok, WIDTH_B), lambda b, i: (b, 0)),
                pl.BlockSpec((n_tok, WIDTH_B), lambda b, i: (b, 0))]
    args = [p["qb"], p["kb"], p["vb"]]
    if n_cache:
        in_specs += [pl.BlockSpec((None, None, n_cache, WIDTH_B), lambda b, i: (b, layer, 0, 0))] * 2
        args += list(cache)
    in_specs.append(_layer_spec((4, DH_B), layer))
    args.append(lw["lam"])
    n_keys = n_cache + n_tok
    return pl.pallas_call(
        functools.partial(_diff_kernel, n_cache=n_cache, lam_init=lam_init),
        grid=(n_batch, nqt),
        in_specs=in_specs,
        out_specs=pl.BlockSpec((Q_TILE, WIDTH_B), lambda b, i: (b * nqt + i, 0)),
        out_shape=jax.ShapeDtypeStruct((n_batch * n_tok, WIDTH_B), F32),
        scratch_shapes=[pltpu.VMEM((n_keys, WIDTH_B), BF16),
                        pltpu.VMEM((H_B, DV_B + ONES_ROWS, n_keys), BF16)],
        compiler_params=_params(2),
    )(*args)


def _split3(x):
    hi = x.astype(BF16)
    r1 = x - hi.astype(F32)
    mid = r1.astype(BF16)
    lo = (r1 - mid.astype(F32)).astype(BF16)
    return hi, mid, lo


def _dot_exact_lhs(a_bf16, x):
    hi, mid, lo = _split3(x)
    return _dot(a_bf16, hi) + _dot(a_bf16, mid) + _dot(a_bf16, lo)


def _gla_kernel(*refs, n_tok, n_seq, has_init, emit_state):
    q_ref, k_ref, v_ref, gf_ref, gb_ref = refs[:5]
    refs = refs[5:]
    if has_init:
        s0f_ref, s0b_ref = refs[:2]
        refs = refs[2:]
    o_ref = refs[0]
    refs = refs[1:]
    if emit_state:
        sf_ref, sb_ref = refs[:2]
        refs = refs[2:]
    st_scr, = refs

    n_chunks = n_tok // CHUNK
    n_groups = n_chunks // GLA_UNROLL
    t_idx = lax.broadcasted_iota(jnp.int32, (CHUNK, CHUNK), 0)
    s_idx = lax.broadcasted_iota(jnp.int32, (CHUNK, CHUNK), 1)
    tri = [jnp.where(s_idx <= t_idx, 1.0, 0.0).astype(F32), jnp.where(s_idx >= t_idx, 1.0, 0.0).astype(F32)]
    tri4 = [jnp.concatenate([t] * H_C, axis=1) for t in tri]
    tri_b = [t.astype(BF16) for t in tri]
    head_k = [_lane_mask(QK_C_PAD, h * DK_C, (h + 1) * DK_C) for h in range(H_C)]
    head_v = [_lane_mask(WIDTH_C, h * DV_C, (h + 1) * DV_C) for h in range(H_C)]
    row_v = lax.broadcasted_iota(jnp.int32, (WIDTH_C, QK_C_PAD), 0)
    col_k = lax.broadcasted_iota(jnp.int32, (WIDTH_C, QK_C_PAD), 1)
    diag = jnp.zeros((WIDTH_C, QK_C_PAD), F32)
    for h in range(H_C):
        inside = ((row_v >= h * DV_C) & (row_v < (h + 1) * DV_C)
                  & (col_k >= h * DK_C) & (col_k < (h + 1) * DK_C))
        diag = jnp.where(inside, 1.0, diag)

    for s in range(n_seq):
        for d, s0_ref in enumerate((s0f_ref, s0b_ref) if has_init else (None, None)):
            st_scr[s, d] = jnp.zeros((WIDTH_C, QK_C_PAD), F32) if s0_ref is None else s0_ref[s]
    o_ref[...] = jnp.zeros_like(o_ref)

    def chunk_steps(chains):
        loaded = []
        for s, d, c in chains:
            g_ref = gb_ref if d else gf_ref
            row0 = s * n_tok + c * CHUNK
            rows = pl.ds(row0 if isinstance(row0, int) else pl.multiple_of(row0, CHUNK), CHUNK)
            bcum = _dot_exact_lhs(tri_b[d], g_ref[rows, :])
            loaded.append((rows, q_ref[rows, :], k_ref[rows, :], v_ref[rows, :], bcum))
        scaled = []
        for (s, d, c), (rows, q, k, v, bcum) in zip(chains, loaded):
            blast = bcum[0:1, :] if d else bcum[CHUNK - 1:CHUNK, :]
            qe = (q * jnp.exp(bcum)).astype(BF16)
            ke = k * jnp.exp(-bcum)
            kd = (k * jnp.exp(blast - bcum)).astype(BF16)
            ke_heads = jnp.concatenate([ke * head_k[h] for h in range(H_C)], axis=0).astype(BF16)
            v_heads = jnp.concatenate([v * head_v[h] for h in range(H_C)], axis=0).astype(BF16)
            scaled.append((qe, ke_heads, kd, v.astype(BF16), v_heads, jnp.exp(blast)))
        products = []
        for (s, d, c), (qe, ke_heads, kd, vb, v_heads, decay) in zip(chains, scaled):
            st = st_scr[s, d]
            attn = _dot_nt(qe, ke_heads)
            inter = _dot_nt(qe, st.astype(BF16))
            upd = _dot_tn(vb, kd)
            products.append((st, attn, inter, upd))
        masked = []
        for (s, d, c), (qe, ke_heads, kd, vb, v_heads, decay), (st, attn, inter, upd) in zip(
                chains, scaled, products):
            st_scr[s, d] = st * decay + upd * diag
            masked.append((attn * tri4[d]).astype(BF16))
        for (rows, *_), (_, _, _, _, v_heads, _), (_, _, inter, _), attn_b in zip(
                loaded, scaled, products, masked):
            o_ref[rows, :] += _dot(attn_b, v_heads) + inter

    def group(i):
        for j in range(GLA_UNROLL):
            fwd = i * GLA_UNROLL + j
            chunk_steps([(s, d, n_chunks - 1 - fwd if d else fwd) for s in range(n_seq) for d in (0, 1)])

    if n_groups == 1:
        group(0)
    else:
        def body(i, carry):
            group(i)
            return carry
        lax.fori_loop(0, n_groups, body, 0)

    if emit_state:
        for s in range(n_seq):
            sf_ref[s] = st_scr[s, 0]
            sb_ref[s] = st_scr[s, 1]


def _gla(p, n_batch, n_tok, init, emit_state):
    n_seq = GLA_SEQS
    tok = lambda w: pl.BlockSpec((n_seq * n_tok, w), lambda b: (b, 0))
    st_spec = pl.BlockSpec((n_seq, WIDTH_C, QK_C_PAD), lambda b: (b, 0, 0))
    in_specs = [tok(QK_C_PAD), tok(QK_C_PAD), tok(WIDTH_C), tok(QK_C_PAD), tok(QK_C_PAD)]
    args = [p["qc"], p["kc"], p["vc"], p["gf"], p["gb"]]
    if init is not None:
        in_specs += [st_spec] * 2
        args += list(init)
    out_specs = [tok(WIDTH_C)]
    out_shape = [jax.ShapeDtypeStruct((n_batch * n_tok, WIDTH_C), F32)]
    if emit_state:
        out_specs += [st_spec] * 2
        out_shape += [jax.ShapeDtypeStruct((n_batch, WIDTH_C, QK_C_PAD), F32)] * 2
    return pl.pallas_call(
        functools.partial(_gla_kernel, n_tok=n_tok, n_seq=n_seq, has_init=init is not None,
                          emit_state=emit_state),
        grid=(n_batch // n_seq,),
        in_specs=in_specs,
        out_specs=out_specs,
        out_shape=out_shape,
        scratch_shapes=[pltpu.VMEM((n_seq, 2, WIDTH_C, QK_C_PAD), F32)],
        compiler_params=_params(1),
    )(*args)


def _group_rms(x, same_group_bf16, group, gain):
    sq = x * x
    hi = sq.astype(BF16)
    mid = (sq - hi.astype(F32)).astype(BF16)
    ms = (_dot(hi, same_group_bf16) + _dot(mid, same_group_bf16)) * (1.0 / group)
    return x * lax.rsqrt(ms + EPS) * gain


def _merge_ffn_kernel(x_ref, mod_ref, oa_ref, ob_ref, oc_ref, rc_ref, na_ref, nb_ref, nc_ref,
                      gb_ref, gc_ref, wo_ref, gain_ref, w13_ref, w2_ref, fin_ref, o_ref, acc_ref,
                      *, lam_init, final):
    a = _rms(oa_ref[...], na_ref[...])
    b = _group_rms(ob_ref[...], gb_ref[...], DV_B, nb_ref[...]) * (1.0 - lam_init)
    c = _group_rms(oc_ref[...], gc_ref[...], DV_C, nc_ref[...]) * _silu(rc_ref[...])
    mixed = (_dot(a.astype(BF16), wo_ref[0:WIDTH_A, :])
             + _dot(b.astype(BF16), wo_ref[WIDTH_A:WIDTH_A + WIDTH_B, :])
             + _dot(c.astype(BF16), wo_ref[WIDTH_A + WIDTH_B:, :]))
    x = x_ref[...] + mod_ref[0, 5:6, :] * mixed
    y = _ffn_block(x, mod_ref, gain_ref[...], w13_ref, w2_ref, acc_ref, 6)
    if final:
        y = _rms(y, fin_ref[...])
    o_ref[...] = y


def _merge_ffn(x, mod, tokens_per_group, oa, ob, oc, rc, lw, consts, fin, layer, lam_init, final):
    n = x.shape[0]
    row = lambda w: pl.BlockSpec((TOKEN_TILE, w), lambda i: (i, 0))
    return pl.pallas_call(
        functools.partial(_merge_ffn_kernel, lam_init=lam_init, final=final),
        grid=(n // TOKEN_TILE,),
        in_specs=[row(D_MODEL), _mod_spec(tokens_per_group), row(WIDTH_A), row(WIDTH_B), row(WIDTH_C),
                  row(WIDTH_C), _layer_spec((1, WIDTH_A), layer), _layer_spec((1, WIDTH_B), layer),
                  _layer_spec((1, WIDTH_C), layer), _const_spec((WIDTH_B, WIDTH_B)),
                  _const_spec((WIDTH_C, WIDTH_C)), _layer_spec((D_MODEL, D_MODEL), layer, 1),
                  _layer_spec((1, D_MODEL), layer), _layer_spec((D_MODEL, 2 * D_FF), layer, 1),
                  _layer_spec((D_FF, D_MODEL), layer, 1), _const_spec((1, D_MODEL))],
        out_specs=row(D_MODEL),
        out_shape=jax.ShapeDtypeStruct((n, D_MODEL), F32),
        scratch_shapes=[pltpu.VMEM((TOKEN_TILE, D_MODEL), F32)],
        compiler_params=_params(1),
    )(x, mod, oa, ob, oc, rc, lw["out_norm_a"], lw["out_norm_b"], lw["out_norm_c"],
      consts["group_b"], consts["group_c"], lw["wo"], lw["norm_ffn2"], lw["ffn2_w13"], lw["ffn2_w2"], fin)


def _prepare_weights(w):
    n_layers = w["w_in"].shape[0]
    row = lambda v: v.reshape(n_layers, 1, -1)

    def pad_heads(m, width):
        m = m.reshape(n_layers, m.shape[1], H_A, width)
        m = jnp.pad(m, ((0, 0), (0, 0), (0, 0), (0, HEAD_PAD - width)))
        return m.reshape(n_layers, m.shape[1], H_A * HEAD_PAD)

    def gate_block(wg, first):
        lo = 0 if first else QK_C_PAD
        return jnp.pad(wg, ((0, 0), (0, 0), (lo, 2 * QK_C_PAD - lo - H_C * DK_C)))

    wg = jnp.concatenate([gate_block(w["gla_wg_f"], True), gate_block(w["gla_wg_b"], False),
                          jnp.zeros((n_layers, LANE - 2 * GATE_RANK, 2 * QK_C_PAD), F32)], axis=1)
    bg = gate_block(w["gla_bg_f"][:, None, :], True) + gate_block(w["gla_bg_b"][:, None, :], False)
    return {
        "norm_ffn1": row(w["norm_ffn1"]), "ffn1_w13": w["ffn1_w13"].astype(BF16),
        "ffn1_w2": w["ffn1_w2"].astype(BF16),
        "norm_ffn2": row(w["norm_ffn2"]), "ffn2_w13": w["ffn2_w13"].astype(BF16),
        "ffn2_w2": w["ffn2_w2"].astype(BF16),
        "norm_mix": row(w["norm_mix"]), "w_in": w["w_in"],
        "q_norm": row(w["mla_q_norm"]),
        "wuq": pad_heads(w["mla_w_uq"], DN_A + DR_A).astype(BF16),
        "kv_norm": row(w["mla_kv_norm"]),
        "wuk": pad_heads(w["mla_w_uk"], DN_A).astype(BF16),
        "wuvt": jnp.swapaxes(w["mla_w_uv"], 1, 2).astype(BF16),
        "wg": wg.astype(BF16), "bg": bg,
        "lam": jnp.stack([w["diff_lq1"], w["diff_lk1"], w["diff_lq2"], w["diff_lk2"]], axis=1),
        "out_norm_a": row(w["mla_out_norm"]),
        "out_norm_b": row(jnp.tile(w["diff_norm"], (1, H_B))),
        "out_norm_c": row(jnp.tile(w["gla_norm"], (1, H_C))),
        "wo": w["w_out"].astype(BF16),
    }


def _rope_tables(n_pos):
    pos = jnp.arange(n_pos)
    rows = (pos // GRID_W).astype(F32)
    cols = (pos % GRID_W).astype(F32)
    half = DH_B // 2
    inv = ROPE_BASE ** (-jnp.arange(0, half, 2, dtype=F32) / half)
    ang = jnp.concatenate([rows[:, None] * inv, rows[:, None] * inv,
                           cols[:, None] * inv, cols[:, None] * inv], axis=1)
    sign = jnp.tile(jnp.concatenate([-jnp.ones(8, F32), jnp.ones(8, F32)]), 2)
    cos32, sin32 = jnp.cos(ang), jnp.sin(ang) * sign
    cos_full, sin_full = jnp.tile(cos32, (1, LANE // 32)), jnp.tile(sin32, (1, LANE // 32))
    ones = lambda n: jnp.ones((n_pos, n), F32)
    zeros = lambda n: jnp.zeros((n_pos, n), F32)
    cos_head = jnp.concatenate([ones(ROPE_OFF), cos32, ones(LANE - ROPE_OFF - DR_A)], axis=1)
    sin_head = jnp.concatenate([zeros(ROPE_OFF), sin32, zeros(LANE - ROPE_OFF - DR_A)], axis=1)
    return cos_head, sin_head, cos_full, sin_full


def _same_group_matrix(width, group):
    idx = jnp.arange(width) // group
    return jnp.where(idx[:, None] == idx[None, :], 1.0, 0.0).astype(BF16)


def _state_to_blockdiag_t(s):
    st = jnp.swapaxes(s, 2, 3)
    blocks = [jnp.pad(st[:, h], ((0, 0), (0, 0), (h * DK_C, QK_C_PAD - (h + 1) * DK_C))) for h in range(H_C)]
    return jnp.concatenate(blocks, axis=1)


def _blockdiag_t_to_state(st):
    return jnp.stack([jnp.swapaxes(st[:, h * DV_C:(h + 1) * DV_C, h * DK_C:(h + 1) * DK_C], 1, 2)
                      for h in range(H_C)], axis=1)


def kernel(x_prompt, x_sample, cache_mla_ckv, cache_mla_krope, cache_diff_k, cache_diff_v, state_gla_fwd, state_gla_bwd, c, c_ctx, w_mod, b_mod, norm_ffn1, ffn1_w13, ffn1_w2, norm_mix, w_in, mla_q_norm, mla_w_uq, mla_kv_norm, mla_w_uk, mla_w_uv, mla_out_norm, diff_lq1, diff_lk1, diff_lq2, diff_lk2, diff_norm, gla_wg_f, gla_bg_f, gla_wg_b, gla_bg_b, gla_norm, w_out, norm_ffn2, ffn2_w13, ffn2_w2, final_norm):
    w = dict(norm_ffn1=norm_ffn1, ffn1_w13=ffn1_w13, ffn1_w2=ffn1_w2, norm_mix=norm_mix, w_in=w_in,
             mla_q_norm=mla_q_norm, mla_w_uq=mla_w_uq, mla_kv_norm=mla_kv_norm, mla_w_uk=mla_w_uk,
             mla_w_uv=mla_w_uv, mla_out_norm=mla_out_norm, diff_lq1=diff_lq1, diff_lk1=diff_lk1,
             diff_lq2=diff_lq2, diff_lk2=diff_lk2, diff_norm=diff_norm, gla_wg_f=gla_wg_f,
             gla_bg_f=gla_bg_f, gla_wg_b=gla_wg_b, gla_bg_b=gla_bg_b, gla_norm=gla_norm, w_out=w_out,
             norm_ffn2=norm_ffn2, ffn2_w13=ffn2_w13, ffn2_w2=ffn2_w2)
    n_ctx_b, n_ctx_t, _ = x_prompt.shape
    n_lat_b, n_lat_t, _ = x_sample.shape
    n_past = cache_mla_ckv.shape[2]

    cvec = jnp.concatenate([c_ctx[None, :], c, jnp.zeros((16 - 1 - n_lat_b, D_MODEL), F32)], axis=0)
    mod = _modulation(cvec, w_mod, b_mod).reshape(DEPTH, 16, N_MOD, D_MODEL)

    rope_tabs = _rope_tables(n_lat_t)
    consts = {"group_b": _same_group_matrix(WIDTH_B, DV_B), "group_c": _same_group_matrix(WIDTH_C, DV_C)}
    cache_kr = jnp.pad(cache_mla_krope, ((0, 0), (0, 0), (0, 0), (ROPE_OFF, LANE - ROPE_OFF - DR_A)))
    cache_dk = cache_diff_k.reshape(n_lat_b, DEPTH, n_past, WIDTH_B)
    cache_dv = cache_diff_v.reshape(n_lat_b, DEPTH, n_past, WIDTH_B)
    fin = final_norm.reshape(1, D_MODEL)

    xp = x_prompt.reshape(n_ctx_b * n_ctx_t, D_MODEL)
    xs = x_sample.reshape(n_lat_b * n_lat_t, D_MODEL)
    new = {k: [] for k in ("ckv", "kr", "dk", "dv", "sf", "sb")}
    lw = _prepare_weights(w)
    for l in range(DEPTH):
        lam_init = 0.8 - 0.6 * math.exp(-0.3 * l)
        last = l == DEPTH - 1
        mod_ctx, mod_lat = mod[l, 0:1], mod[l, 1:1 + n_lat_b]

        def trunk(x, m, tpg, n_batch, n_tok, rope, mla_cache, diff_cache, gla_init, emit_state):
            x = _ffn(x, m, tpg, lw["norm_ffn1"], lw["ffn1_w13"], lw["ffn1_w2"], l, mod_row=0)
            p = _project(x, m, tpg, lw, lw["w_in"], l, rope)
            oa = _mla(p, lw, n_batch, n_tok, mla_cache, l)
            ob = _diff(p, lw, n_batch, n_tok, diff_cache, l, lam_init)
            gla_out = _gla(p, n_batch, n_tok, gla_init, emit_state)
            x = _merge_ffn(x, m, tpg, oa, ob, gla_out[0], p["rc"], lw, consts, fin, l, lam_init, last)
            return x, p, gla_out

        xp, p, gla_out = trunk(xp, mod_ctx, n_ctx_b * n_ctx_t, n_ctx_b, n_ctx_t, None, None, None, None, True)
        new["ckv"].append(p["ckv"].reshape(n_ctx_b, n_ctx_t, KV_RANK))
        new["kr"].append(p["kr"][:, ROPE_OFF:ROPE_OFF + DR_A].reshape(n_ctx_b, n_ctx_t, DR_A))
        new["dk"].append(p["kb"].reshape(n_ctx_b, n_ctx_t, H_B, 2 * DH_B))
        new["dv"].append(p["vb"].reshape(n_ctx_b, n_ctx_t, H_B, DV_B))
        new["sf"].append(_blockdiag_t_to_state(gla_out[1]))
        new["sb"].append(_blockdiag_t_to_state(gla_out[2]))

        gla_init = (_state_to_blockdiag_t(state_gla_fwd[:, l]), _state_to_blockdiag_t(state_gla_bwd[:, l]))
        xs, _, _ = trunk(xs, mod_lat, n_lat_t, n_lat_b, n_lat_t, rope_tabs, (cache_mla_ckv, cache_kr),
                         (cache_dk, cache_dv), gla_init, False)

    stack = lambda name: jnp.stack(new[name], axis=1)
    return (xp.reshape(x_prompt.shape), xs.reshape(x_sample.shape), stack("ckv"), stack("kr"),
            stack("dk"), stack("dv"), stack("sf"), stack("sb"))
```

```python
import functools
import math

import jax
import jax.numpy as jnp
from jax import lax
from jax.experimental import pallas as pl
from jax.experimental.pallas import tpu as pltpu

F32 = jnp.float32
BF16 = jnp.bfloat16

D_MODEL = 1024
DEPTH = 2
GRID_W = 64
ROPE_BASE = 10000.0
EPS = 1e-6
H_A, DN_A, DR_A, DV_A = 6, 64, 32, 64
Q_RANK, KV_RANK = 384, 256
H_B, DH_B, DV_B = 4, 32, 64
H_C, DK_C, DV_C = 4, 48, 96
GATE_RANK = 16
GATE_NORM = 16.0
CHUNK = 64
WIDTH_A = H_A * DV_A
WIDTH_B = H_B * DV_B
WIDTH_C = H_C * DV_C
D_FF = 2816
N_MOD = 9

LANE = 128
FF_CHUNK = 256
TOKEN_TILE = 512
Q_TILE = 1024
KEY_BLOCK = 256
ONES_ROWS = 16
MOD_TILE = 1152
GLA_UNROLL = 4
ATTN_SEQS = 4
GLA_SEQS = 2
VMEM_LIMIT = 56 * 1024 * 1024

HEAD_PAD = LANE
ROPE_OFF = DN_A
QK_C_PAD = 256
SEG = {}
_off = 0
for _name, _w in (("cq", Q_RANK), ("ckv", KV_RANK), ("krope", LANE), ("qb", 256), ("kb", 256),
                  ("vb", 256), ("qc", QK_C_PAD), ("kc", QK_C_PAD), ("vc", WIDTH_C), ("rc", WIDTH_C),
                  ("gl", LANE)):
    SEG[_name] = (_off, _off + _w)
    _off += _w
PROJ_PAD = _off

PROJ_SPLITS = (Q_RANK, KV_RANK, DR_A, 2 * H_B * DH_B, 2 * H_B * DH_B, H_B * DV_B, H_C * DK_C, H_C * DK_C,
               WIDTH_C, WIDTH_C, GATE_RANK, GATE_RANK)
PROJ_COLS = sum(PROJ_SPLITS)
_SRC = [sum(PROJ_SPLITS[:i]) for i in range(len(PROJ_SPLITS) + 1)]


def _src(i, j=None):
    return ("src", _SRC[i], _SRC[(i if j is None else j) + 1])


PROJ_LAYOUT = (
    ("cq", (_src(0),)), ("ckv", (_src(1),)),
    ("krope", (("zero", 0, ROPE_OFF), _src(2), ("zero", 0, LANE - ROPE_OFF - DR_A))),
    ("qb", (_src(3),)), ("kb", (_src(4),)), ("vb", (_src(5),)),
    ("qc", (_src(6), ("zero", 0, QK_C_PAD - H_C * DK_C))),
    ("kc", (_src(7), ("zero", 0, QK_C_PAD - H_C * DK_C))),
    ("vc", (_src(8),)), ("rc", (_src(9),)),
    ("gl", (_src(10, 11), ("zero", 0, LANE - 2 * GATE_RANK))),
)


def _dot(a, b):
    return jnp.dot(a, b, preferred_element_type=F32)


def _dot_nt(a, b):
    return lax.dot_general(a, b, (((1,), (1,)), ((), ())), preferred_element_type=F32)


def _dot_tn(a, b):
    return lax.dot_general(a, b, (((0,), (0,)), ((), ())), preferred_element_type=F32)


def _rms(x, gain):
    ms = jnp.mean(x * x, axis=-1, keepdims=True)
    return (x * lax.rsqrt(ms + EPS)) * gain


def _silu(x):
    return x * jax.nn.sigmoid(x)


def _const_spec(shape):
    nd = len(shape)
    return pl.BlockSpec(shape, lambda *_: (0,) * nd)


def _layer_spec(shape, layer, buffers=None):
    nd = len(shape)
    mode = {} if buffers is None else {"pipeline_mode": pl.Buffered(buffers)}
    return pl.BlockSpec((None,) + tuple(shape), lambda *_: (layer,) + (0,) * nd, **mode)


def _params(n_axes):
    return pltpu.CompilerParams(dimension_semantics=("arbitrary",) * n_axes,
                                vmem_limit_bytes=VMEM_LIMIT)


def _mod_kernel(c_ref, w_ref, b_ref, o_ref):
    s = _silu(c_ref[...]).astype(BF16)
    o_ref[0] = _dot(s, w_ref[0].astype(BF16)) + b_ref[0]


def _modulation(cvec, w_mod, b_mod):
    n_rows = cvec.shape[0]
    width = N_MOD * D_MODEL
    return pl.pallas_call(
        _mod_kernel,
        grid=(DEPTH, width // MOD_TILE),
        in_specs=[
            pl.BlockSpec((n_rows, D_MODEL), lambda l, j: (0, 0)),
            pl.BlockSpec((1, D_MODEL, MOD_TILE), lambda l, j: (l, 0, j)),
            pl.BlockSpec((1, 1, MOD_TILE), lambda l, j: (l, 0, j)),
        ],
        out_specs=pl.BlockSpec((1, n_rows, MOD_TILE), lambda l, j: (l, 0, j)),
        out_shape=jax.ShapeDtypeStruct((DEPTH, n_rows, width), F32),
        compiler_params=_params(2),
    )(cvec, w_mod, b_mod.reshape(DEPTH, 1, width))


def _mod_spec(tokens_per_group):
    return pl.BlockSpec((1, N_MOD, D_MODEL), lambda i: ((i * TOKEN_TILE) // tokens_per_group, 0, 0))


def _ada_norm(x, gain, mod_ref, first_row):
    shift = mod_ref[0, first_row:first_row + 1, :]
    scale = mod_ref[0, first_row + 1:first_row + 2, :]
    return _rms(x, gain) * (1.0 + scale) + shift


def _ffn_block(x, mod_ref, gain, w13_ref, w2_ref, acc_ref, mod_row):
    u = _ada_norm(x, gain, mod_ref, mod_row).astype(BF16)
    for c in range(D_FF // FF_CHUNK):
        lo, hi = c * FF_CHUNK, (c + 1) * FF_CHUNK
        a = _dot(u, w13_ref[:, lo:hi])
        b = _dot(u, w13_ref[:, D_FF + lo:D_FF + hi])
        t = _dot((_silu(a) * b).astype(BF16), w2_ref[lo:hi, :])
        if c == 0:
            acc_ref[...] = t
        else:
            acc_ref[...] += t
    gate = mod_ref[0, mod_row + 2:mod_row + 3, :]
    return x + (0.5 * gate) * acc_ref[...]


def _ffn_kernel(x_ref, mod_ref, gain_ref, w13_ref, w2_ref, o_ref, acc_ref, *, mod_row):
    o_ref[...] = _ffn_block(x_ref[...], mod_ref, gain_ref[...], w13_ref, w2_ref, acc_ref, mod_row)


def _ffn(x, mod, tokens_per_group, gain, w13, w2, layer, *, mod_row):
    n = x.shape[0]
    tile = pl.BlockSpec((TOKEN_TILE, D_MODEL), lambda i: (i, 0))
    return pl.pallas_call(
        functools.partial(_ffn_kernel, mod_row=mod_row),
        grid=(n // TOKEN_TILE,),
        in_specs=[tile, _mod_spec(tokens_per_group), _layer_spec((1, D_MODEL), layer),
                  _layer_spec((D_MODEL, 2 * D_FF), layer, 1), _layer_spec((D_FF, D_MODEL), layer, 1)],
        out_specs=tile,
        out_shape=jax.ShapeDtypeStruct((n, D_MODEL), F32),
        scratch_shapes=[pltpu.VMEM((TOKEN_TILE, D_MODEL), F32)],
        compiler_params=_params(1),
    )(x, mod, gain, w13, w2)


def _rope(x, cos, sin_signed):
    first = (lax.broadcasted_iota(jnp.int32, (x.shape[0], LANE), 1) % 16) < 8
    outs = []
    for j in range(x.shape[1] // LANE):
        xj = x[:, j * LANE:(j + 1) * LANE]
        partner = jnp.where(first, pltpu.roll(xj, LANE - 8, 1), pltpu.roll(xj, 8, 1))
        outs.append(xj * cos + partner * sin_signed)
    return outs[0] if len(outs) == 1 else jnp.concatenate(outs, axis=1)


def _log_sigmoid(z):
    return jnp.minimum(z, 0.0) - jnp.log1p(jnp.exp(-jnp.abs(z)))


def _proj_kernel(*refs, rope):
    (x_ref, mod_ref, gain_ref, win_ref, qn_ref, wuq_ref, kvn_ref, wg_ref, bg_ref) = refs[:9]
    refs = refs[9:]
    if rope:
        ch_ref, sh_ref, cf_ref, sf_ref = refs[:4]
        refs = refs[4:]
    (q_o, ckv_o, kr_o, qb_o, kb_o, vb_o, qc_o, kc_o, vc_o, rc_o, gf_o, gb_o, wp_ref) = refs

    @pl.when(pl.program_id(0) == 0)
    def _():
        rows = 256
        for r0 in range(0, D_MODEL, rows):
            for name, pieces in PROJ_LAYOUT:
                parts = []
                for kind, a, b in pieces:
                    parts.append(jnp.zeros((rows, b - a), F32) if kind == "zero" else win_ref[r0:r0 + rows, a:b])
                blk = parts[0] if len(parts) == 1 else jnp.concatenate(parts, axis=1)
                lo, hi = SEG[name]
                wp_ref[r0:r0 + rows, lo:hi] = blk.astype(BF16)

    u = _ada_norm(x_ref[...], gain_ref[...], mod_ref, 3).astype(BF16)

    def seg(name):
        lo, hi = SEG[name]
        return _dot(u, wp_ref[:, lo:hi])

    q = _dot(_rms(seg("cq"), qn_ref[...]).astype(BF16), wuq_ref[...])
    kr = seg("krope")
    qb = seg("qb")
    kb = seg("kb")
    if rope:
        q = _rope(q, ch_ref[...], sh_ref[...])
        kr = _rope(kr, ch_ref[...], sh_ref[...])
        qb = _rope(qb, cf_ref[...], sf_ref[...])
        kb = _rope(kb, cf_ref[...], sf_ref[...])
    q_o[...] = (q * (DN_A + DR_A) ** -0.5).astype(BF16)
    ckv_o[...] = _rms(seg("ckv"), kvn_ref[...])
    kr_o[...] = kr
    qb_o[...] = (qb * DH_B ** -0.5).astype(BF16)
    kb_o[...] = kb
    vb_o[...] = seg("vb")
    qc_o[...] = seg("qc") * (DK_C ** -0.5)
    kc_o[...] = seg("kc")
    vc_o[...] = seg("vc")
    rc_o[...] = seg("rc")
    z = _dot(seg("gl").astype(BF16), wg_ref[...]) + bg_ref[...]
    g = _log_sigmoid(z) / GATE_NORM
    gf_o[...] = g[:, :QK_C_PAD]
    gb_o[...] = g[:, QK_C_PAD:]


PROJ_OUT = (("q", H_A * HEAD_PAD, BF16), ("ckv", KV_RANK, F32), ("kr", LANE, F32), ("qb", 256, BF16),
            ("kb", 256, F32), ("vb", 256, F32), ("qc", QK_C_PAD, F32), ("kc", QK_C_PAD, F32),
            ("vc", WIDTH_C, F32), ("rc", WIDTH_C, F32), ("gf", QK_C_PAD, F32), ("gb", QK_C_PAD, F32))


def _project(x, mod, tokens_per_group, lw, w_in, layer, rope_tabs):
    n = x.shape[0]
    rope = rope_tabs is not None
    row = lambda w: pl.BlockSpec((TOKEN_TILE, w), lambda i: (i, 0))
    in_specs = [row(D_MODEL), _mod_spec(tokens_per_group), _layer_spec((1, D_MODEL), layer),
                _layer_spec((D_MODEL, PROJ_COLS), layer, 1), _layer_spec((1, Q_RANK), layer),
                _layer_spec((Q_RANK, H_A * HEAD_PAD), layer), _layer_spec((1, KV_RANK), layer),
                _layer_spec((LANE, 2 * QK_C_PAD), layer), _layer_spec((1, 2 * QK_C_PAD), layer)]
    args = [x, mod, lw["norm_mix"], w_in, lw["q_norm"], lw["wuq"], lw["kv_norm"], lw["wg"], lw["bg"]]
    if rope:
        n_pos = rope_tabs[0].shape[0]
        tab = pl.BlockSpec((TOKEN_TILE, LANE), lambda i: (i % (n_pos // TOKEN_TILE), 0))
        in_specs += [tab] * 4
        args += list(rope_tabs)
    outs = pl.pallas_call(
        functools.partial(_proj_kernel, rope=rope),
        grid=(n // TOKEN_TILE,),
        in_specs=in_specs,
        out_specs=[row(w) for _, w, _ in PROJ_OUT],
        out_shape=[jax.ShapeDtypeStruct((n, w), dt) for _, w, dt in PROJ_OUT],
        scratch_shapes=[pltpu.VMEM((D_MODEL, PROJ_PAD), BF16)],
        compiler_params=_params(1),
    )(*args)
    return {name: o for (name, _, _), o in zip(PROJ_OUT, outs)}


def _lane_mask(width, lo, hi):
    lane = lax.broadcasted_iota(jnp.int32, (1, width), 1)
    return jnp.where((lane >= lo) & (lane < hi), 1.0, 0.0).astype(F32)


def _attend_t(units, dv):
    outs = []
    q, keys, vt_ext = units[0]()
    s = _dot_nt(q, keys)
    for u in range(len(units)):
        s_now, vt_now = s, vt_ext
        if u + 1 < len(units):
            q, keys, vt_ext = units[u + 1]()
            s = _dot_nt(q, keys)
        e = jnp.exp(s_now - jnp.max(s_now, axis=-1, keepdims=True)).astype(BF16)
        r = _dot_nt(vt_now, e)
        outs.append(r[0:dv] * (1.0 / r[dv:dv + 1]))
    return outs


def _attn_seqs(n_cache, n_q_tiles):
    return ATTN_SEQS if (n_cache == 0 and n_q_tiles == 1) else 1


def _mla_kernel(*refs, n_cache, n_seq, n_tok):
    if n_cache:
        q_ref, ckv_ref, kr_ref, cckv_ref, ckr_ref, wuk_ref, wuvt_ref, o_ref, kf_scr, vt_scr = refs
    else:
        q_ref, ckv_ref, kr_ref, wuk_ref, wuvt_ref, o_ref, kf_scr, vt_scr = refs

    @pl.when(pl.program_id(1) == 0)
    def _():
        def fill(s, row0, ckv, kr):
            rows = ckv.shape[0]
            cb = ckv.astype(BF16)
            kr_all = jnp.concatenate([kr] * H_A, axis=1)
            kf_scr[s, row0:row0 + rows, :] = (_dot(cb, wuk_ref[...]) + kr_all).astype(BF16)
            vt = _dot_nt(wuvt_ref[...], cb)
            for h in range(H_A):
                vt_scr[s * H_A + h, 0:DV_A, row0:row0 + rows] = vt[h * DV_A:(h + 1) * DV_A].astype(BF16)

        for s in range(n_seq):
            if n_cache:
                fill(s, 0, cckv_ref[...], ckr_ref[...])
            tok = slice(s * n_tok, (s + 1) * n_tok)
            fill(s, n_cache, ckv_ref[tok, :], kr_ref[tok, :])
        for u in range(n_seq * H_A):
            vt_scr[u, DV_A:, :] = jnp.ones((ONES_ROWS, vt_scr.shape[2]), BF16)

    n_q = q_ref.shape[0] // n_seq

    def unit(s, h):
        sl = slice(h * HEAD_PAD, (h + 1) * HEAD_PAD)
        return lambda: (q_ref[s * n_q:(s + 1) * n_q, sl], kf_scr[s, :, sl], vt_scr[s * H_A + h])

    heads = _attend_t([unit(s, h) for h in range(H_A) for s in range(n_seq)], DV_A)
    for s in range(n_seq):
        o_ref[s * n_q:(s + 1) * n_q, :] = jnp.concatenate(heads[s::n_seq], axis=0).T


def _mla(p, lw, n_batch, n_tok, cache, layer):
    n_cache = 0 if cache is None else cache[0].shape[2]
    q_tile = min(Q_TILE, n_tok)
    nqt = n_tok // q_tile
    n_seq = _attn_seqs(n_cache, nqt)
    in_specs = [pl.BlockSpec((n_seq * q_tile, H_A * HEAD_PAD), lambda b, i: (b * nqt + i, 0)),
                pl.BlockSpec((n_seq * n_tok, KV_RANK), lambda b, i: (b, 0)),
                pl.BlockSpec((n_seq * n_tok, LANE), lambda b, i: (b, 0))]
    args = [p["q"], p["ckv"], p["kr"]]
    if n_cache:
        in_specs += [pl.BlockSpec((None, None, n_cache, KV_RANK), lambda b, i: (b, layer, 0, 0)),
                     pl.BlockSpec((None, None, n_cache, LANE), lambda b, i: (b, layer, 0, 0))]
        args += list(cache)
    in_specs += [_layer_spec((KV_RANK, H_A * HEAD_PAD), layer), _layer_spec((WIDTH_A, KV_RANK), layer)]
    args += [lw["wuk"], lw["wuvt"]]
    n_keys = n_cache + n_tok
    return pl.pallas_call(
        functools.partial(_mla_kernel, n_cache=n_cache, n_seq=n_seq, n_tok=n_tok),
        grid=(n_batch // n_seq, nqt),
        in_specs=in_specs,
        out_specs=pl.BlockSpec((n_seq * q_tile, WIDTH_A), lambda b, i: (b * nqt + i, 0)),
        out_shape=jax.ShapeDtypeStruct((n_batch * n_tok, WIDTH_A), F32),
        scratch_shapes=[pltpu.VMEM((n_seq, n_keys, H_A * HEAD_PAD), BF16),
                        pltpu.VMEM((n_seq * H_A, DV_A + ONES_ROWS, n_keys), BF16)],
        compiler_params=_params(2),
    )(*args)


def _diff_kernel(*refs, n_cache, n_seq, n_tok, lam_init):
    if n_cache:
        q_ref, k_ref, v_ref, ck_ref, cv_ref, lam_ref, o_ref, k_scr, vt_scr = refs
    else:
        q_ref, k_ref, v_ref, lam_ref, o_ref, k_scr, vt_scr = refs

    @pl.when(pl.program_id(1) == 0)
    def _():
        def fill(s, row0, k, v):
            rows = k.shape[0]
            k_scr[s, row0:row0 + rows, :] = k.astype(BF16)
            vt = v.T
            for h in range(H_B):
                vt_scr[s * H_B + h, 0:DV_B, row0:row0 + rows] = vt[h * DV_B:(h + 1) * DV_B].astype(BF16)

        for s in range(n_seq):
            if n_cache:
                fill(s, 0, ck_ref[...], cv_ref[...])
            tok = slice(s * n_tok, (s + 1) * n_tok)
            fill(s, n_cache, k_ref[tok, :], v_ref[tok, :])
        for u in range(n_seq * H_B):
            vt_scr[u, DV_B:, :] = jnp.ones((ONES_ROWS, vt_scr.shape[2]), BF16)

    lv = lam_ref[...]
    lam = (jnp.exp(jnp.sum(lv[0:1] * lv[1:2], axis=-1, keepdims=True))
           - jnp.exp(jnp.sum(lv[2:3] * lv[3:4], axis=-1, keepdims=True)) + lam_init)
    n_q = q_ref.shape[0] // n_seq
    lane = lax.broadcasted_iota(jnp.int32, (n_q, LANE), 1)

    def unit(s, h, j):
        blk = slice((h // 2) * LANE, (h // 2 + 1) * LANE)
        lo = (h % 2) * 2 * DH_B + j * DH_B

        def fn():
            qh = q_ref[s * n_q:(s + 1) * n_q, blk]
            qm = jnp.where((lane >= lo) & (lane < lo + DH_B), qh, jnp.zeros_like(qh))
            return qm, k_scr[s, :, blk], vt_scr[s * H_B + h]
        return fn

    maps = _attend_t([unit(s, h, j) for h in range(H_B) for j in range(2) for s in range(n_seq)], DV_B)
    for s in range(n_seq):
        mine = maps[s::n_seq]
        heads = [mine[2 * h] - lam * mine[2 * h + 1] for h in range(H_B)]
        o_ref[s * n_q:(s + 1) * n_q, :] = jnp.concatenate(heads, axis=0).T


def _diff(p, lw, n_batch, n_tok, cache, layer, lam_init):
    n_cache = 0 if cache is None else cache[0].shape[2]
    q_tile = min(Q_TILE, n_tok)
    nqt = n_tok // q_tile
    n_seq = _attn_seqs(n_cache, nqt)
    in_specs = [pl.BlockSpec((n_seq * q_tile, WIDTH_B), lambda b, i: (b * nqt + i, 0)),
                pl.BlockSpec((n_seq * n_tok, WIDTH_B), lambda b, i: (b, 0)),
                pl.BlockSpec((n_seq * n_tok, WIDTH_B), lambda b, i: (b, 0))]
    args = [p["qb"], p["kb"], p["vb"]]
    if n_cache:
        in_specs += [pl.BlockSpec((None, None, n_cache, WIDTH_B), lambda b, i: (b, layer, 0, 0))] * 2
        args += list(cache)
    in_specs.append(_layer_spec((4, DH_B), layer))
    args.append(lw["lam"])
    n_keys = n_cache + n_tok
    return pl.pallas_call(
        functools.partial(_diff_kernel, n_cache=n_cache, n_seq=n_seq, n_tok=n_tok, lam_init=lam_init),
        grid=(n_batch // n_seq, nqt),
        in_specs=in_specs,
        out_specs=pl.BlockSpec((n_seq * q_tile, WIDTH_B), lambda b, i: (b * nqt + i, 0)),
        out_shape=jax.ShapeDtypeStruct((n_batch * n_tok, WIDTH_B), F32),
        scratch_shapes=[pltpu.VMEM((n_seq, n_keys, WIDTH_B), BF16),
                        pltpu.VMEM((n_seq * H_B, DV_B + ONES_ROWS, n_keys), BF16)],
        compiler_params=_params(2),
    )(*args)


def _split3(x):
    hi = x.astype(BF16)
    r1 = x - hi.astype(F32)
    mid = r1.astype(BF16)
    lo = (r1 - mid.astype(F32)).astype(BF16)
    return hi, mid, lo


def _dot_exact_lhs(a_bf16, x):
    hi, mid, lo = _split3(x)
    return _dot(a_bf16, hi) + _dot(a_bf16, mid) + _dot(a_bf16, lo)


def _gla_kernel(*refs, n_tok, n_seq, has_init, emit_state):
    q_ref, k_ref, v_ref, gf_ref, gb_ref = refs[:5]
    refs = refs[5:]
    if has_init:
        s0f_ref, s0b_ref = refs[:2]
        refs = refs[2:]
    o_ref = refs[0]
    refs = refs[1:]
    if emit_state:
        sf_ref, sb_ref = refs[:2]
        refs = refs[2:]
    st_scr, = refs

    n_chunks = n_tok // CHUNK
    n_groups = n_chunks // GLA_UNROLL
    t_idx = lax.broadcasted_iota(jnp.int32, (CHUNK, CHUNK), 0)
    s_idx = lax.broadcasted_iota(jnp.int32, (CHUNK, CHUNK), 1)
    tri = [jnp.where(s_idx <= t_idx, 1.0, 0.0).astype(F32), jnp.where(s_idx >= t_idx, 1.0, 0.0).astype(F32)]
    tri4 = [jnp.concatenate([t] * H_C, axis=1) for t in tri]
    tri_b = [t.astype(BF16) for t in tri]
    head_k = [_lane_mask(QK_C_PAD, h * DK_C, (h + 1) * DK_C) for h in range(H_C)]
    head_v = [_lane_mask(WIDTH_C, h * DV_C, (h + 1) * DV_C) for h in range(H_C)]
    row_v = lax.broadcasted_iota(jnp.int32, (WIDTH_C, QK_C_PAD), 0)
    col_k = lax.broadcasted_iota(jnp.int32, (WIDTH_C, QK_C_PAD), 1)
    diag = jnp.zeros((WIDTH_C, QK_C_PAD), F32)
    for h in range(H_C):
        inside = ((row_v >= h * DV_C) & (row_v < (h + 1) * DV_C)
                  & (col_k >= h * DK_C) & (col_k < (h + 1) * DK_C))
        diag = jnp.where(inside, 1.0, diag)

    for s in range(n_seq):
        for d, s0_ref in enumerate((s0f_ref, s0b_ref) if has_init else (None, None)):
            st_scr[s, d] = jnp.zeros((WIDTH_C, QK_C_PAD), F32) if s0_ref is None else s0_ref[s]
    o_ref[...] = jnp.zeros_like(o_ref)

    def chunk_steps(chains):
        loaded = []
        for s, d, c in chains:
            g_ref = gb_ref if d else gf_ref
            row0 = s * n_tok + c * CHUNK
            rows = pl.ds(row0 if isinstance(row0, int) else pl.multiple_of(row0, CHUNK), CHUNK)
            bcum = _dot_exact_lhs(tri_b[d], g_ref[rows, :])
            loaded.append((rows, q_ref[rows, :], k_ref[rows, :], v_ref[rows, :], bcum))
        scaled = []
        for (s, d, c), (rows, q, k, v, bcum) in zip(chains, loaded):
            blast = bcum[0:1, :] if d else bcum[CHUNK - 1:CHUNK, :]
            qe = (q * jnp.exp(bcum)).astype(BF16)
            ke = k * jnp.exp(-bcum)
            kd = (k * jnp.exp(blast - bcum)).astype(BF16)
            ke_heads = jnp.concatenate([ke * head_k[h] for h in range(H_C)], axis=0).astype(BF16)
            v_heads = jnp.concatenate([v * head_v[h] for h in range(H_C)], axis=0).astype(BF16)
            scaled.append((qe, ke_heads, kd, v.astype(BF16), v_heads, jnp.exp(blast)))
        products = []
        for (s, d, c), (qe, ke_heads, kd, vb, v_heads, decay) in zip(chains, scaled):
            st = st_scr[s, d]
            attn = _dot_nt(qe, ke_heads)
            inter = _dot_nt(qe, st.astype(BF16))
            upd = _dot_tn(vb, kd)
            products.append((st, attn, inter, upd))
        masked = []
        for (s, d, c), (qe, ke_heads, kd, vb, v_heads, decay), (st, attn, inter, upd) in zip(
                chains, scaled, products):
            st_scr[s, d] = st * decay + upd * diag
            masked.append((attn * tri4[d]).astype(BF16))
        for (rows, *_), (_, _, _, _, v_heads, _), (_, _, inter, _), attn_b in zip(
                loaded, scaled, products, masked):
            o_ref[rows, :] += _dot(attn_b, v_heads) + inter

    def group(i):
        for j in range(GLA_UNROLL):
            fwd = i * GLA_UNROLL + j
            chunk_steps([(s, d, n_chunks - 1 - fwd if d else fwd) for s in range(n_seq) for d in (0, 1)])

    if n_groups == 1:
        group(0)
    else:
        def body(i, carry):
            group(i)
            return carry
        lax.fori_loop(0, n_groups, body, 0)

    if emit_state:
        for s in range(n_seq):
            sf_ref[s] = st_scr[s, 0]
            sb_ref[s] = st_scr[s, 1]


def _gla(p, n_batch, n_tok, init, emit_state):
    n_seq = GLA_SEQS
    tok = lambda w: pl.BlockSpec((n_seq * n_tok, w), lambda b: (b, 0))
    st_spec = pl.BlockSpec((n_seq, WIDTH_C, QK_C_PAD), lambda b: (b, 0, 0))
    in_specs = [tok(QK_C_PAD), tok(QK_C_PAD), tok(WIDTH_C), tok(QK_C_PAD), tok(QK_C_PAD)]
    args = [p["qc"], p["kc"], p["vc"], p["gf"], p["gb"]]
    if init is not None:
        in_specs += [st_spec] * 2
        args += list(init)
    out_specs = [tok(WIDTH_C)]
    out_shape = [jax.ShapeDtypeStruct((n_batch * n_tok, WIDTH_C), F32)]
    if emit_state:
        out_specs += [st_spec] * 2
        out_shape += [jax.ShapeDtypeStruct((n_batch, WIDTH_C, QK_C_PAD), F32)] * 2
    return pl.pallas_call(
        functools.partial(_gla_kernel, n_tok=n_tok, n_seq=n_seq, has_init=init is not None,
                          emit_state=emit_state),
        grid=(n_batch // n_seq,),
        in_specs=in_specs,
        out_specs=out_specs,
        out_shape=out_shape,
        scratch_shapes=[pltpu.VMEM((n_seq, 2, WIDTH_C, QK_C_PAD), F32)],
        compiler_params=_params(1),
    )(*args)


def _group_rms(x, same_group_bf16, group, gain):
    sq = x * x
    hi = sq.astype(BF16)
    mid = (sq - hi.astype(F32)).astype(BF16)
    ms = (_dot(hi, same_group_bf16) + _dot(mid, same_group_bf16)) * (1.0 / group)
    return x * lax.rsqrt(ms + EPS) * gain


def _merge_ffn_kernel(x_ref, mod_ref, oa_ref, ob_ref, oc_ref, rc_ref, na_ref, nb_ref, nc_ref,
                      gb_ref, gc_ref, wo_ref, gain_ref, w13_ref, w2_ref, fin_ref, o_ref, acc_ref,
                      *, lam_init, final):
    a = _rms(oa_ref[...], na_ref[...])
    b = _group_rms(ob_ref[...], gb_ref[...], DV_B, nb_ref[...]) * (1.0 - lam_init)
    c = _group_rms(oc_ref[...], gc_ref[...], DV_C, nc_ref[...]) * _silu(rc_ref[...])
    mixed = (_dot(a.astype(BF16), wo_ref[0:WIDTH_A, :])
             + _dot(b.astype(BF16), wo_ref[WIDTH_A:WIDTH_A + WIDTH_B, :])
             + _dot(c.astype(BF16), wo_ref[WIDTH_A + WIDTH_B:, :]))
    x = x_ref[...] + mod_ref[0, 5:6, :] * mixed
    y = _ffn_block(x, mod_ref, gain_ref[...], w13_ref, w2_ref, acc_ref, 6)
    if final:
        y = _rms(y, fin_ref[...])
    o_ref[...] = y


def _merge_ffn(x, mod, tokens_per_group, oa, ob, oc, rc, lw, consts, fin, layer, lam_init, final):
    n = x.shape[0]
    row = lambda w: pl.BlockSpec((TOKEN_TILE, w), lambda i: (i, 0))
    return pl.pallas_call(
        functools.partial(_merge_ffn_kernel, lam_init=lam_init, final=final),
        grid=(n // TOKEN_TILE,),
        in_specs=[row(D_MODEL), _mod_spec(tokens_per_group), row(WIDTH_A), row(WIDTH_B), row(WIDTH_C),
                  row(WIDTH_C), _layer_spec((1, WIDTH_A), layer), _layer_spec((1, WIDTH_B), layer),
                  _layer_spec((1, WIDTH_C), layer), _const_spec((WIDTH_B, WIDTH_B)),
                  _const_spec((WIDTH_C, WIDTH_C)), _layer_spec((D_MODEL, D_MODEL), layer, 1),
                  _layer_spec((1, D_MODEL), layer), _layer_spec((D_MODEL, 2 * D_FF), layer, 1),
                  _layer_spec((D_FF, D_MODEL), layer, 1), _const_spec((1, D_MODEL))],
        out_specs=row(D_MODEL),
        out_shape=jax.ShapeDtypeStruct((n, D_MODEL), F32),
        scratch_shapes=[pltpu.VMEM((TOKEN_TILE, D_MODEL), F32)],
        compiler_params=_params(1),
    )(x, mod, oa, ob, oc, rc, lw["out_norm_a"], lw["out_norm_b"], lw["out_norm_c"],
      consts["group_b"], consts["group_c"], lw["wo"], lw["norm_ffn2"], lw["ffn2_w13"], lw["ffn2_w2"], fin)


def _prepare_weights(w):
    n_layers = w["w_in"].shape[0]
    row = lambda v: v.reshape(n_layers, 1, -1)

    def pad_heads(m, width):
        m = m.reshape(n_layers, m.shape[1], H_A, width)
        m = jnp.pad(m, ((0, 0), (0, 0), (0, 0), (0, HEAD_PAD - width)))
        return m.reshape(n_layers, m.shape[1], H_A * HEAD_PAD)

    def gate_block(wg, first):
        lo = 0 if first else QK_C_PAD
        return jnp.pad(wg, ((0, 0), (0, 0), (lo, 2 * QK_C_PAD - lo - H_C * DK_C)))

    wg = jnp.concatenate([gate_block(w["gla_wg_f"], True), gate_block(w["gla_wg_b"], False),
                          jnp.zeros((n_layers, LANE - 2 * GATE_RANK, 2 * QK_C_PAD), F32)], axis=1)
    bg = gate_block(w["gla_bg_f"][:, None, :], True) + gate_block(w["gla_bg_b"][:, None, :], False)
    return {
        "norm_ffn1": row(w["norm_ffn1"]), "ffn1_w13": w["ffn1_w13"].astype(BF16),
        "ffn1_w2": w["ffn1_w2"].astype(BF16),
        "norm_ffn2": row(w["norm_ffn2"]), "ffn2_w13": w["ffn2_w13"].astype(BF16),
        "ffn2_w2": w["ffn2_w2"].astype(BF16),
        "norm_mix": row(w["norm_mix"]), "w_in": w["w_in"],
        "q_norm": row(w["mla_q_norm"]),
        "wuq": pad_heads(w["mla_w_uq"], DN_A + DR_A).astype(BF16),
        "kv_norm": row(w["mla_kv_norm"]),
        "wuk": pad_heads(w["mla_w_uk"], DN_A).astype(BF16),
        "wuvt": jnp.swapaxes(w["mla_w_uv"], 1, 2).astype(BF16),
        "wg": wg.astype(BF16), "bg": bg,
        "lam": jnp.stack([w["diff_lq1"], w["diff_lk1"], w["diff_lq2"], w["diff_lk2"]], axis=1),
        "out_norm_a": row(w["mla_out_norm"]),
        "out_norm_b": row(jnp.tile(w["diff_norm"], (1, H_B))),
        "out_norm_c": row(jnp.tile(w["gla_norm"], (1, H_C))),
        "wo": w["w_out"].astype(BF16),
    }


def _rope_tables(n_pos):
    pos = jnp.arange(n_pos)
    rows = (pos // GRID_W).astype(F32)
    cols = (pos % GRID_W).astype(F32)
    half = DH_B // 2
    inv = ROPE_BASE ** (-jnp.arange(0, half, 2, dtype=F32) / half)
    ang = jnp.concatenate([rows[:, None] * inv, rows[:, None] * inv,
                           cols[:, None] * inv, cols[:, None] * inv], axis=1)
    sign = jnp.tile(jnp.concatenate([-jnp.ones(8, F32), jnp.ones(8, F32)]), 2)
    cos32, sin32 = jnp.cos(ang), jnp.sin(ang) * sign
    cos_full, sin_full = jnp.tile(cos32, (1, LANE // 32)), jnp.tile(sin32, (1, LANE // 32))
    ones = lambda n: jnp.ones((n_pos, n), F32)
    zeros = lambda n: jnp.zeros((n_pos, n), F32)
    cos_head = jnp.concatenate([ones(ROPE_OFF), cos32, ones(LANE - ROPE_OFF - DR_A)], axis=1)
    sin_head = jnp.concatenate([zeros(ROPE_OFF), sin32, zeros(LANE - ROPE_OFF - DR_A)], axis=1)
    return cos_head, sin_head, cos_full, sin_full


def _same_group_matrix(width, group):
    idx = jnp.arange(width) // group
    return jnp.where(idx[:, None] == idx[None, :], 1.0, 0.0).astype(BF16)


def _state_to_blockdiag_t(s):
    st = jnp.swapaxes(s, 2, 3)
    blocks = [jnp.pad(st[:, h], ((0, 0), (0, 0), (h * DK_C, QK_C_PAD - (h + 1) * DK_C))) for h in range(H_C)]
    return jnp.concatenate(blocks, axis=1)


def _blockdiag_t_to_state(st):
    return jnp.stack([jnp.swapaxes(st[:, h * DV_C:(h + 1) * DV_C, h * DK_C:(h + 1) * DK_C], 1, 2)
                      for h in range(H_C)], axis=1)


def kernel(x_prompt, x_sample, cache_mla_ckv, cache_mla_krope, cache_diff_k, cache_diff_v, state_gla_fwd, state_gla_bwd, c, c_ctx, w_mod, b_mod, norm_ffn1, ffn1_w13, ffn1_w2, norm_mix, w_in, mla_q_norm, mla_w_uq, mla_kv_norm, mla_w_uk, mla_w_uv, mla_out_norm, diff_lq1, diff_lk1, diff_lq2, diff_lk2, diff_norm, gla_wg_f, gla_bg_f, gla_wg_b, gla_bg_b, gla_norm, w_out, norm_ffn2, ffn2_w13, ffn2_w2, final_norm):
    w = dict(norm_ffn1=norm_ffn1, ffn1_w13=ffn1_w13, ffn1_w2=ffn1_w2, norm_mix=norm_mix, w_in=w_in,
             mla_q_norm=mla_q_norm, mla_w_uq=mla_w_uq, mla_kv_norm=mla_kv_norm, mla_w_uk=mla_w_uk,
             mla_w_uv=mla_w_uv, mla_out_norm=mla_out_norm, diff_lq1=diff_lq1, diff_lk1=diff_lk1,
             diff_lq2=diff_lq2, diff_lk2=diff_lk2, diff_norm=diff_norm, gla_wg_f=gla_wg_f,
             gla_bg_f=gla_bg_f, gla_wg_b=gla_wg_b, gla_bg_b=gla_bg_b, gla_norm=gla_norm, w_out=w_out,
             norm_ffn2=norm_ffn2, ffn2_w13=ffn2_w13, ffn2_w2=ffn2_w2)
    n_ctx_b, n_ctx_t, _ = x_prompt.shape
    n_lat_b, n_lat_t, _ = x_sample.shape
    n_past = cache_mla_ckv.shape[2]

    cvec = jnp.concatenate([c_ctx[None, :], c, jnp.zeros((16 - 1 - n_lat_b, D_MODEL), F32)], axis=0)
    mod = _modulation(cvec, w_mod, b_mod).reshape(DEPTH, 16, N_MOD, D_MODEL)

    rope_tabs = _rope_tables(n_lat_t)
    consts = {"group_b": _same_group_matrix(WIDTH_B, DV_B), "group_c": _same_group_matrix(WIDTH_C, DV_C)}
    cache_kr = jnp.pad(cache_mla_krope, ((0, 0), (0, 0), (0, 0), (ROPE_OFF, LANE - ROPE_OFF - DR_A)))
    cache_dk = cache_diff_k.reshape(n_lat_b, DEPTH, n_past, WIDTH_B)
    cache_dv = cache_diff_v.reshape(n_lat_b, DEPTH, n_past, WIDTH_B)
    fin = final_norm.reshape(1, D_MODEL)

    xp = x_prompt.reshape(n_ctx_b * n_ctx_t, D_MODEL)
    xs = x_sample.reshape(n_lat_b * n_lat_t, D_MODEL)
    new = {k: [] for k in ("ckv", "kr", "dk", "dv", "sf", "sb")}
    lw = _prepare_weights(w)
    for l in range(DEPTH):
        lam_init = 0.8 - 0.6 * math.exp(-0.3 * l)
        last = l == DEPTH - 1
        mod_ctx, mod_lat = mod[l, 0:1], mod[l, 1:1 + n_lat_b]

        def trunk(x, m, tpg, n_batch, n_tok, rope, mla_cache, diff_cache, gla_init, emit_state):
            x = _ffn(x, m, tpg, lw["norm_ffn1"], lw["ffn1_w13"], lw["ffn1_w2"], l, mod_row=0)
            p = _project(x, m, tpg, lw, lw["w_in"], l, rope)
            oa = _mla(p, lw, n_batch, n_tok, mla_cache, l)
            ob = _diff(p, lw, n_batch, n_tok, diff_cache, l, lam_init)
            gla_out = _gla(p, n_batch, n_tok, gla_init, emit_state)
            x = _merge_ffn(x, m, tpg, oa, ob, gla_out[0], p["rc"], lw, consts, fin, l, lam_init, last)
            return x, p, gla_out

        xp, p, gla_out = trunk(xp, mod_ctx, n_ctx_b * n_ctx_t, n_ctx_b, n_ctx_t, None, None, None, None, True)
        new["ckv"].append(p["ckv"].reshape(n_ctx_b, n_ctx_t, KV_RANK))
        new["kr"].append(p["kr"][:, ROPE_OFF:ROPE_OFF + DR_A].reshape(n_ctx_b, n_ctx_t, DR_A))
        new["dk"].append(p["kb"].reshape(n_ctx_b, n_ctx_t, H_B, 2 * DH_B))
        new["dv"].append(p["vb"].reshape(n_ctx_b, n_ctx_t, H_B, DV_B))
        new["sf"].append(_blockdiag_t_to_state(gla_out[1]))
        new["sb"].append(_blockdiag_t_to_state(gla_out[2]))

        gla_init = (_state_to_blockdiag_t(state_gla_fwd[:, l]), _state_to_blockdiag_t(state_gla_bwd[:, l]))
        xs, _, _ = trunk(xs, mod_lat, n_lat_t, n_lat_b, n_lat_t, rope_tabs, (cache_mla_ckv, cache_kr),
                         (cache_dk, cache_dv), gla_init, False)

    stack = lambda name: jnp.stack(new[name], axis=1)
    return (xp.reshape(x_prompt.shape), xs.reshape(x_sample.shape), stack("ckv"), stack("kr"),
            stack("dk"), stack("dv"), stack("sf"), stack("sb"))
```

```python
import functools
import math

import jax
import jax.numpy as jnp
from jax import lax
from jax.experimental import pallas as pl
from jax.experimental.pallas import tpu as pltpu

F32 = jnp.float32
BF16 = jnp.bfloat16

D_MODEL = 1024
DEPTH = 2
GRID_W = 64
ROPE_BASE = 10000.0
EPS = 1e-6
H_A, DN_A, DR_A, DV_A = 6, 64, 32, 64
Q_RANK, KV_RANK = 384, 256
H_B, DH_B, DV_B = 4, 32, 64
H_C, DK_C, DV_C = 4, 48, 96
GATE_RANK = 16
GATE_NORM = 16.0
CHUNK = 64
WIDTH_A = H_A * DV_A
WIDTH_B = H_B * DV_B
WIDTH_C = H_C * DV_C
D_FF = 2816
N_MOD = 9

LANE = 128
FF_CHUNK = 256
TOKEN_TILE = 512
Q_TILE = 1024
KEY_BLOCK = 256
ONES_ROWS = 16
MOD_TILE = 1152
GLA_UNROLL = 4
ATTN_SEQS = 4
GLA_SEQS = 2
VMEM_LIMIT = 56 * 1024 * 1024

HEAD_PAD = LANE
ROPE_OFF = DN_A
QK_C_PAD = 256
SEG = {}
_off = 0
for _name, _w in (("cq", Q_RANK), ("krope", LANE), ("ckv", KV_RANK), ("qb", 256), ("kb", 256),
                  ("vb", 256), ("qc", QK_C_PAD), ("kc", QK_C_PAD), ("vc", WIDTH_C), ("gl", LANE),
                  ("rc", WIDTH_C)):
    SEG[_name] = (_off, _off + _w)
    _off += _w
PROJ_PAD = _off

PROJ_SPLITS = (Q_RANK, KV_RANK, DR_A, 2 * H_B * DH_B, 2 * H_B * DH_B, H_B * DV_B, H_C * DK_C, H_C * DK_C,
               WIDTH_C, WIDTH_C, GATE_RANK, GATE_RANK)
PROJ_COLS = sum(PROJ_SPLITS)
_SRC = [sum(PROJ_SPLITS[:i]) for i in range(len(PROJ_SPLITS) + 1)]


def _src(i, j=None):
    return ("src", _SRC[i], _SRC[(i if j is None else j) + 1])


PROJ_LAYOUT = (
    ("cq", (_src(0),)), ("ckv", (_src(1),)),
    ("krope", (("zero", 0, ROPE_OFF), _src(2), ("zero", 0, LANE - ROPE_OFF - DR_A))),
    ("qb", (_src(3),)), ("kb", (_src(4),)), ("vb", (_src(5),)),
    ("qc", (_src(6), ("zero", 0, QK_C_PAD - H_C * DK_C))),
    ("kc", (_src(7), ("zero", 0, QK_C_PAD - H_C * DK_C))),
    ("vc", (_src(8),)), ("rc", (_src(9),)),
    ("gl", (_src(10, 11), ("zero", 0, LANE - 2 * GATE_RANK))),
)


def _dot(a, b):
    return jnp.dot(a, b, preferred_element_type=F32)


def _dot_nt(a, b):
    return lax.dot_general(a, b, (((1,), (1,)), ((), ())), preferred_element_type=F32)


def _dot_tn(a, b):
    return lax.dot_general(a, b, (((0,), (0,)), ((), ())), preferred_element_type=F32)


def _rms(x, gain):
    ms = jnp.mean(x * x, axis=-1, keepdims=True)
    return (x * lax.rsqrt(ms + EPS)) * gain


def _silu(x):
    return x * jax.nn.sigmoid(x)


def _const_spec(shape):
    nd = len(shape)
    return pl.BlockSpec(shape, lambda *_: (0,) * nd)


def _layer_spec(shape, layer, buffers=None):
    nd = len(shape)
    mode = {} if buffers is None else {"pipeline_mode": pl.Buffered(buffers)}
    return pl.BlockSpec((None,) + tuple(shape), lambda *_: (layer,) + (0,) * nd, **mode)


def _params(n_axes):
    return pltpu.CompilerParams(dimension_semantics=("arbitrary",) * n_axes,
                                vmem_limit_bytes=VMEM_LIMIT)


def _mod_kernel(c_ref, w_ref, b_ref, o_ref):
    s = _silu(c_ref[...]).astype(BF16)
    o_ref[0] = _dot(s, w_ref[0].astype(BF16)) + b_ref[0]


def _modulation(cvec, w_mod, b_mod):
    n_rows = cvec.shape[0]
    width = N_MOD * D_MODEL
    return pl.pallas_call(
        _mod_kernel,
        grid=(DEPTH, width // MOD_TILE),
        in_specs=[
            pl.BlockSpec((n_rows, D_MODEL), lambda l, j: (0, 0)),
            pl.BlockSpec((1, D_MODEL, MOD_TILE), lambda l, j: (l, 0, j)),
            pl.BlockSpec((1, 1, MOD_TILE), lambda l, j: (l, 0, j)),
        ],
        out_specs=pl.BlockSpec((1, n_rows, MOD_TILE), lambda l, j: (l, 0, j)),
        out_shape=jax.ShapeDtypeStruct((DEPTH, n_rows, width), F32),
        compiler_params=_params(2),
    )(cvec, w_mod, b_mod.reshape(DEPTH, 1, width))


def _mod_spec(tokens_per_group):
    return pl.BlockSpec((1, N_MOD, D_MODEL), lambda i: ((i * TOKEN_TILE) // tokens_per_group, 0, 0))


def _ada_norm(x, gain, mod_ref, first_row):
    shift = mod_ref[0, first_row:first_row + 1, :]
    scale = mod_ref[0, first_row + 1:first_row + 2, :]
    return _rms(x, gain) * (1.0 + scale) + shift


def _ffn_block(x, mod_ref, gain, w13_ref, w2_ref, acc_ref, mod_row):
    u = _ada_norm(x, gain, mod_ref, mod_row).astype(BF16)
    for c in range(D_FF // FF_CHUNK):
        lo, hi = c * FF_CHUNK, (c + 1) * FF_CHUNK
        a = _dot(u, w13_ref[:, lo:hi])
        b = _dot(u, w13_ref[:, D_FF + lo:D_FF + hi])
        t = _dot((_silu(a) * b).astype(BF16), w2_ref[lo:hi, :])
        if c == 0:
            acc_ref[...] = t
        else:
            acc_ref[...] += t
    gate = mod_ref[0, mod_row + 2:mod_row + 3, :]
    return x + (0.5 * gate) * acc_ref[...]


def _ffn_kernel(x_ref, mod_ref, gain_ref, w13_ref, w2_ref, o_ref, acc_ref, *, mod_row):
    o_ref[...] = _ffn_block(x_ref[...], mod_ref, gain_ref[...], w13_ref, w2_ref, acc_ref, mod_row)


def _ffn(x, mod, tokens_per_group, gain, w13, w2, layer, *, mod_row):
    n = x.shape[0]
    tile = pl.BlockSpec((TOKEN_TILE, D_MODEL), lambda i: (i, 0))
    return pl.pallas_call(
        functools.partial(_ffn_kernel, mod_row=mod_row),
        grid=(n // TOKEN_TILE,),
        in_specs=[tile, _mod_spec(tokens_per_group), _layer_spec((1, D_MODEL), layer),
                  _layer_spec((D_MODEL, 2 * D_FF), layer, 1), _layer_spec((D_FF, D_MODEL), layer, 1)],
        out_specs=tile,
        out_shape=jax.ShapeDtypeStruct((n, D_MODEL), F32),
        scratch_shapes=[pltpu.VMEM((TOKEN_TILE, D_MODEL), F32)],
        compiler_params=_params(1),
    )(x, mod, gain, w13, w2)


def _rope(x, cos, sin_signed):
    first = (lax.broadcasted_iota(jnp.int32, (x.shape[0], LANE), 1) % 16) < 8
    outs = []
    for j in range(x.shape[1] // LANE):
        xj = x[:, j * LANE:(j + 1) * LANE]
        partner = jnp.where(first, pltpu.roll(xj, LANE - 8, 1), pltpu.roll(xj, 8, 1))
        outs.append(xj * cos + partner * sin_signed)
    return outs[0] if len(outs) == 1 else jnp.concatenate(outs, axis=1)


def _log_sigmoid(z):
    return jnp.minimum(z, 0.0) - jnp.log1p(jnp.exp(-jnp.abs(z)))


def _proj_kernel(*refs, rope, n_carried, seq_len):
    (x_ref, mod_ref, gain_ref, wint_ref, qn_ref, wuq_ref, kvn_ref, wg_ref, bg_ref) = refs[:9]
    refs = refs[9:]
    if rope:
        ch_ref, sh_ref, cf_ref, sf_ref = refs[:4]
        refs = refs[4:]
    refs = refs[n_carried:]
    (q_o, ckv_o, kr_o, qb_o, kb_o, vb_o, qc_o, kc_o, vc_o, rc_o, gf_o, gb_o) = refs[:12]
    refs = refs[12:]
    if seq_len:
        ckv_all_o, kbt_all_o, vbt_all_o = refs[:3]
        refs = refs[3:]
    wpt_ref, = refs

    @pl.when(pl.program_id(0) == 0)
    def _():
        for name, pieces in PROJ_LAYOUT:
            parts = []
            for kind, a, b in pieces:
                parts.append(jnp.zeros((b - a, D_MODEL), F32) if kind == "zero" else wint_ref[a:b, :])
            blk = parts[0] if len(parts) == 1 else jnp.concatenate(parts, axis=0)
            lo, hi = SEG[name]
            wpt_ref[lo:hi, :] = blk.astype(BF16)

    u = _ada_norm(x_ref[...], gain_ref[...], mod_ref, 3).astype(BF16)

    def seg(first, last=None):
        lo, hi = SEG[first][0], SEG[last or first][1]
        return _dot_nt(u, wpt_ref[lo:hi, :])

    cq_kr = seg("cq", "krope")
    q = _dot(_rms(cq_kr[:, :Q_RANK], qn_ref[...]).astype(BF16), wuq_ref[...])
    kr = cq_kr[:, Q_RANK:]
    qb = seg("qb")
    kb = seg("kb")
    if rope:
        q = _rope(q, ch_ref[...], sh_ref[...])
        kr = _rope(kr, ch_ref[...], sh_ref[...])
        qb = _rope(qb, cf_ref[...], sf_ref[...])
        kb = _rope(kb, cf_ref[...], sf_ref[...])
    q_o[...] = (q * (DN_A + DR_A) ** -0.5).astype(BF16)
    ckv = _rms(seg("ckv"), kvn_ref[...])
    ckv_o[...] = ckv
    if seq_len:
        kbt = _dot_nt(wpt_ref[SEG["kb"][0]:SEG["kb"][1], :], u)
        vbt = _dot_nt(wpt_ref[SEG["vb"][0]:SEG["vb"][1], :], u)
        for j in range(u.shape[0] // seq_len):
            tok = slice(j * seq_len, (j + 1) * seq_len)
            ckv_all_o[j] = ckv[tok, :]
            kbt_all_o[j] = kbt[:, tok]
            vbt_all_o[j] = vbt[:, tok]
    kr_o[...] = kr
    qb_o[...] = (qb * DH_B ** -0.5).astype(BF16)
    kb_o[...] = kb
    vb_o[...] = seg("vb")
    qc_o[...] = seg("qc") * (DK_C ** -0.5)
    kc_o[...] = seg("kc")
    vc_gl = seg("vc", "gl")
    vc_o[...] = vc_gl[:, :WIDTH_C]
    rc_o[...] = seg("rc")
    z = _dot(vc_gl[:, WIDTH_C:].astype(BF16), wg_ref[...]) + bg_ref[...]
    g = _log_sigmoid(z) / GATE_NORM
    gf_o[...] = g[:, :QK_C_PAD]
    gb_o[...] = g[:, QK_C_PAD:]


PROJ_OUT = (("q", H_A * HEAD_PAD, BF16), ("ckv", KV_RANK, F32), ("kr", LANE, F32), ("qb", 256, BF16),
            ("kb", 256, F32), ("vb", 256, F32), ("qc", QK_C_PAD, F32), ("kc", QK_C_PAD, F32),
            ("vc", WIDTH_C, F32), ("rc", WIDTH_C, F32), ("gf", QK_C_PAD, F32), ("gb", QK_C_PAD, F32))


def _project(x, mod, tokens_per_group, lw, w_in, layer, rope_tabs, seq_len=0, carried=None):
    n = x.shape[0]
    rope = rope_tabs is not None
    row = lambda w: pl.BlockSpec((TOKEN_TILE, w), lambda i: (i, 0))
    in_specs = [row(D_MODEL), _mod_spec(tokens_per_group), _layer_spec((1, D_MODEL), layer),
                _layer_spec((PROJ_COLS, D_MODEL), layer, 1), _layer_spec((1, Q_RANK), layer),
                _layer_spec((Q_RANK, H_A * HEAD_PAD), layer), _layer_spec((1, KV_RANK), layer),
                _layer_spec((LANE, 2 * QK_C_PAD), layer), _layer_spec((1, 2 * QK_C_PAD), layer)]
    args = [x, mod, lw["norm_mix"], w_in, lw["q_norm"], lw["wuq"], lw["kv_norm"], lw["wg"], lw["bg"]]
    if rope:
        n_pos = rope_tabs[0].shape[0]
        tab = pl.BlockSpec((TOKEN_TILE, LANE), lambda i: (i % (n_pos // TOKEN_TILE), 0))
        in_specs += [tab] * 4
        args += list(rope_tabs)
    out_specs = [row(w) for _, w, _ in PROJ_OUT]
    out_shape = [jax.ShapeDtypeStruct((n, w), dt) for _, w, dt in PROJ_OUT]
    aliases = {}
    if seq_len:
        per_tile = TOKEN_TILE // seq_len
        if carried is not None:
            for k in range(len(carried)):
                aliases[len(args) + k] = len(out_specs) + k
            in_specs += [pl.BlockSpec(memory_space=pl.ANY)] * len(carried)
            args += list(carried)
        for shape in ((seq_len, KV_RANK), (WIDTH_B, seq_len), (WIDTH_B, seq_len)):
            out_specs.append(pl.BlockSpec((per_tile, None) + shape, lambda i: (i, layer, 0, 0)))
            out_shape.append(jax.ShapeDtypeStruct((n // seq_len, DEPTH) + shape, F32))
    outs = pl.pallas_call(
        functools.partial(_proj_kernel, rope=rope, n_carried=0 if carried is None else len(carried),
                          seq_len=seq_len),
        grid=(n // TOKEN_TILE,),
        in_specs=in_specs,
        out_specs=out_specs,
        out_shape=out_shape,
        input_output_aliases=aliases,
        scratch_shapes=[pltpu.VMEM((PROJ_PAD, D_MODEL), BF16)],
        compiler_params=_params(1),
    )(*args)
    result = {name: o for (name, _, _), o in zip(PROJ_OUT, outs)}
    if seq_len:
        result["cache"] = tuple(outs[len(PROJ_OUT):])
    return result


def _lane_mask(width, lo, hi):
    lane = lax.broadcasted_iota(jnp.int32, (1, width), 1)
    return jnp.where((lane >= lo) & (lane < hi), 1.0, 0.0).astype(F32)


def _attend_t(units, dv):
    outs = []
    q, keys, vt_ext = units[0]()
    s = _dot_nt(q, keys)
    for u in range(len(units)):
        s_now, vt_now = s, vt_ext
        if u + 1 < len(units):
            q, keys, vt_ext = units[u + 1]()
            s = _dot_nt(q, keys)
        e = jnp.exp(s_now - jnp.max(s_now, axis=-1, keepdims=True)).astype(BF16)
        r = _dot_nt(vt_now, e)
        outs.append(r[0:dv] * (1.0 / r[dv:dv + 1]))
    return outs


def _attn_seqs(n_cache, n_q_tiles):
    return ATTN_SEQS if (n_cache == 0 and n_q_tiles == 1) else 1


def _mla_kernel(*refs, n_cache, n_seq, n_tok):
    if n_cache:
        q_ref, ckv_ref, kr_ref, cckv_ref, ckr_ref, wuk_ref, wuvt_ref, o_ref, kf_scr, vt_scr = refs
    else:
        q_ref, ckv_ref, kr_ref, wuk_ref, wuvt_ref, o_ref, kf_scr, vt_scr = refs

    @pl.when(pl.program_id(1) == 0)
    def _():
        def fill(s, row0, ckv, kr):
            rows = ckv.shape[0]
            cb = ckv.astype(BF16)
            kr_all = jnp.concatenate([kr] * H_A, axis=1)
            kf_scr[s, row0:row0 + rows, :] = (_dot(cb, wuk_ref[...]) + kr_all).astype(BF16)
            vt = _dot_nt(wuvt_ref[...], cb)
            for h in range(H_A):
                vt_scr[s * H_A + h, 0:DV_A, row0:row0 + rows] = vt[h * DV_A:(h + 1) * DV_A].astype(BF16)

        for s in range(n_seq):
            if n_cache:
                fill(s, 0, cckv_ref[...], ckr_ref[...])
            tok = slice(s * n_tok, (s + 1) * n_tok)
            fill(s, n_cache, ckv_ref[tok, :], kr_ref[tok, :])
        for u in range(n_seq * H_A):
            vt_scr[u, DV_A:, :] = jnp.ones((ONES_ROWS, vt_scr.shape[2]), BF16)

    n_q = q_ref.shape[0] // n_seq

    def unit(s, h):
        sl = slice(h * HEAD_PAD, (h + 1) * HEAD_PAD)
        return lambda: (q_ref[s * n_q:(s + 1) * n_q, sl], kf_scr[s, :, sl], vt_scr[s * H_A + h])

    heads = _attend_t([unit(s, h) for h in range(H_A) for s in range(n_seq)], DV_A)
    for s in range(n_seq):
        o_ref[s * n_q:(s + 1) * n_q, :] = jnp.concatenate(heads[s::n_seq], axis=0).T


def _mla(p, lw, n_batch, n_tok, cache, layer):
    n_cache = 0 if cache is None else cache[0].shape[2]
    q_tile = min(Q_TILE, n_tok)
    nqt = n_tok // q_tile
    n_seq = _attn_seqs(n_cache, nqt)
    in_specs = [pl.BlockSpec((n_seq * q_tile, H_A * HEAD_PAD), lambda b, i: (b * nqt + i, 0)),
                pl.BlockSpec((n_seq * n_tok, KV_RANK), lambda b, i: (b, 0)),
                pl.BlockSpec((n_seq * n_tok, LANE), lambda b, i: (b, 0))]
    args = [p["q"], p["ckv"], p["kr"]]
    if n_cache:
        in_specs += [pl.BlockSpec((None, None, n_cache, KV_RANK), lambda b, i: (b, layer, 0, 0)),
                     pl.BlockSpec((None, None, n_cache, LANE), lambda b, i: (b, layer, 0, 0))]
        args += list(cache)
    in_specs += [_layer_spec((KV_RANK, H_A * HEAD_PAD), layer), _layer_spec((WIDTH_A, KV_RANK), layer)]
    args += [lw["wuk"], lw["wuvt"]]
    n_keys = n_cache + n_tok
    return pl.pallas_call(
        functools.partial(_mla_kernel, n_cache=n_cache, n_seq=n_seq, n_tok=n_tok),
        grid=(n_batch // n_seq, nqt),
        in_specs=in_specs,
        out_specs=pl.BlockSpec((n_seq * q_tile, WIDTH_A), lambda b, i: (b * nqt + i, 0)),
        out_shape=jax.ShapeDtypeStruct((n_batch * n_tok, WIDTH_A), F32),
        scratch_shapes=[pltpu.VMEM((n_seq, n_keys, H_A * HEAD_PAD), BF16),
                        pltpu.VMEM((n_seq * H_A, DV_A + ONES_ROWS, n_keys), BF16)],
        compiler_params=_params(2),
    )(*args)


def _diff_kernel(*refs, n_cache, n_seq, n_tok, lam_init):
    if n_cache:
        q_ref, k_ref, v_ref, ck_ref, cv_ref, lam_ref, o_ref, k_scr, vt_scr = refs
    else:
        q_ref, k_ref, v_ref, lam_ref, o_ref, k_scr, vt_scr = refs

    @pl.when(pl.program_id(1) == 0)
    def _():
        def fill(s, row0, k, v):
            rows = k.shape[0]
            k_scr[s, row0:row0 + rows, :] = k.astype(BF16)
            vt = v.T
            for h in range(H_B):
                vt_scr[s * H_B + h, 0:DV_B, row0:row0 + rows] = vt[h * DV_B:(h + 1) * DV_B].astype(BF16)

        for s in range(n_seq):
            if n_cache:
                fill(s, 0, ck_ref[...], cv_ref[...])
            tok = slice(s * n_tok, (s + 1) * n_tok)
            fill(s, n_cache, k_ref[tok, :], v_ref[tok, :])
        for u in range(n_seq * H_B):
            vt_scr[u, DV_B:, :] = jnp.ones((ONES_ROWS, vt_scr.shape[2]), BF16)

    lv = lam_ref[...]
    lam = (jnp.exp(jnp.sum(lv[0:1] * lv[1:2], axis=-1, keepdims=True))
           - jnp.exp(jnp.sum(lv[2:3] * lv[3:4], axis=-1, keepdims=True)) + lam_init)
    n_q = q_ref.shape[0] // n_seq
    lane = lax.broadcasted_iota(jnp.int32, (n_q, LANE), 1)

    def unit(s, h, j):
        blk = slice((h // 2) * LANE, (h // 2 + 1) * LANE)
        lo = (h % 2) * 2 * DH_B + j * DH_B

        def fn():
            qh = q_ref[s * n_q:(s + 1) * n_q, blk]
            qm = jnp.where((lane >= lo) & (lane < lo + DH_B), qh, jnp.zeros_like(qh))
            return qm, k_scr[s, :, blk], vt_scr[s * H_B + h]
        return fn

    maps = _attend_t([unit(s, h, j) for h in range(H_B) for j in range(2) for s in range(n_seq)], DV_B)
    for s in range(n_seq):
        mine = maps[s::n_seq]
        heads = [mine[2 * h] - lam * mine[2 * h + 1] for h in range(H_B)]
        o_ref[s * n_q:(s + 1) * n_q, :] = jnp.concatenate(heads, axis=0).T


def _diff(p, lw, n_batch, n_tok, cache, layer, lam_init):
    n_cache = 0 if cache is None else cache[0].shape[2]
    q_tile = min(Q_TILE, n_tok)
    nqt = n_tok // q_tile
    n_seq = _attn_seqs(n_cache, nqt)
    in_specs = [pl.BlockSpec((n_seq * q_tile, WIDTH_B), lambda b, i: (b * nqt + i, 0)),
                pl.BlockSpec((n_seq * n_tok, WIDTH_B), lambda b, i: (b, 0)),
                pl.BlockSpec((n_seq * n_tok, WIDTH_B), lambda b, i: (b, 0))]
    args = [p["qb"], p["kb"], p["vb"]]
    if n_cache:
        in_specs += [pl.BlockSpec((None, None, n_cache, WIDTH_B), lambda b, i: (b, layer, 0, 0))] * 2
        args += list(cache)
    in_specs.append(_layer_spec((4, DH_B), layer))
    args.append(lw["lam"])
    n_keys = n_cache + n_tok
    return pl.pallas_call(
        functools.partial(_diff_kernel, n_cache=n_cache, n_seq=n_seq, n_tok=n_tok, lam_init=lam_init),
        grid=(n_batch // n_seq, nqt),
        in_specs=in_specs,
        out_specs=pl.BlockSpec((n_seq * q_tile, WIDTH_B), lambda b, i: (b * nqt + i, 0)),
        out_shape=jax.ShapeDtypeStruct((n_batch * n_tok, WIDTH_B), F32),
        scratch_shapes=[pltpu.VMEM((n_seq, n_keys, WIDTH_B), BF16),
                        pltpu.VMEM((n_seq * H_B, DV_B + ONES_ROWS, n_keys), BF16)],
        compiler_params=_params(2),
    )(*args)


def _split3(x):
    hi = x.astype(BF16)
    r1 = x - hi.astype(F32)
    mid = r1.astype(BF16)
    lo = (r1 - mid.astype(F32)).astype(BF16)
    return hi, mid, lo


def _dot_exact_lhs(a_bf16, x):
    hi, mid, lo = _split3(x)
    return _dot(a_bf16, hi) + _dot(a_bf16, mid) + _dot(a_bf16, lo)


def _gla_kernel(*refs, n_tok, n_seq, has_init, emit_state):
    q_ref, k_ref, v_ref, gf_ref, gb_ref = refs[:5]
    refs = refs[5:]
    if has_init:
        s0f_ref, s0b_ref = refs[:2]
        refs = refs[2:]
    o_ref = refs[0]
    refs = refs[1:]
    if emit_state:
        sf_ref, sb_ref = refs[:2]
        refs = refs[2:]
    st_scr, = refs

    n_chunks = n_tok // CHUNK
    n_groups = n_chunks // GLA_UNROLL
    t_idx = lax.broadcasted_iota(jnp.int32, (CHUNK, CHUNK), 0)
    s_idx = lax.broadcasted_iota(jnp.int32, (CHUNK, CHUNK), 1)
    tri = [jnp.where(s_idx <= t_idx, 1.0, 0.0).astype(F32), jnp.where(s_idx >= t_idx, 1.0, 0.0).astype(F32)]
    tri4 = [jnp.concatenate([t] * H_C, axis=1) for t in tri]
    tri_b = [t.astype(BF16) for t in tri]
    head_k = [_lane_mask(QK_C_PAD, h * DK_C, (h + 1) * DK_C) for h in range(H_C)]
    head_v = [_lane_mask(WIDTH_C, h * DV_C, (h + 1) * DV_C) for h in range(H_C)]
    row_v = lax.broadcasted_iota(jnp.int32, (WIDTH_C, QK_C_PAD), 0)
    col_k = lax.broadcasted_iota(jnp.int32, (WIDTH_C, QK_C_PAD), 1)
    diag = jnp.zeros((WIDTH_C, QK_C_PAD), F32)
    for h in range(H_C):
        inside = ((row_v >= h * DV_C) & (row_v < (h + 1) * DV_C)
                  & (col_k >= h * DK_C) & (col_k < (h + 1) * DK_C))
        diag = jnp.where(inside, 1.0, diag)

    for s in range(n_seq):
        for d, s0_ref in enumerate((s0f_ref, s0b_ref) if has_init else (None, None)):
            st_scr[s, d] = jnp.zeros((WIDTH_C, QK_C_PAD), F32) if s0_ref is None else s0_ref[s]
    o_ref[...] = jnp.zeros_like(o_ref)

    def chunk_steps(chains):
        loaded = []
        for s, d, c in chains:
            g_ref = gb_ref if d else gf_ref
            row0 = s * n_tok + c * CHUNK
            rows = pl.ds(row0 if isinstance(row0, int) else pl.multiple_of(row0, CHUNK), CHUNK)
            bcum = _dot_exact_lhs(tri_b[d], g_ref[rows, :])
            loaded.append((rows, q_ref[rows, :], k_ref[rows, :], v_ref[rows, :], bcum))
        scaled = []
        for (s, d, c), (rows, q, k, v, bcum) in zip(chains, loaded):
            blast = bcum[0:1, :] if d else bcum[CHUNK - 1:CHUNK, :]
            qe = (q * jnp.exp(bcum)).astype(BF16)
            ke = k * jnp.exp(-bcum)
            kd = (k * jnp.exp(blast - bcum)).astype(BF16)
            ke_heads = jnp.concatenate([ke * head_k[h] for h in range(H_C)], axis=0).astype(BF16)
            v_heads = jnp.concatenate([v * head_v[h] for h in range(H_C)], axis=0).astype(BF16)
            scaled.append((qe, ke_heads, kd, v.astype(BF16), v_heads, jnp.exp(blast)))
        products = []
        for (s, d, c), (qe, ke_heads, kd, vb, v_heads, decay) in zip(chains, scaled):
            st = st_scr[s, d]
            attn = _dot_nt(qe, ke_heads)
            inter = _dot_nt(qe, st.astype(BF16))
            upd = _dot_tn(vb, kd)
            products.append((st, attn, inter, upd))
        masked = []
        for (s, d, c), (qe, ke_heads, kd, vb, v_heads, decay), (st, attn, inter, upd) in zip(
                chains, scaled, products):
            st_scr[s, d] = st * decay + upd * diag
            masked.append((attn * tri4[d]).astype(BF16))
        for (rows, *_), (_, _, _, _, v_heads, _), (_, _, inter, _), attn_b in zip(
                loaded, scaled, products, masked):
            o_ref[rows, :] += _dot(attn_b, v_heads) + inter

    def group(i):
        for j in range(GLA_UNROLL):
            fwd = i * GLA_UNROLL + j
            chunk_steps([(s, d, n_chunks - 1 - fwd if d else fwd) for s in range(n_seq) for d in (0, 1)])

    if n_groups == 1:
        group(0)
    else:
        def body(i, carry):
            group(i)
            return carry
        lax.fori_loop(0, n_groups, body, 0)

    if emit_state:
        for s in range(n_seq):
            sf_ref[s] = st_scr[s, 0]
            sb_ref[s] = st_scr[s, 1]


def _gla(p, n_batch, n_tok, init, emit_state):
    n_seq = GLA_SEQS
    tok = lambda w: pl.BlockSpec((n_seq * n_tok, w), lambda b: (b, 0))
    st_spec = pl.BlockSpec((n_seq, WIDTH_C, QK_C_PAD), lambda b: (b, 0, 0))
    in_specs = [tok(QK_C_PAD), tok(QK_C_PAD), tok(WIDTH_C), tok(QK_C_PAD), tok(QK_C_PAD)]
    args = [p["qc"], p["kc"], p["vc"], p["gf"], p["gb"]]
    if init is not None:
        in_specs += [st_spec] * 2
        args += list(init)
    out_specs = [tok(WIDTH_C)]
    out_shape = [jax.ShapeDtypeStruct((n_batch * n_tok, WIDTH_C), F32)]
    if emit_state:
        out_specs += [st_spec] * 2
        out_shape += [jax.ShapeDtypeStruct((n_batch, WIDTH_C, QK_C_PAD), F32)] * 2
    return pl.pallas_call(
        functools.partial(_gla_kernel, n_tok=n_tok, n_seq=n_seq, has_init=init is not None,
                          emit_state=emit_state),
        grid=(n_batch // n_seq,),
        in_specs=in_specs,
        out_specs=out_specs,
        out_shape=out_shape,
        scratch_shapes=[pltpu.VMEM((n_seq, 2, WIDTH_C, QK_C_PAD), F32)],
        compiler_params=_params(1),
    )(*args)


def _group_rms(x, same_group_bf16, group, gain):
    sq = x * x
    hi = sq.astype(BF16)
    mid = (sq - hi.astype(F32)).astype(BF16)
    ms = (_dot(hi, same_group_bf16) + _dot(mid, same_group_bf16)) * (1.0 / group)
    return x * lax.rsqrt(ms + EPS) * gain


def _merge_ffn_kernel(x_ref, mod_ref, oa_ref, ob_ref, oc_ref, rc_ref, na_ref, nb_ref, nc_ref,
                      gb_ref, gc_ref, wo_ref, gain_ref, w13_ref, w2_ref, fin_ref, o_ref, acc_ref,
                      *, lam_init, final):
    a = _rms(oa_ref[...], na_ref[...])
    b = _group_rms(ob_ref[...], gb_ref[...], DV_B, nb_ref[...]) * (1.0 - lam_init)
    c = _group_rms(oc_ref[...], gc_ref[...], DV_C, nc_ref[...]) * _silu(rc_ref[...])
    mixed = (_dot(a.astype(BF16), wo_ref[0:WIDTH_A, :])
             + _dot(b.astype(BF16), wo_ref[WIDTH_A:WIDTH_A + WIDTH_B, :])
             + _dot(c.astype(BF16), wo_ref[WIDTH_A + WIDTH_B:, :]))
    x = x_ref[...] + mod_ref[0, 5:6, :] * mixed
    y = _ffn_block(x, mod_ref, gain_ref[...], w13_ref, w2_ref, acc_ref, 6)
    if final:
        y = _rms(y, fin_ref[...])
    o_ref[...] = y


def _merge_ffn(x, mod, tokens_per_group, oa, ob, oc, rc, lw, consts, fin, layer, lam_init, final):
    n = x.shape[0]
    row = lambda w: pl.BlockSpec((TOKEN_TILE, w), lambda i: (i, 0))
    return pl.pallas_call(
        functools.partial(_merge_ffn_kernel, lam_init=lam_init, final=final),
        grid=(n // TOKEN_TILE,),
        in_specs=[row(D_MODEL), _mod_spec(tokens_per_group), row(WIDTH_A), row(WIDTH_B), row(WIDTH_C),
                  row(WIDTH_C), _layer_spec((1, WIDTH_A), layer), _layer_spec((1, WIDTH_B), layer),
                  _layer_spec((1, WIDTH_C), layer), _const_spec((WIDTH_B, WIDTH_B)),
                  _const_spec((WIDTH_C, WIDTH_C)), _layer_spec((D_MODEL, D_MODEL), layer, 1),
                  _layer_spec((1, D_MODEL), layer), _layer_spec((D_MODEL, 2 * D_FF), layer, 1),
                  _layer_spec((D_FF, D_MODEL), layer, 1), _const_spec((1, D_MODEL))],
        out_specs=row(D_MODEL),
        out_shape=jax.ShapeDtypeStruct((n, D_MODEL), F32),
        scratch_shapes=[pltpu.VMEM((TOKEN_TILE, D_MODEL), F32)],
        compiler_params=_params(1),
    )(x, mod, oa, ob, oc, rc, lw["out_norm_a"], lw["out_norm_b"], lw["out_norm_c"],
      consts["group_b"], consts["group_c"], lw["wo"], lw["norm_ffn2"], lw["ffn2_w13"], lw["ffn2_w2"], fin)


def _prepare_weights(w):
    n_layers = w["w_in"].shape[0]
    row = lambda v: v.reshape(n_layers, 1, -1)

    def pad_heads(m, width):
        m = m.reshape(n_layers, m.shape[1], H_A, width)
        m = jnp.pad(m, ((0, 0), (0, 0), (0, 0), (0, HEAD_PAD - width)))
        return m.reshape(n_layers, m.shape[1], H_A * HEAD_PAD)

    def gate_block(wg, first):
        lo = 0 if first else QK_C_PAD
        return jnp.pad(wg, ((0, 0), (0, 0), (lo, 2 * QK_C_PAD - lo - H_C * DK_C)))

    wg = jnp.concatenate([gate_block(w["gla_wg_f"], True), gate_block(w["gla_wg_b"], False),
                          jnp.zeros((n_layers, LANE - 2 * GATE_RANK, 2 * QK_C_PAD), F32)], axis=1)
    bg = gate_block(w["gla_bg_f"][:, None, :], True) + gate_block(w["gla_bg_b"][:, None, :], False)
    return {
        "norm_ffn1": row(w["norm_ffn1"]), "ffn1_w13": w["ffn1_w13"].astype(BF16),
        "ffn1_w2": w["ffn1_w2"].astype(BF16),
        "norm_ffn2": row(w["norm_ffn2"]), "ffn2_w13": w["ffn2_w13"].astype(BF16),
        "ffn2_w2": w["ffn2_w2"].astype(BF16),
        "norm_mix": row(w["norm_mix"]), "w_in_t": jnp.swapaxes(w["w_in"], 1, 2),
        "q_norm": row(w["mla_q_norm"]),
        "wuq": pad_heads(w["mla_w_uq"], DN_A + DR_A).astype(BF16),
        "kv_norm": row(w["mla_kv_norm"]),
        "wuk": pad_heads(w["mla_w_uk"], DN_A).astype(BF16),
        "wuvt": jnp.swapaxes(w["mla_w_uv"], 1, 2).astype(BF16),
        "wg": wg.astype(BF16), "bg": bg,
        "lam": jnp.stack([w["diff_lq1"], w["diff_lk1"], w["diff_lq2"], w["diff_lk2"]], axis=1),
        "out_norm_a": row(w["mla_out_norm"]),
        "out_norm_b": row(jnp.tile(w["diff_norm"], (1, H_B))),
        "out_norm_c": row(jnp.tile(w["gla_norm"], (1, H_C))),
        "wo": w["w_out"].astype(BF16),
    }


def _rope_tables(n_pos):
    pos = jnp.arange(n_pos)
    rows = (pos // GRID_W).astype(F32)
    cols = (pos % GRID_W).astype(F32)
    half = DH_B // 2
    inv = ROPE_BASE ** (-jnp.arange(0, half, 2, dtype=F32) / half)
    ang = jnp.concatenate([rows[:, None] * inv, rows[:, None] * inv,
                           cols[:, None] * inv, cols[:, None] * inv], axis=1)
    sign = jnp.tile(jnp.concatenate([-jnp.ones(8, F32), jnp.ones(8, F32)]), 2)
    cos32, sin32 = jnp.cos(ang), jnp.sin(ang) * sign
    cos_full, sin_full = jnp.tile(cos32, (1, LANE // 32)), jnp.tile(sin32, (1, LANE // 32))
    ones = lambda n: jnp.ones((n_pos, n), F32)
    zeros = lambda n: jnp.zeros((n_pos, n), F32)
    cos_head = jnp.concatenate([ones(ROPE_OFF), cos32, ones(LANE - ROPE_OFF - DR_A)], axis=1)
    sin_head = jnp.concatenate([zeros(ROPE_OFF), sin32, zeros(LANE - ROPE_OFF - DR_A)], axis=1)
    return cos_head, sin_head, cos_full, sin_full


def _same_group_matrix(width, group):
    idx = jnp.arange(width) // group
    return jnp.where(idx[:, None] == idx[None, :], 1.0, 0.0).astype(BF16)


def _state_to_blockdiag_t(s):
    st = jnp.swapaxes(s, 2, 3)
    blocks = [jnp.pad(st[:, h], ((0, 0), (0, 0), (h * DK_C, QK_C_PAD - (h + 1) * DK_C))) for h in range(H_C)]
    return jnp.concatenate(blocks, axis=1)


def _blockdiag_t_to_state(st):
    return jnp.stack([jnp.swapaxes(st[:, h * DV_C:(h + 1) * DV_C, h * DK_C:(h + 1) * DK_C], 1, 2)
                      for h in range(H_C)], axis=1)


def kernel(x_prompt, x_sample, cache_mla_ckv, cache_mla_krope, cache_diff_k, cache_diff_v, state_gla_fwd, state_gla_bwd, c, c_ctx, w_mod, b_mod, norm_ffn1, ffn1_w13, ffn1_w2, norm_mix, w_in, mla_q_norm, mla_w_uq, mla_kv_norm, mla_w_uk, mla_w_uv, mla_out_norm, diff_lq1, diff_lk1, diff_lq2, diff_lk2, diff_norm, gla_wg_f, gla_bg_f, gla_wg_b, gla_bg_b, gla_norm, w_out, norm_ffn2, ffn2_w13, ffn2_w2, final_norm):
    w = dict(norm_ffn1=norm_ffn1, ffn1_w13=ffn1_w13, ffn1_w2=ffn1_w2, norm_mix=norm_mix, w_in=w_in,
             mla_q_norm=mla_q_norm, mla_w_uq=mla_w_uq, mla_kv_norm=mla_kv_norm, mla_w_uk=mla_w_uk,
             mla_w_uv=mla_w_uv, mla_out_norm=mla_out_norm, diff_lq1=diff_lq1, diff_lk1=diff_lk1,
             diff_lq2=diff_lq2, diff_lk2=diff_lk2, diff_norm=diff_norm, gla_wg_f=gla_wg_f,
             gla_bg_f=gla_bg_f, gla_wg_b=gla_wg_b, gla_bg_b=gla_bg_b, gla_norm=gla_norm, w_out=w_out,
             norm_ffn2=norm_ffn2, ffn2_w13=ffn2_w13, ffn2_w2=ffn2_w2)
    n_ctx_b, n_ctx_t, _ = x_prompt.shape
    n_lat_b, n_lat_t, _ = x_sample.shape
    n_past = cache_mla_ckv.shape[2]

    cvec = jnp.concatenate([c_ctx[None, :], c, jnp.zeros((16 - 1 - n_lat_b, D_MODEL), F32)], axis=0)
    mod = _modulation(cvec, w_mod, b_mod).reshape(DEPTH, 16, N_MOD, D_MODEL)

    rope_tabs = _rope_tables(n_lat_t)
    consts = {"group_b": _same_group_matrix(WIDTH_B, DV_B), "group_c": _same_group_matrix(WIDTH_C, DV_C)}
    cache_kr = jnp.pad(cache_mla_krope, ((0, 0), (0, 0), (0, 0), (ROPE_OFF, LANE - ROPE_OFF - DR_A)))
    cache_dk = cache_diff_k.reshape(n_lat_b, DEPTH, n_past, WIDTH_B)
    cache_dv = cache_diff_v.reshape(n_lat_b, DEPTH, n_past, WIDTH_B)
    fin = final_norm.reshape(1, D_MODEL)

    xp = x_prompt.reshape(n_ctx_b * n_ctx_t, D_MODEL)
    xs = x_sample.reshape(n_lat_b * n_lat_t, D_MODEL)
    new = {k: [] for k in ("kr", "sf", "sb")}
    new_cache = tuple(jnp.zeros((n_ctx_b, DEPTH) + shape, F32)
                      for shape in ((n_ctx_t, KV_RANK), (WIDTH_B, n_ctx_t), (WIDTH_B, n_ctx_t)))
    lw = _prepare_weights(w)
    for l in range(DEPTH):
        lam_init = 0.8 - 0.6 * math.exp(-0.3 * l)
        last = l == DEPTH - 1
        mod_ctx, mod_lat = mod[l, 0:1], mod[l, 1:1 + n_lat_b]

        def trunk(x, m, tpg, n_batch, n_tok, rope, mla_cache, diff_cache, gla_init, emit_state):
            x = _ffn(x, m, tpg, lw["norm_ffn1"], lw["ffn1_w13"], lw["ffn1_w2"], l, mod_row=0)
            p = _project(x, m, tpg, lw, lw["w_in_t"], l, rope, seq_len=n_tok if emit_state else 0,
                         carried=new_cache if emit_state else None)
            oa = _mla(p, lw, n_batch, n_tok, mla_cache, l)
            ob = _diff(p, lw, n_batch, n_tok, diff_cache, l, lam_init)
            gla_out = _gla(p, n_batch, n_tok, gla_init, emit_state)
            x = _merge_ffn(x, m, tpg, oa, ob, gla_out[0], p["rc"], lw, consts, fin, l, lam_init, last)
            return x, p, gla_out

        xp, p, gla_out = trunk(xp, mod_ctx, n_ctx_b * n_ctx_t, n_ctx_b, n_ctx_t, None, None, None, None, True)
        new_cache = p["cache"]
        new["kr"].append(p["kr"][:, ROPE_OFF:ROPE_OFF + DR_A].reshape(n_ctx_b, n_ctx_t, DR_A))
        new["sf"].append(_blockdiag_t_to_state(gla_out[1]))
        new["sb"].append(_blockdiag_t_to_state(gla_out[2]))

        gla_init = (_state_to_blockdiag_t(state_gla_fwd[:, l]), _state_to_blockdiag_t(state_gla_bwd[:, l]))
        xs, _, _ = trunk(xs, mod_lat, n_lat_t, n_lat_b, n_lat_t, rope_tabs, (cache_mla_ckv, cache_kr),
                         (cache_dk, cache_dv), gla_init, False)

    stack = lambda name: jnp.stack(new[name], axis=1)
    new_ckv, new_dkt, new_dvt = new_cache

    def token_major(t, dh):
        return jnp.transpose(t.reshape(n_ctx_b, DEPTH, H_B, dh, n_ctx_t), (0, 1, 4, 2, 3))

    return (xp.reshape(x_prompt.shape), xs.reshape(x_sample.shape), new_ckv, stack("kr"),
            token_major(new_dkt, 2 * DH_B), token_major(new_dvt, DV_B), stack("sf"), stack("sb"))
```

```python
import functools
import math

import jax
import jax.numpy as jnp
from jax import lax
from jax.experimental import pallas as pl
from jax.experimental.pallas import tpu as pltpu

F32 = jnp.float32
BF16 = jnp.bfloat16

D_MODEL = 1024
DEPTH = 2
GRID_W = 64
ROPE_BASE = 10000.0
EPS = 1e-6
H_A, DN_A, DR_A, DV_A = 6, 64, 32, 64
Q_RANK, KV_RANK = 384, 256
H_B, DH_B, DV_B = 4, 32, 64
H_C, DK_C, DV_C = 4, 48, 96
GATE_RANK = 16
GATE_NORM = 16.0
CHUNK = 64
WIDTH_A = H_A * DV_A
WIDTH_B = H_B * DV_B
WIDTH_C = H_C * DV_C
D_FF = 2816
N_MOD = 9

LANE = 128
FF_CHUNK = 256
TOKEN_TILE = 512
Q_TILE = 1024
KEY_BLOCK = 256
ONES_ROWS = 16
MOD_TILE = 1152
GLA_UNROLL = 4
ATTN_SEQS = 4
GLA_SEQS = 2
VMEM_LIMIT = 56 * 1024 * 1024

HEAD_PAD = LANE
ROPE_OFF = DN_A
QK_C_PAD = 256
SEG = {}
_off = 0
for _name, _w in (("cq", Q_RANK), ("krope", LANE), ("ckv", KV_RANK), ("qb", 256), ("kb", 256),
                  ("vb", 256), ("qc", QK_C_PAD), ("kc", QK_C_PAD), ("vc", WIDTH_C), ("gl", LANE),
                  ("rc", WIDTH_C)):
    SEG[_name] = (_off, _off + _w)
    _off += _w
PROJ_PAD = _off

PROJ_SPLITS = (Q_RANK, KV_RANK, DR_A, 2 * H_B * DH_B, 2 * H_B * DH_B, H_B * DV_B, H_C * DK_C, H_C * DK_C,
               WIDTH_C, WIDTH_C, GATE_RANK, GATE_RANK)
PROJ_COLS = sum(PROJ_SPLITS)
_SRC = [sum(PROJ_SPLITS[:i]) for i in range(len(PROJ_SPLITS) + 1)]


def _src(i, j=None):
    return ("src", _SRC[i], _SRC[(i if j is None else j) + 1])


PROJ_LAYOUT = (
    ("cq", (_src(0),)), ("ckv", (_src(1),)),
    ("krope", (("zero", 0, ROPE_OFF), _src(2), ("zero", 0, LANE - ROPE_OFF - DR_A))),
    ("qb", (_src(3),)), ("kb", (_src(4),)), ("vb", (_src(5),)),
    ("qc", (_src(6), ("zero", 0, QK_C_PAD - H_C * DK_C))),
    ("kc", (_src(7), ("zero", 0, QK_C_PAD - H_C * DK_C))),
    ("vc", (_src(8),)), ("rc", (_src(9),)),
    ("gl", (_src(10, 11), ("zero", 0, LANE - 2 * GATE_RANK))),
)


def _dot(a, b):
    return jnp.dot(a, b, preferred_element_type=F32)


def _dot_nt(a, b):
    return lax.dot_general(a, b, (((1,), (1,)), ((), ())), preferred_element_type=F32)


def _dot_tn(a, b):
    return lax.dot_general(a, b, (((0,), (0,)), ((), ())), preferred_element_type=F32)


def _rms(x, gain):
    ms = jnp.mean(x * x, axis=-1, keepdims=True)
    return (x * lax.rsqrt(ms + EPS)) * gain


def _silu(x):
    return x * jax.nn.sigmoid(x)


def _const_spec(shape):
    nd = len(shape)
    return pl.BlockSpec(shape, lambda *_: (0,) * nd)


def _layer_spec(shape, layer, buffers=None):
    nd = len(shape)
    mode = {} if buffers is None else {"pipeline_mode": pl.Buffered(buffers)}
    return pl.BlockSpec((None,) + tuple(shape), lambda *_: (layer,) + (0,) * nd, **mode)


def _params(n_axes):
    return pltpu.CompilerParams(dimension_semantics=("arbitrary",) * n_axes,
                                vmem_limit_bytes=VMEM_LIMIT)


def _mod_kernel(c_ref, w_ref, b_ref, o_ref):
    s = _silu(c_ref[...]).astype(BF16)
    o_ref[0] = _dot(s, w_ref[0].astype(BF16)) + b_ref[0]


def _modulation(cvec, w_mod, b_mod):
    n_rows = cvec.shape[0]
    width = N_MOD * D_MODEL
    return pl.pallas_call(
        _mod_kernel,
        grid=(DEPTH, width // MOD_TILE),
        in_specs=[
            pl.BlockSpec((n_rows, D_MODEL), lambda l, j: (0, 0)),
            pl.BlockSpec((1, D_MODEL, MOD_TILE), lambda l, j: (l, 0, j)),
            pl.BlockSpec((1, 1, MOD_TILE), lambda l, j: (l, 0, j)),
        ],
        out_specs=pl.BlockSpec((1, n_rows, MOD_TILE), lambda l, j: (l, 0, j)),
        out_shape=jax.ShapeDtypeStruct((DEPTH, n_rows, width), F32),
        compiler_params=_params(2),
    )(cvec, w_mod, b_mod.reshape(DEPTH, 1, width))


def _mod_spec(tokens_per_group):
    return pl.BlockSpec((1, N_MOD, D_MODEL), lambda i: ((i * TOKEN_TILE) // tokens_per_group, 0, 0))


def _ada_norm(x, gain, mod_ref, first_row):
    shift = mod_ref[0, first_row:first_row + 1, :]
    scale = mod_ref[0, first_row + 1:first_row + 2, :]
    return _rms(x, gain) * (1.0 + scale) + shift


def _ffn_block(x, mod_ref, gain, w13_ref, w2_ref, acc_ref, mod_row):
    u = _ada_norm(x, gain, mod_ref, mod_row).astype(BF16)
    for c in range(D_FF // FF_CHUNK):
        lo, hi = c * FF_CHUNK, (c + 1) * FF_CHUNK
        a = _dot(u, w13_ref[:, lo:hi])
        b = _dot(u, w13_ref[:, D_FF + lo:D_FF + hi])
        t = _dot((_silu(a) * b).astype(BF16), w2_ref[lo:hi, :])
        if c == 0:
            acc_ref[...] = t
        else:
            acc_ref[...] += t
    gate = mod_ref[0, mod_row + 2:mod_row + 3, :]
    return x + (0.5 * gate) * acc_ref[...]


def _ffn_kernel(x_ref, mod_ref, gain_ref, w13_ref, w2_ref, o_ref, acc_ref, *, mod_row):
    o_ref[...] = _ffn_block(x_ref[...], mod_ref, gain_ref[...], w13_ref, w2_ref, acc_ref, mod_row)


def _ffn(x, mod, tokens_per_group, gain, w13, w2, layer, *, mod_row):
    n = x.shape[0]
    tile = pl.BlockSpec((TOKEN_TILE, D_MODEL), lambda i: (i, 0))
    return pl.pallas_call(
        functools.partial(_ffn_kernel, mod_row=mod_row),
        grid=(n // TOKEN_TILE,),
        in_specs=[tile, _mod_spec(tokens_per_group), _layer_spec((1, D_MODEL), layer),
                  _layer_spec((D_MODEL, 2 * D_FF), layer, 1), _layer_spec((D_FF, D_MODEL), layer, 1)],
        out_specs=tile,
        out_shape=jax.ShapeDtypeStruct((n, D_MODEL), F32),
        scratch_shapes=[pltpu.VMEM((TOKEN_TILE, D_MODEL), F32)],
        compiler_params=_params(1),
    )(x, mod, gain, w13, w2)


def _rope(x, cos, sin_signed):
    first = (lax.broadcasted_iota(jnp.int32, (x.shape[0], LANE), 1) % 16) < 8
    outs = []
    for j in range(x.shape[1] // LANE):
        xj = x[:, j * LANE:(j + 1) * LANE]
        partner = jnp.where(first, pltpu.roll(xj, LANE - 8, 1), pltpu.roll(xj, 8, 1))
        outs.append(xj * cos + partner * sin_signed)
    return outs[0] if len(outs) == 1 else jnp.concatenate(outs, axis=1)


def _log_sigmoid(z):
    return jnp.minimum(z, 0.0) - jnp.log1p(jnp.exp(-jnp.abs(z)))


def _proj_kernel(*refs, rope, n_carried, seq_len):
    (x_ref, mod_ref, gain_ref, wint_ref, qn_ref, wuq_ref, kvn_ref, wg_ref, bg_ref) = refs[:9]
    refs = refs[9:]
    if rope:
        ch_ref, sh_ref, cf_ref, sf_ref = refs[:4]
        refs = refs[4:]
    refs = refs[n_carried:]
    (q_o, ckv_o, kr_o, qb_o, kb_o, vb_o, qc_o, kc_o, vc_o, rc_o, gf_o, gb_o) = refs[:12]
    refs = refs[12:]
    if seq_len:
        ckv_all_o, kbt_all_o, vbt_all_o = refs[:3]
        refs = refs[3:]
    wpt_ref, = refs

    @pl.when(pl.program_id(0) == 0)
    def _():
        for name, pieces in PROJ_LAYOUT:
            parts = []
            for kind, a, b in pieces:
                parts.append(jnp.zeros((b - a, D_MODEL), F32) if kind == "zero" else wint_ref[a:b, :])
            blk = parts[0] if len(parts) == 1 else jnp.concatenate(parts, axis=0)
            lo, hi = SEG[name]
            wpt_ref[lo:hi, :] = blk.astype(BF16)

    u = _ada_norm(x_ref[...], gain_ref[...], mod_ref, 3).astype(BF16)

    def seg(first, last=None):
        lo, hi = SEG[first][0], SEG[last or first][1]
        return _dot_nt(u, wpt_ref[lo:hi, :])

    cq_kr = seg("cq", "krope")
    q = _dot(_rms(cq_kr[:, :Q_RANK], qn_ref[...]).astype(BF16), wuq_ref[...])
    kr = cq_kr[:, Q_RANK:]
    qb = seg("qb")
    kb = seg("kb")
    if rope:
        q = _rope(q, ch_ref[...], sh_ref[...])
        kr = _rope(kr, ch_ref[...], sh_ref[...])
        qb = _rope(qb, cf_ref[...], sf_ref[...])
        kb = _rope(kb, cf_ref[...], sf_ref[...])
    q_o[...] = (q * (DN_A + DR_A) ** -0.5).astype(BF16)
    ckv = _rms(seg("ckv"), kvn_ref[...])
    ckv_o[...] = ckv
    if seq_len:
        kbt = _dot_nt(wpt_ref[SEG["kb"][0]:SEG["kb"][1], :], u)
        vbt = _dot_nt(wpt_ref[SEG["vb"][0]:SEG["vb"][1], :], u)
        for j in range(u.shape[0] // seq_len):
            tok = slice(j * seq_len, (j + 1) * seq_len)
            ckv_all_o[j] = ckv[tok, :]
            kbt_all_o[j] = kbt[:, tok]
            vbt_all_o[j] = vbt[:, tok]
    kr_o[...] = kr
    qb_o[...] = (qb * DH_B ** -0.5).astype(BF16)
    kb_o[...] = kb
    vb_o[...] = seg("vb")
    qc_o[...] = seg("qc") * (DK_C ** -0.5)
    kc_o[...] = seg("kc")
    vc_gl = seg("vc", "gl")
    vc_o[...] = vc_gl[:, :WIDTH_C]
    rc_o[...] = seg("rc")
    z = _dot(vc_gl[:, WIDTH_C:].astype(BF16), wg_ref[...]) + bg_ref[...]
    g = _log_sigmoid(z) / GATE_NORM
    gf_o[...] = g[:, :QK_C_PAD]
    gb_o[...] = g[:, QK_C_PAD:]


PROJ_OUT = (("q", H_A * HEAD_PAD, BF16), ("ckv", KV_RANK, F32), ("kr", LANE, F32), ("qb", 256, BF16),
            ("kb", 256, F32), ("vb", 256, F32), ("qc", QK_C_PAD, F32), ("kc", QK_C_PAD, F32),
            ("vc", WIDTH_C, F32), ("rc", WIDTH_C, F32), ("gf", QK_C_PAD, F32), ("gb", QK_C_PAD, F32))


def _project(x, mod, tokens_per_group, lw, w_in, layer, rope_tabs, seq_len=0, carried=None):
    n = x.shape[0]
    rope = rope_tabs is not None
    row = lambda w: pl.BlockSpec((TOKEN_TILE, w), lambda i: (i, 0))
    in_specs = [row(D_MODEL), _mod_spec(tokens_per_group), _layer_spec((1, D_MODEL), layer),
                _layer_spec((PROJ_COLS, D_MODEL), layer, 1), _layer_spec((1, Q_RANK), layer),
                _layer_spec((Q_RANK, H_A * HEAD_PAD), layer), _layer_spec((1, KV_RANK), layer),
                _layer_spec((LANE, 2 * QK_C_PAD), layer), _layer_spec((1, 2 * QK_C_PAD), layer)]
    args = [x, mod, lw["norm_mix"], w_in, lw["q_norm"], lw["wuq"], lw["kv_norm"], lw["wg"], lw["bg"]]
    if rope:
        n_pos = rope_tabs[0].shape[0]
        tab = pl.BlockSpec((TOKEN_TILE, LANE), lambda i: (i % (n_pos // TOKEN_TILE), 0))
        in_specs += [tab] * 4
        args += list(rope_tabs)
    out_specs = [row(w) for _, w, _ in PROJ_OUT]
    out_shape = [jax.ShapeDtypeStruct((n, w), dt) for _, w, dt in PROJ_OUT]
    aliases = {}
    if seq_len:
        per_tile = TOKEN_TILE // seq_len
        if carried is not None:
            for k in range(len(carried)):
                aliases[len(args) + k] = len(out_specs) + k
            in_specs += [pl.BlockSpec(memory_space=pl.ANY)] * len(carried)
            args += list(carried)
        for shape in ((seq_len, KV_RANK), (WIDTH_B, seq_len), (WIDTH_B, seq_len)):
            out_specs.append(pl.BlockSpec((per_tile, None) + shape, lambda i: (i, layer, 0, 0)))
            out_shape.append(jax.ShapeDtypeStruct((n // seq_len, DEPTH) + shape, F32))
    outs = pl.pallas_call(
        functools.partial(_proj_kernel, rope=rope, n_carried=0 if carried is None else len(carried),
                          seq_len=seq_len),
        grid=(n // TOKEN_TILE,),
        in_specs=in_specs,
        out_specs=out_specs,
        out_shape=out_shape,
        input_output_aliases=aliases,
        scratch_shapes=[pltpu.VMEM((PROJ_PAD, D_MODEL), BF16)],
        compiler_params=_params(1),
    )(*args)
    result = {name: o for (name, _, _), o in zip(PROJ_OUT, outs)}
    if seq_len:
        result["cache"] = tuple(outs[len(PROJ_OUT):])
    return result


def _lane_mask(width, lo, hi):
    lane = lax.broadcasted_iota(jnp.int32, (1, width), 1)
    return jnp.where((lane >= lo) & (lane < hi), 1.0, 0.0).astype(F32)


def _attend_t(units, dv):
    outs = []
    q, keys, vt_ext = units[0]()
    s = _dot_nt(q, keys)
    for u in range(len(units)):
        s_now, vt_now = s, vt_ext
        if u + 1 < len(units):
            q, keys, vt_ext = units[u + 1]()
            s = _dot_nt(q, keys)
        e = jnp.exp(s_now - jnp.max(s_now, axis=-1, keepdims=True)).astype(BF16)
        r = _dot_nt(vt_now, e)
        outs.append(r[0:dv] * (1.0 / r[dv:dv + 1]))
    return outs


def _attn_seqs(n_cache, n_q_tiles):
    return ATTN_SEQS if (n_cache == 0 and n_q_tiles == 1) else 1


def _mla_kernel(*refs, n_cache, n_seq, n_tok):
    if n_cache:
        q_ref, ckv_ref, kr_ref, cckv_ref, ckr_ref, wuk_ref, wuvt_ref, o_ref, kf_scr, vt_scr = refs
    else:
        q_ref, ckv_ref, kr_ref, wuk_ref, wuvt_ref, o_ref, kf_scr, vt_scr = refs

    @pl.when(pl.program_id(1) == 0)
    def _():
        def fill(s, row0, ckv, kr):
            rows = ckv.shape[0]
            cb = ckv.astype(BF16)
            kr_all = jnp.concatenate([kr] * H_A, axis=1)
            kf_scr[s, row0:row0 + rows, :] = (_dot(cb, wuk_ref[...]) + kr_all).astype(BF16)
            vt = _dot_nt(wuvt_ref[...], cb)
            for h in range(H_A):
                vt_scr[s * H_A + h, 0:DV_A, row0:row0 + rows] = vt[h * DV_A:(h + 1) * DV_A].astype(BF16)

        for s in range(n_seq):
            if n_cache:
                fill(s, 0, cckv_ref[...], ckr_ref[...])
            tok = slice(s * n_tok, (s + 1) * n_tok)
            fill(s, n_cache, ckv_ref[tok, :], kr_ref[tok, :])
        for u in range(n_seq * H_A):
            vt_scr[u, DV_A:, :] = jnp.ones((ONES_ROWS, vt_scr.shape[2]), BF16)

    n_q = q_ref.shape[0] // n_seq

    def unit(s, h):
        sl = slice(h * HEAD_PAD, (h + 1) * HEAD_PAD)
        return lambda: (q_ref[s * n_q:(s + 1) * n_q, sl], kf_scr[s, :, sl], vt_scr[s * H_A + h])

    heads = _attend_t([unit(s, h) for h in range(H_A) for s in range(n_seq)], DV_A)
    for s in range(n_seq):
        o_ref[s * n_q:(s + 1) * n_q, :] = jnp.concatenate(heads[s::n_seq], axis=0).T


def _mla(p, lw, n_batch, n_tok, cache, layer):
    n_cache = 0 if cache is None else cache[0].shape[2]
    q_tile = min(Q_TILE, n_tok)
    nqt = n_tok // q_tile
    n_seq = _attn_seqs(n_cache, nqt)
    in_specs = [pl.BlockSpec((n_seq * q_tile, H_A * HEAD_PAD), lambda b, i: (b * nqt + i, 0)),
                pl.BlockSpec((n_seq * n_tok, KV_RANK), lambda b, i: (b, 0)),
                pl.BlockSpec((n_seq * n_tok, LANE), lambda b, i: (b, 0))]
    args = [p["q"], p["ckv"], p["kr"]]
    if n_cache:
        in_specs += [pl.BlockSpec((None, None, n_cache, KV_RANK), lambda b, i: (b, layer, 0, 0)),
                     pl.BlockSpec((None, None, n_cache, LANE), lambda b, i: (b, layer, 0, 0))]
        args += list(cache)
    in_specs += [_layer_spec((KV_RANK, H_A * HEAD_PAD), layer), _layer_spec((WIDTH_A, KV_RANK), layer)]
    args += [lw["wuk"], lw["wuvt"]]
    n_keys = n_cache + n_tok
    return pl.pallas_call(
        functools.partial(_mla_kernel, n_cache=n_cache, n_seq=n_seq, n_tok=n_tok),
        grid=(n_batch // n_seq, nqt),
        in_specs=in_specs,
        out_specs=pl.BlockSpec((n_seq * q_tile, WIDTH_A), lambda b, i: (b * nqt + i, 0)),
        out_shape=jax.ShapeDtypeStruct((n_batch * n_tok, WIDTH_A), F32),
        scratch_shapes=[pltpu.VMEM((n_seq, n_keys, H_A * HEAD_PAD), BF16),
                        pltpu.VMEM((n_seq * H_A, DV_A + ONES_ROWS, n_keys), BF16)],
        compiler_params=_params(2),
    )(*args)


def _diff_kernel(*refs, n_cache, n_seq, n_tok, lam_init):
    if n_cache:
        q_ref, k_ref, v_ref, ck_ref, cv_ref, lam_ref, o_ref, k_scr, vt_scr = refs
    else:
        q_ref, k_ref, v_ref, lam_ref, o_ref, k_scr, vt_scr = refs

    @pl.when(pl.program_id(1) == 0)
    def _():
        def fill(s, row0, k, v):
            rows = k.shape[0]
            k_scr[s, row0:row0 + rows, :] = k.astype(BF16)
            vt = v.T
            for h in range(H_B):
                vt_scr[s * H_B + h, 0:DV_B, row0:row0 + rows] = vt[h * DV_B:(h + 1) * DV_B].astype(BF16)

        for s in range(n_seq):
            if n_cache:
                fill(s, 0, ck_ref[...], cv_ref[...])
            tok = slice(s * n_tok, (s + 1) * n_tok)
            fill(s, n_cache, k_ref[tok, :], v_ref[tok, :])
        for u in range(n_seq * H_B):
            vt_scr[u, DV_B:, :] = jnp.ones((ONES_ROWS, vt_scr.shape[2]), BF16)

    lv = lam_ref[...]
    lam = (jnp.exp(jnp.sum(lv[0:1] * lv[1:2], axis=-1, keepdims=True))
           - jnp.exp(jnp.sum(lv[2:3] * lv[3:4], axis=-1, keepdims=True)) + lam_init)
    n_q = q_ref.shape[0] // n_seq
    lane = lax.broadcasted_iota(jnp.int32, (n_q, LANE), 1)

    def unit(s, h, j):
        blk = slice((h // 2) * LANE, (h // 2 + 1) * LANE)
        lo = (h % 2) * 2 * DH_B + j * DH_B

        def fn():
            qh = q_ref[s * n_q:(s + 1) * n_q, blk]
            qm = jnp.where((lane >= lo) & (lane < lo + DH_B), qh, jnp.zeros_like(qh))
            return qm, k_scr[s, :, blk], vt_scr[s * H_B + h]
        return fn

    maps = _attend_t([unit(s, h, j) for h in range(H_B) for j in range(2) for s in range(n_seq)], DV_B)
    for s in range(n_seq):
        mine = maps[s::n_seq]
        heads = [mine[2 * h] - lam * mine[2 * h + 1] for h in range(H_B)]
        o_ref[s * n_q:(s + 1) * n_q, :] = jnp.concatenate(heads, axis=0).T


def _diff(p, lw, n_batch, n_tok, cache, layer, lam_init):
    n_cache = 0 if cache is None else cache[0].shape[2]
    q_tile = min(Q_TILE, n_tok)
    nqt = n_tok // q_tile
    n_seq = _attn_seqs(n_cache, nqt)
    in_specs = [pl.BlockSpec((n_seq * q_tile, WIDTH_B), lambda b, i: (b * nqt + i, 0)),
                pl.BlockSpec((n_seq * n_tok, WIDTH_B), lambda b, i: (b, 0)),
                pl.BlockSpec((n_seq * n_tok, WIDTH_B), lambda b, i: (b, 0))]
    args = [p["qb"], p["kb"], p["vb"]]
    if n_cache:
        in_specs += [pl.BlockSpec((None, None, n_cache, WIDTH_B), lambda b, i: (b, layer, 0, 0))] * 2
        args += list(cache)
    in_specs.append(_layer_spec((4, DH_B), layer))
    args.append(lw["lam"])
    n_keys = n_cache + n_tok
    return pl.pallas_call(
        functools.partial(_diff_kernel, n_cache=n_cache, n_seq=n_seq, n_tok=n_tok, lam_init=lam_init),
        grid=(n_batch // n_seq, nqt),
        in_specs=in_specs,
        out_specs=pl.BlockSpec((n_seq * q_tile, WIDTH_B), lambda b, i: (b * nqt + i, 0)),
        out_shape=jax.ShapeDtypeStruct((n_batch * n_tok, WIDTH_B), F32),
        scratch_shapes=[pltpu.VMEM((n_seq, n_keys, WIDTH_B), BF16),
                        pltpu.VMEM((n_seq * H_B, DV_B + ONES_ROWS, n_keys), BF16)],
        compiler_params=_params(2),
    )(*args)


def _split3(x):
    hi = x.astype(BF16)
    r1 = x - hi.astype(F32)
    mid = r1.astype(BF16)
    lo = (r1 - mid.astype(F32)).astype(BF16)
    return hi, mid, lo


def _dot_exact_lhs(a_bf16, x):
    hi, mid, lo = _split3(x)
    return _dot(a_bf16, hi) + _dot(a_bf16, mid) + _dot(a_bf16, lo)


def _gla_kernel(*refs, n_tok, n_seq, has_init, emit_state, n_carried):
    q_ref, k_ref, v_ref, gf_ref, gb_ref = refs[:5]
    refs = refs[5:]
    if has_init:
        s0f_ref, s0b_ref = refs[:2]
        refs = refs[2:]
    refs = refs[n_carried:]
    o_ref = refs[0]
    refs = refs[1:]
    if emit_state:
        sf_ref, sb_ref = refs[:2]
        refs = refs[2:]
    st_scr, bd_scr = refs


    n_chunks = n_tok // CHUNK
    n_groups = n_chunks // GLA_UNROLL
    t_idx = lax.broadcasted_iota(jnp.int32, (CHUNK, CHUNK), 0)
    s_idx = lax.broadcasted_iota(jnp.int32, (CHUNK, CHUNK), 1)
    tri = [jnp.where(s_idx <= t_idx, 1.0, 0.0).astype(F32), jnp.where(s_idx >= t_idx, 1.0, 0.0).astype(F32)]
    tri4 = [jnp.concatenate([t] * H_C, axis=1) for t in tri]
    tri_b = [t.astype(BF16) for t in tri]
    head_k = [_lane_mask(QK_C_PAD, h * DK_C, (h + 1) * DK_C) for h in range(H_C)]
    head_v = [_lane_mask(WIDTH_C, h * DV_C, (h + 1) * DV_C) for h in range(H_C)]
    row_v = lax.broadcasted_iota(jnp.int32, (WIDTH_C, QK_C_PAD), 0)
    col_k = lax.broadcasted_iota(jnp.int32, (WIDTH_C, QK_C_PAD), 1)
    diag = jnp.zeros((WIDTH_C, QK_C_PAD), F32)
    for h in range(H_C):
        inside = ((row_v >= h * DV_C) & (row_v < (h + 1) * DV_C)
                  & (col_k >= h * DK_C) & (col_k < (h + 1) * DK_C))
        diag = jnp.where(inside, 1.0, diag)

    for s in range(n_seq):
        for d, s0_ref in enumerate((s0f_ref, s0b_ref) if has_init else (None, None)):
            if s0_ref is None:
                st_scr[s, d] = jnp.zeros((WIDTH_C, QK_C_PAD), F32)
            else:
                bd_scr[...] = jnp.zeros_like(bd_scr)
                for h in range(H_C):
                    bd_scr[h * DK_C:(h + 1) * DK_C, h * DV_C:(h + 1) * DV_C] = s0_ref[s, h]
                st_scr[s, d] = bd_scr[...].T
    o_ref[...] = jnp.zeros_like(o_ref)

    def chunk_steps(chains):
        loaded = []
        for s, d, c in chains:
            g_ref = gb_ref if d else gf_ref
            row0 = s * n_tok + c * CHUNK
            rows = pl.ds(row0 if isinstance(row0, int) else pl.multiple_of(row0, CHUNK), CHUNK)
            bcum = _dot_exact_lhs(tri_b[d], g_ref[rows, :])
            loaded.append((rows, q_ref[rows, :], k_ref[rows, :], v_ref[rows, :], bcum))
        scaled = []
        for (s, d, c), (rows, q, k, v, bcum) in zip(chains, loaded):
            blast = bcum[0:1, :] if d else bcum[CHUNK - 1:CHUNK, :]
            qe = (q * jnp.exp(bcum)).astype(BF16)
            ke = k * jnp.exp(-bcum)
            kd = (k * jnp.exp(blast - bcum)).astype(BF16)
            ke_heads = jnp.concatenate([ke * head_k[h] for h in range(H_C)], axis=0).astype(BF16)
            v_heads = jnp.concatenate([v * head_v[h] for h in range(H_C)], axis=0).astype(BF16)
            scaled.append((qe, ke_heads, kd, v.astype(BF16), v_heads, jnp.exp(blast)))
        products = []
        for (s, d, c), (qe, ke_heads, kd, vb, v_heads, decay) in zip(chains, scaled):
            st = st_scr[s, d]
            attn = _dot_nt(qe, ke_heads)
            inter = _dot_nt(qe, st.astype(BF16))
            upd = _dot_tn(vb, kd)
            products.append((st, attn, inter, upd))
        masked = []
        for (s, d, c), (qe, ke_heads, kd, vb, v_heads, decay), (st, attn, inter, upd) in zip(
                chains, scaled, products):
            st_scr[s, d] = st * decay + upd * diag
            masked.append((attn * tri4[d]).astype(BF16))
        for (rows, *_), (_, _, _, _, v_heads, _), (_, _, inter, _), attn_b in zip(
                loaded, scaled, products, masked):
            o_ref[rows, :] += _dot(attn_b, v_heads) + inter

    def group(i):
        for j in range(GLA_UNROLL):
            fwd = i * GLA_UNROLL + j
            chunk_steps([(s, d, n_chunks - 1 - fwd if d else fwd) for s in range(n_seq) for d in (0, 1)])

    if n_groups == 1:
        group(0)
    else:
        def body(i, carry):
            group(i)
            return carry
        lax.fori_loop(0, n_groups, body, 0)

    if emit_state:
        for s in range(n_seq):
            for d, s_ref in enumerate((sf_ref, sb_ref)):
                bd_scr[...] = st_scr[s, d].T
                for h in range(H_C):
                    s_ref[s, h] = bd_scr[h * DK_C:(h + 1) * DK_C, h * DV_C:(h + 1) * DV_C]


def _gla(p, n_batch, n_tok, layer, init=None, carried=None):
    n_seq = GLA_SEQS
    tok = lambda w: pl.BlockSpec((n_seq * n_tok, w), lambda b: (b, 0))
    st_spec = pl.BlockSpec((n_seq, None, H_C, DK_C, DV_C), lambda b: (b, layer, 0, 0, 0))
    in_specs = [tok(QK_C_PAD), tok(QK_C_PAD), tok(WIDTH_C), tok(QK_C_PAD), tok(QK_C_PAD)]
    args = [p["qc"], p["kc"], p["vc"], p["gf"], p["gb"]]
    if init is not None:
        in_specs += [st_spec] * 2
        args += list(init)
    out_specs = [tok(WIDTH_C)]
    out_shape = [jax.ShapeDtypeStruct((n_batch * n_tok, WIDTH_C), F32)]
    aliases = {}
    if carried is not None:
        for k, arr in enumerate(carried):
            aliases[len(args) + k] = len(out_specs) + k
            out_shape.append(jax.ShapeDtypeStruct(arr.shape, F32))
        in_specs += [pl.BlockSpec(memory_space=pl.ANY)] * len(carried)
        args += list(carried)
        out_specs += [st_spec] * len(carried)
    return pl.pallas_call(
        functools.partial(_gla_kernel, n_tok=n_tok, n_seq=n_seq, has_init=init is not None,
                          emit_state=carried is not None, n_carried=0 if carried is None else len(carried)),
        grid=(n_batch // n_seq,),
        in_specs=in_specs,
        out_specs=out_specs,
        out_shape=out_shape,
        input_output_aliases=aliases,
        scratch_shapes=[pltpu.VMEM((n_seq, 2, WIDTH_C, QK_C_PAD), F32), pltpu.VMEM((QK_C_PAD, WIDTH_C), F32)],
        compiler_params=_params(1),
    )(*args)


def _group_rms(x, same_group_bf16, group, gain):
    sq = x * x
    hi = sq.astype(BF16)
    mid = (sq - hi.astype(F32)).astype(BF16)
    ms = (_dot(hi, same_group_bf16) + _dot(mid, same_group_bf16)) * (1.0 / group)
    return x * lax.rsqrt(ms + EPS) * gain


def _merge_ffn_kernel(x_ref, mod_ref, oa_ref, ob_ref, oc_ref, rc_ref, na_ref, nb_ref, nc_ref,
                      gb_ref, gc_ref, wo_ref, gain_ref, w13_ref, w2_ref, fin_ref, o_ref, acc_ref,
                      *, lam_init, final):
    a = _rms(oa_ref[...], na_ref[...])
    b = _group_rms(ob_ref[...], gb_ref[...], DV_B, nb_ref[...]) * (1.0 - lam_init)
    c = _group_rms(oc_ref[...], gc_ref[...], DV_C, nc_ref[...]) * _silu(rc_ref[...])
    mixed = (_dot(a.astype(BF16), wo_ref[0:WIDTH_A, :])
             + _dot(b.astype(BF16), wo_ref[WIDTH_A:WIDTH_A + WIDTH_B, :])
             + _dot(c.astype(BF16), wo_ref[WIDTH_A + WIDTH_B:, :]))
    x = x_ref[...] + mod_ref[0, 5:6, :] * mixed
    y = _ffn_block(x, mod_ref, gain_ref[...], w13_ref, w2_ref, acc_ref, 6)
    if final:
        y = _rms(y, fin_ref[...])
    o_ref[...] = y


def _merge_ffn(x, mod, tokens_per_group, oa, ob, oc, rc, lw, consts, fin, layer, lam_init, final):
    n = x.shape[0]
    row = lambda w: pl.BlockSpec((TOKEN_TILE, w), lambda i: (i, 0))
    return pl.pallas_call(
        functools.partial(_merge_ffn_kernel, lam_init=lam_init, final=final),
        grid=(n // TOKEN_TILE,),
        in_specs=[row(D_MODEL), _mod_spec(tokens_per_group), row(WIDTH_A), row(WIDTH_B), row(WIDTH_C),
                  row(WIDTH_C), _layer_spec((1, WIDTH_A), layer), _layer_spec((1, WIDTH_B), layer),
                  _layer_spec((1, WIDTH_C), layer), _const_spec((WIDTH_B, WIDTH_B)),
                  _const_spec((WIDTH_C, WIDTH_C)), _layer_spec((D_MODEL, D_MODEL), layer, 1),
                  _layer_spec((1, D_MODEL), layer), _layer_spec((D_MODEL, 2 * D_FF), layer, 1),
                  _layer_spec((D_FF, D_MODEL), layer, 1), _const_spec((1, D_MODEL))],
        out_specs=row(D_MODEL),
        out_shape=jax.ShapeDtypeStruct((n, D_MODEL), F32),
        scratch_shapes=[pltpu.VMEM((TOKEN_TILE, D_MODEL), F32)],
        compiler_params=_params(1),
    )(x, mod, oa, ob, oc, rc, lw["out_norm_a"], lw["out_norm_b"], lw["out_norm_c"],
      consts["group_b"], consts["group_c"], lw["wo"], lw["norm_ffn2"], lw["ffn2_w13"], lw["ffn2_w2"], fin)


def _prepare_weights(w):
    n_layers = w["w_in"].shape[0]
    row = lambda v: v.reshape(n_layers, 1, -1)

    def pad_heads(m, width):
        m = m.reshape(n_layers, m.shape[1], H_A, width)
        m = jnp.pad(m, ((0, 0), (0, 0), (0, 0), (0, HEAD_PAD - width)))
        return m.reshape(n_layers, m.shape[1], H_A * HEAD_PAD)

    def gate_block(wg, first):
        lo = 0 if first else QK_C_PAD
        return jnp.pad(wg, ((0, 0), (0, 0), (lo, 2 * QK_C_PAD - lo - H_C * DK_C)))

    wg = jnp.concatenate([gate_block(w["gla_wg_f"], True), gate_block(w["gla_wg_b"], False),
                          jnp.zeros((n_layers, LANE - 2 * GATE_RANK, 2 * QK_C_PAD), F32)], axis=1)
    bg = gate_block(w["gla_bg_f"][:, None, :], True) + gate_block(w["gla_bg_b"][:, None, :], False)
    return {
        "norm_ffn1": row(w["norm_ffn1"]), "ffn1_w13": w["ffn1_w13"].astype(BF16),
        "ffn1_w2": w["ffn1_w2"].astype(BF16),
        "norm_ffn2": row(w["norm_ffn2"]), "ffn2_w13": w["ffn2_w13"].astype(BF16),
        "ffn2_w2": w["ffn2_w2"].astype(BF16),
        "norm_mix": row(w["norm_mix"]), "w_in_t": jnp.swapaxes(w["w_in"], 1, 2),
        "q_norm": row(w["mla_q_norm"]),
        "wuq": pad_heads(w["mla_w_uq"], DN_A + DR_A).astype(BF16),
        "kv_norm": row(w["mla_kv_norm"]),
        "wuk": pad_heads(w["mla_w_uk"], DN_A).astype(BF16),
        "wuvt": jnp.swapaxes(w["mla_w_uv"], 1, 2).astype(BF16),
        "wg": wg.astype(BF16), "bg": bg,
        "lam": jnp.stack([w["diff_lq1"], w["diff_lk1"], w["diff_lq2"], w["diff_lk2"]], axis=1),
        "out_norm_a": row(w["mla_out_norm"]),
        "out_norm_b": row(jnp.tile(w["diff_norm"], (1, H_B))),
        "out_norm_c": row(jnp.tile(w["gla_norm"], (1, H_C))),
        "wo": w["w_out"].astype(BF16),
    }


def _rope_tables(n_pos):
    pos = jnp.arange(n_pos)
    rows = (pos // GRID_W).astype(F32)
    cols = (pos % GRID_W).astype(F32)
    half = DH_B // 2
    inv = ROPE_BASE ** (-jnp.arange(0, half, 2, dtype=F32) / half)
    ang = jnp.concatenate([rows[:, None] * inv, rows[:, None] * inv,
                           cols[:, None] * inv, cols[:, None] * inv], axis=1)
    sign = jnp.tile(jnp.concatenate([-jnp.ones(8, F32), jnp.ones(8, F32)]), 2)
    cos32, sin32 = jnp.cos(ang), jnp.sin(ang) * sign
    cos_full, sin_full = jnp.tile(cos32, (1, LANE // 32)), jnp.tile(sin32, (1, LANE // 32))
    ones = lambda n: jnp.ones((n_pos, n), F32)
    zeros = lambda n: jnp.zeros((n_pos, n), F32)
    cos_head = jnp.concatenate([ones(ROPE_OFF), cos32, ones(LANE - ROPE_OFF - DR_A)], axis=1)
    sin_head = jnp.concatenate([zeros(ROPE_OFF), sin32, zeros(LANE - ROPE_OFF - DR_A)], axis=1)
    return cos_head, sin_head, cos_full, sin_full


def _same_group_matrix(width, group):
    idx = jnp.arange(width) // group
    return jnp.where(idx[:, None] == idx[None, :], 1.0, 0.0).astype(BF16)


def kernel(x_prompt, x_sample, cache_mla_ckv, cache_mla_krope, cache_diff_k, cache_diff_v, state_gla_fwd, state_gla_bwd, c, c_ctx, w_mod, b_mod, norm_ffn1, ffn1_w13, ffn1_w2, norm_mix, w_in, mla_q_norm, mla_w_uq, mla_kv_norm, mla_w_uk, mla_w_uv, mla_out_norm, diff_lq1, diff_lk1, diff_lq2, diff_lk2, diff_norm, gla_wg_f, gla_bg_f, gla_wg_b, gla_bg_b, gla_norm, w_out, norm_ffn2, ffn2_w13, ffn2_w2, final_norm):
    w = dict(norm_ffn1=norm_ffn1, ffn1_w13=ffn1_w13, ffn1_w2=ffn1_w2, norm_mix=norm_mix, w_in=w_in,
             mla_q_norm=mla_q_norm, mla_w_uq=mla_w_uq, mla_kv_norm=mla_kv_norm, mla_w_uk=mla_w_uk,
             mla_w_uv=mla_w_uv, mla_out_norm=mla_out_norm, diff_lq1=diff_lq1, diff_lk1=diff_lk1,
             diff_lq2=diff_lq2, diff_lk2=diff_lk2, diff_norm=diff_norm, gla_wg_f=gla_wg_f,
             gla_bg_f=gla_bg_f, gla_wg_b=gla_wg_b, gla_bg_b=gla_bg_b, gla_norm=gla_norm, w_out=w_out,
             norm_ffn2=norm_ffn2, ffn2_w13=ffn2_w13, ffn2_w2=ffn2_w2)
    n_ctx_b, n_ctx_t, _ = x_prompt.shape
    n_lat_b, n_lat_t, _ = x_sample.shape
    n_past = cache_mla_ckv.shape[2]

    cvec = jnp.concatenate([c_ctx[None, :], c, jnp.zeros((16 - 1 - n_lat_b, D_MODEL), F32)], axis=0)
    mod = _modulation(cvec, w_mod, b_mod).reshape(DEPTH, 16, N_MOD, D_MODEL)

    rope_tabs = _rope_tables(n_lat_t)
    consts = {"group_b": _same_group_matrix(WIDTH_B, DV_B), "group_c": _same_group_matrix(WIDTH_C, DV_C)}
    cache_kr = jnp.pad(cache_mla_krope, ((0, 0), (0, 0), (0, 0), (ROPE_OFF, LANE - ROPE_OFF - DR_A)))
    cache_dk = cache_diff_k.reshape(n_lat_b, DEPTH, n_past, WIDTH_B)
    cache_dv = cache_diff_v.reshape(n_lat_b, DEPTH, n_past, WIDTH_B)
    fin = final_norm.reshape(1, D_MODEL)

    xp = x_prompt.reshape(n_ctx_b * n_ctx_t, D_MODEL)
    xs = x_sample.reshape(n_lat_b * n_lat_t, D_MODEL)
    new_kr = []
    new_cache = tuple(jnp.zeros((n_ctx_b, DEPTH) + shape, F32)
                      for shape in ((n_ctx_t, KV_RANK), (WIDTH_B, n_ctx_t), (WIDTH_B, n_ctx_t)))
    new_states = tuple(jnp.zeros((n_ctx_b, DEPTH, H_C, DK_C, DV_C), F32) for _ in range(2))
    lw = _prepare_weights(w)
    for l in range(DEPTH):
        lam_init = 0.8 - 0.6 * math.exp(-0.3 * l)
        last = l == DEPTH - 1
        mod_ctx, mod_lat = mod[l, 0:1], mod[l, 1:1 + n_lat_b]

        def trunk(x, m, tpg, n_batch, n_tok, rope, mla_cache, diff_cache, gla_init, is_ctx):
            x = _ffn(x, m, tpg, lw["norm_ffn1"], lw["ffn1_w13"], lw["ffn1_w2"], l, mod_row=0)
            p = _project(x, m, tpg, lw, lw["w_in_t"], l, rope, seq_len=n_tok if is_ctx else 0,
                         carried=new_cache if is_ctx else None)
            oa = _mla(p, lw, n_batch, n_tok, mla_cache, l)
            ob = _diff(p, lw, n_batch, n_tok, diff_cache, l, lam_init)
            gla_out = _gla(p, n_batch, n_tok, l, init=gla_init, carried=new_states if is_ctx else None)
            x = _merge_ffn(x, m, tpg, oa, ob, gla_out[0], p["rc"], lw, consts, fin, l, lam_init, last)
            return x, p, gla_out

        xp, p, gla_out = trunk(xp, mod_ctx, n_ctx_b * n_ctx_t, n_ctx_b, n_ctx_t, None, None, None, None, True)
        new_cache = p["cache"]
        new_states = tuple(gla_out[1:])
        new_kr.append(p["kr"][:, ROPE_OFF:ROPE_OFF + DR_A].reshape(n_ctx_b, n_ctx_t, DR_A))

        xs, _, _ = trunk(xs, mod_lat, n_lat_t, n_lat_b, n_lat_t, rope_tabs, (cache_mla_ckv, cache_kr),
                         (cache_dk, cache_dv), (state_gla_fwd, state_gla_bwd), False)

    new_ckv, new_dkt, new_dvt = new_cache

    def token_major(t, dh):
        return jnp.transpose(t.reshape(n_ctx_b, DEPTH, H_B, dh, n_ctx_t), (0, 1, 4, 2, 3))

    return (xp.reshape(x_prompt.shape), xs.reshape(x_sample.shape), new_ckv, jnp.stack(new_kr, axis=1),
            token_major(new_dkt, 2 * DH_B), token_major(new_dvt, DV_B), new_states[0], new_states[1])
```

```python
import functools
import math

import jax
import jax.numpy as jnp
from jax import lax
from jax.experimental import pallas as pl
from jax.experimental.pallas import tpu as pltpu

F32 = jnp.float32
BF16 = jnp.bfloat16

D_MODEL = 1024
DEPTH = 2
GRID_W = 64
ROPE_BASE = 10000.0
EPS = 1e-6
H_A, DN_A, DR_A, DV_A = 6, 64, 32, 64
Q_RANK, KV_RANK = 384, 256
H_B, DH_B, DV_B = 4, 32, 64
H_C, DK_C, DV_C = 4, 48, 96
GATE_RANK = 16
GATE_NORM = 16.0
CHUNK = 64
WIDTH_A = H_A * DV_A
WIDTH_B = H_B * DV_B
WIDTH_C = H_C * DV_C
D_FF = 2816
N_MOD = 9

LANE = 128
FF_CHUNK = 256
TOKEN_TILE = 512
Q_TILE = 1024
KEY_BLOCK = 256
ONES_ROWS = 16
MOD_TILE = 1152
GLA_UNROLL = 4
ATTN_SEQS = 4
GLA_SEQS = 2
GLA_ROWS = 1024
VMEM_LIMIT = 56 * 1024 * 1024

HEAD_PAD = LANE
ROPE_OFF = DN_A
QK_C_PAD = 256
SEG = {}
_off = 0
for _name, _w in (("cq", Q_RANK), ("krope", LANE), ("ckv", KV_RANK), ("qb", 256), ("kb", 256),
                  ("vb", 256), ("qc", QK_C_PAD), ("kc", QK_C_PAD), ("vc", WIDTH_C), ("gl", LANE),
                  ("rc", WIDTH_C)):
    SEG[_name] = (_off, _off + _w)
    _off += _w
PROJ_PAD = _off

PROJ_SPLITS = (Q_RANK, KV_RANK, DR_A, 2 * H_B * DH_B, 2 * H_B * DH_B, H_B * DV_B, H_C * DK_C, H_C * DK_C,
               WIDTH_C, WIDTH_C, GATE_RANK, GATE_RANK)
PROJ_COLS = sum(PROJ_SPLITS)
_SRC = [sum(PROJ_SPLITS[:i]) for i in range(len(PROJ_SPLITS) + 1)]


def _src(i, j=None):
    return ("src", _SRC[i], _SRC[(i if j is None else j) + 1])


PROJ_LAYOUT = (
    ("cq", (_src(0),)), ("ckv", (_src(1),)),
    ("krope", (("zero", 0, ROPE_OFF), _src(2), ("zero", 0, LANE - ROPE_OFF - DR_A))),
    ("qb", (_src(3),)), ("kb", (_src(4),)), ("vb", (_src(5),)),
    ("qc", (_src(6), ("zero", 0, QK_C_PAD - H_C * DK_C))),
    ("kc", (_src(7), ("zero", 0, QK_C_PAD - H_C * DK_C))),
    ("vc", (_src(8),)), ("rc", (_src(9),)),
    ("gl", (_src(10, 11), ("zero", 0, LANE - 2 * GATE_RANK))),
)


def _dot(a, b):
    return jnp.dot(a, b, preferred_element_type=F32)


def _dot_nt(a, b):
    return lax.dot_general(a, b, (((1,), (1,)), ((), ())), preferred_element_type=F32)


def _dot_tn(a, b):
    return lax.dot_general(a, b, (((0,), (0,)), ((), ())), preferred_element_type=F32)


def _rms(x, gain):
    ms = jnp.mean(x * x, axis=-1, keepdims=True)
    return (x * lax.rsqrt(ms + EPS)) * gain


def _silu(x):
    return x * jax.nn.sigmoid(x)


def _const_spec(shape):
    nd = len(shape)
    return pl.BlockSpec(shape, lambda *_: (0,) * nd)


def _layer_spec(shape, layer, buffers=None):
    nd = len(shape)
    mode = {} if buffers is None else {"pipeline_mode": pl.Buffered(buffers)}
    return pl.BlockSpec((None,) + tuple(shape), lambda *_: (layer,) + (0,) * nd, **mode)


def _params(n_axes):
    return pltpu.CompilerParams(dimension_semantics=("arbitrary",) * n_axes,
                                vmem_limit_bytes=VMEM_LIMIT)


def _mod_kernel(c_ref, w_ref, b_ref, o_ref):
    s = _silu(c_ref[...]).astype(BF16)
    o_ref[0] = _dot(s, w_ref[0].astype(BF16)) + b_ref[0]


def _modulation(cvec, w_mod, b_mod):
    n_rows = cvec.shape[0]
    width = N_MOD * D_MODEL
    return pl.pallas_call(
        _mod_kernel,
        grid=(DEPTH, width // MOD_TILE),
        in_specs=[
            pl.BlockSpec((n_rows, D_MODEL), lambda l, j: (0, 0)),
            pl.BlockSpec((1, D_MODEL, MOD_TILE), lambda l, j: (l, 0, j)),
            pl.BlockSpec((1, 1, MOD_TILE), lambda l, j: (l, 0, j)),
        ],
        out_specs=pl.BlockSpec((1, n_rows, MOD_TILE), lambda l, j: (l, 0, j)),
        out_shape=jax.ShapeDtypeStruct((DEPTH, n_rows, width), F32),
        compiler_params=_params(2),
    )(cvec, w_mod, b_mod.reshape(DEPTH, 1, width))


def _all_layer_spec(lead, shape, layer, first):
    zeros = (0,) * len(shape)
    if first:
        return pl.BlockSpec((lead, DEPTH) + tuple(shape), lambda i: (i, 0) + zeros)
    return pl.BlockSpec((lead, None) + tuple(shape), lambda i: (i, layer) + zeros)


def _put_layer(ref, lead, tail, layer, first, val):
    if not first:
        ref[lead + tail] = val
        return
    for l in range(DEPTH):
        ref[lead + (l,) + tail] = val if l == layer else jnp.zeros_like(val)


def _mod_spec(tokens_per_group):
    return pl.BlockSpec((1, N_MOD, D_MODEL), lambda i: ((i * TOKEN_TILE) // tokens_per_group, 0, 0))


def _ada_norm(x, gain, mod_ref, first_row):
    shift = mod_ref[0, first_row:first_row + 1, :]
    scale = mod_ref[0, first_row + 1:first_row + 2, :]
    return _rms(x, gain) * (1.0 + scale) + shift


def _ffn_block(x, mod_ref, gain, w13_ref, w2_ref, acc_ref, mod_row):
    u = _ada_norm(x, gain, mod_ref, mod_row).astype(BF16)
    for c in range(D_FF // FF_CHUNK):
        lo, hi = c * FF_CHUNK, (c + 1) * FF_CHUNK
        a = _dot(u, w13_ref[:, lo:hi])
        b = _dot(u, w13_ref[:, D_FF + lo:D_FF + hi])
        t = _dot((_silu(a) * b).astype(BF16), w2_ref[lo:hi, :])
        if c == 0:
            acc_ref[...] = t
        else:
            acc_ref[...] += t
    gate = mod_ref[0, mod_row + 2:mod_row + 3, :]
    return x + (0.5 * gate) * acc_ref[...]


def _ffn_kernel(x_ref, mod_ref, gain_ref, w13_ref, w2_ref, o_ref, acc_ref, *, mod_row):
    o_ref[...] = _ffn_block(x_ref[...], mod_ref, gain_ref[...], w13_ref, w2_ref, acc_ref, mod_row)


def _ffn(x, mod, tokens_per_group, gain, w13, w2, layer, *, mod_row):
    n = x.shape[0]
    tile = pl.BlockSpec((TOKEN_TILE, D_MODEL), lambda i: (i, 0))
    return pl.pallas_call(
        functools.partial(_ffn_kernel, mod_row=mod_row),
        grid=(n // TOKEN_TILE,),
        in_specs=[tile, _mod_spec(tokens_per_group), _layer_spec((1, D_MODEL), layer),
                  _layer_spec((D_MODEL, 2 * D_FF), layer, 1), _layer_spec((D_FF, D_MODEL), layer, 1)],
        out_specs=tile,
        out_shape=jax.ShapeDtypeStruct((n, D_MODEL), F32),
        scratch_shapes=[pltpu.VMEM((TOKEN_TILE, D_MODEL), F32)],
        compiler_params=_params(1),
    )(x, mod, gain, w13, w2)


def _rope(x, cos, sin_signed):
    first = (lax.broadcasted_iota(jnp.int32, (x.shape[0], LANE), 1) % 16) < 8
    outs = []
    for j in range(x.shape[1] // LANE):
        xj = x[:, j * LANE:(j + 1) * LANE]
        partner = jnp.where(first, pltpu.roll(xj, LANE - 8, 1), pltpu.roll(xj, 8, 1))
        outs.append(xj * cos + partner * sin_signed)
    return outs[0] if len(outs) == 1 else jnp.concatenate(outs, axis=1)


def _log_sigmoid(z):
    return jnp.minimum(z, 0.0) - jnp.log1p(jnp.exp(-jnp.abs(z)))


def _proj_kernel(*refs, rope, n_carried, seq_len, layer):
    (x_ref, mod_ref, gain_ref, wint_ref, qn_ref, wuq_ref, kvn_ref, wg_ref, bg_ref) = refs[:9]
    refs = refs[9:]
    if rope:
        ch_ref, sh_ref, cf_ref, sf_ref = refs[:4]
        refs = refs[4:]
    refs = refs[n_carried:]
    (q_o, ckv_o, kr_o, qb_o, kb_o, vb_o, qc_o, kc_o, vc_o, rc_o, gf_o, gb_o) = refs[:12]
    refs = refs[12:]
    if seq_len:
        ckv_all_o, kbt_all_o, vbt_all_o = refs[:3]
        refs = refs[3:]
    wpt_ref, = refs

    @pl.when(pl.program_id(0) == 0)
    def _():
        for name, pieces in PROJ_LAYOUT:
            parts = []
            for kind, a, b in pieces:
                parts.append(jnp.zeros((b - a, D_MODEL), F32) if kind == "zero" else wint_ref[a:b, :])
            blk = parts[0] if len(parts) == 1 else jnp.concatenate(parts, axis=0)
            lo, hi = SEG[name]
            wpt_ref[lo:hi, :] = blk.astype(BF16)

    u = _ada_norm(x_ref[...], gain_ref[...], mod_ref, 3).astype(BF16)

    def seg(first, last=None):
        lo, hi = SEG[first][0], SEG[last or first][1]
        return _dot_nt(u, wpt_ref[lo:hi, :])

    cq_kr = seg("cq", "krope")
    q = _dot(_rms(cq_kr[:, :Q_RANK], qn_ref[...]).astype(BF16), wuq_ref[...])
    kr = cq_kr[:, Q_RANK:]
    qb = seg("qb")
    kb = seg("kb")
    if rope:
        q = _rope(q, ch_ref[...], sh_ref[...])
        kr = _rope(kr, ch_ref[...], sh_ref[...])
        qb = _rope(qb, cf_ref[...], sf_ref[...])
        kb = _rope(kb, cf_ref[...], sf_ref[...])
    q_o[...] = (q * (DN_A + DR_A) ** -0.5).astype(BF16)
    ckv = _rms(seg("ckv"), kvn_ref[...])
    ckv_o[...] = ckv
    if seq_len:
        kbt = _dot_nt(wpt_ref[SEG["kb"][0]:SEG["kb"][1], :], u)
        vbt = _dot_nt(wpt_ref[SEG["vb"][0]:SEG["vb"][1], :], u)
        for j in range(u.shape[0] // seq_len):
            tok = slice(j * seq_len, (j + 1) * seq_len)
            _put_layer(ckv_all_o, (j,), (), layer, n_carried == 0, ckv[tok, :])
            _put_layer(kbt_all_o, (j,), (), layer, n_carried == 0, kbt[:, tok])
            _put_layer(vbt_all_o, (j,), (), layer, n_carried == 0, vbt[:, tok])
    kr_o[...] = kr
    qb_o[...] = (qb * DH_B ** -0.5).astype(BF16)
    kb_o[...] = kb
    vb_o[...] = seg("vb")
    qc_o[...] = seg("qc") * (DK_C ** -0.5)
    kc_o[...] = seg("kc")
    vc_gl = seg("vc", "gl")
    vc_o[...] = vc_gl[:, :WIDTH_C]
    rc_o[...] = seg("rc")
    z = _dot(vc_gl[:, WIDTH_C:].astype(BF16), wg_ref[...]) + bg_ref[...]
    g = _log_sigmoid(z) / GATE_NORM
    gf_o[...] = g[:, :QK_C_PAD]
    gb_o[...] = g[:, QK_C_PAD:]


PROJ_OUT = (("q", H_A * HEAD_PAD, BF16), ("ckv", KV_RANK, F32), ("kr", LANE, F32), ("qb", 256, BF16),
            ("kb", 256, F32), ("vb", 256, F32), ("qc", QK_C_PAD, F32), ("kc", QK_C_PAD, F32),
            ("vc", WIDTH_C, F32), ("rc", WIDTH_C, F32), ("gf", QK_C_PAD, F32), ("gb", QK_C_PAD, F32))


def _project(x, mod, tokens_per_group, lw, w_in, layer, rope_tabs, seq_len=0, carried=None):
    n = x.shape[0]
    rope = rope_tabs is not None
    row = lambda w: pl.BlockSpec((TOKEN_TILE, w), lambda i: (i, 0))
    in_specs = [row(D_MODEL), _mod_spec(tokens_per_group), _layer_spec((1, D_MODEL), layer),
                _layer_spec((PROJ_COLS, D_MODEL), layer, 1), _layer_spec((1, Q_RANK), layer),
                _layer_spec((Q_RANK, H_A * HEAD_PAD), layer), _layer_spec((1, KV_RANK), layer),
                _layer_spec((LANE, 2 * QK_C_PAD), layer), _layer_spec((1, 2 * QK_C_PAD), layer)]
    args = [x, mod, lw["norm_mix"], w_in, lw["q_norm"], lw["wuq"], lw["kv_norm"], lw["wg"], lw["bg"]]
    if rope:
        n_pos = rope_tabs[0].shape[0]
        tab = pl.BlockSpec((TOKEN_TILE, LANE), lambda i: (i % (n_pos // TOKEN_TILE), 0))
        in_specs += [tab] * 4
        args += list(rope_tabs)
    out_specs = [row(w) for _, w, _ in PROJ_OUT]
    out_shape = [jax.ShapeDtypeStruct((n, w), dt) for _, w, dt in PROJ_OUT]
    aliases = {}
    if seq_len:
        per_tile = TOKEN_TILE // seq_len
        if carried is not None:
            for k in range(len(carried)):
                aliases[len(args) + k] = len(out_specs) + k
            in_specs += [pl.BlockSpec(memory_space=pl.ANY)] * len(carried)
            args += list(carried)
        for shape in ((seq_len, KV_RANK), (WIDTH_B, seq_len), (WIDTH_B, seq_len)):
            out_specs.append(_all_layer_spec(per_tile, shape, layer, carried is None))
            out_shape.append(jax.ShapeDtypeStruct((n // seq_len, DEPTH) + shape, F32))
    outs = pl.pallas_call(
        functools.partial(_proj_kernel, rope=rope, n_carried=0 if carried is None else len(carried),
                          seq_len=seq_len, layer=layer),
        grid=(n // TOKEN_TILE,),
        in_specs=in_specs,
        out_specs=out_specs,
        out_shape=out_shape,
        input_output_aliases=aliases,
        scratch_shapes=[pltpu.VMEM((PROJ_PAD, D_MODEL), BF16)],
        compiler_params=_params(1),
    )(*args)
    result = {name: o for (name, _, _), o in zip(PROJ_OUT, outs)}
    if seq_len:
        result["cache"] = tuple(outs[len(PROJ_OUT):])
    return result


def _lane_mask(width, lo, hi):
    lane = lax.broadcasted_iota(jnp.int32, (1, width), 1)
    return jnp.where((lane >= lo) & (lane < hi), 1.0, 0.0).astype(F32)


def _attend_t(units, dv):
    outs = []
    q, keys, vt_ext = units[0]()
    s = _dot_nt(q, keys)
    for u in range(len(units)):
        s_now, vt_now = s, vt_ext
        if u + 1 < len(units):
            q, keys, vt_ext = units[u + 1]()
            s = _dot_nt(q, keys)
        e = jnp.exp((s_now - jnp.max(s_now, axis=-1, keepdims=True)).astype(BF16))
        r = _dot_nt(vt_now, e)
        outs.append(r[0:dv] * (1.0 / r[dv:dv + 1]))
    return outs


def _attn_seqs(n_cache, n_q_tiles):
    return ATTN_SEQS if (n_cache == 0 and n_q_tiles == 1) else 1


def _mla_kernel(*refs, n_cache, n_seq, n_tok):
    if n_cache:
        q_ref, ckv_ref, kr_ref, cckv_ref, ckr_ref, wuk_ref, wuvt_ref, o_ref, kf_scr, vt_scr = refs
    else:
        q_ref, ckv_ref, kr_ref, wuk_ref, wuvt_ref, o_ref, kf_scr, vt_scr = refs

    @pl.when(pl.program_id(1) == 0)
    def _():
        def fill(s, row0, ckv, kr):
            rows = ckv.shape[0]
            cb = ckv.astype(BF16)
            kr_all = jnp.concatenate([kr] * H_A, axis=1)
            kf_scr[s, row0:row0 + rows, :] = (_dot(cb, wuk_ref[...]) + kr_all).astype(BF16)
            vt = _dot_nt(wuvt_ref[...], cb)
            for h in range(H_A):
                vt_scr[s * H_A + h, 0:DV_A, row0:row0 + rows] = vt[h * DV_A:(h + 1) * DV_A].astype(BF16)

        for s in range(n_seq):
            if n_cache:
                fill(s, 0, cckv_ref[...], ckr_ref[...])
            tok = slice(s * n_tok, (s + 1) * n_tok)
            fill(s, n_cache, ckv_ref[tok, :], kr_ref[tok, :])
        for u in range(n_seq * H_A):
            vt_scr[u, DV_A:, :] = jnp.ones((ONES_ROWS, vt_scr.shape[2]), BF16)

    n_q = q_ref.shape[0] // n_seq

    def unit(s, h):
        sl = slice(h * HEAD_PAD, (h + 1) * HEAD_PAD)
        return lambda: (q_ref[s * n_q:(s + 1) * n_q, sl], kf_scr[s, :, sl], vt_scr[s * H_A + h])

    heads = _attend_t([unit(s, h) for h in range(H_A) for s in range(n_seq)], DV_A)
    for s in range(n_seq):
        o_ref[s * n_q:(s + 1) * n_q, :] = jnp.concatenate(heads[s::n_seq], axis=0).T


def _mla(p, lw, n_batch, n_tok, cache, layer):
    n_cache = 0 if cache is None else cache[0].shape[2]
    q_tile = min(Q_TILE, n_tok)
    nqt = n_tok // q_tile
    n_seq = _attn_seqs(n_cache, nqt)
    in_specs = [pl.BlockSpec((n_seq * q_tile, H_A * HEAD_PAD), lambda b, i: (b * nqt + i, 0)),
                pl.BlockSpec((n_seq * n_tok, KV_RANK), lambda b, i: (b, 0)),
                pl.BlockSpec((n_seq * n_tok, LANE), lambda b, i: (b, 0))]
    args = [p["q"], p["ckv"], p["kr"]]
    if n_cache:
        in_specs += [pl.BlockSpec((None, None, n_cache, KV_RANK), lambda b, i: (b, layer, 0, 0)),
                     pl.BlockSpec((None, None, n_cache, LANE), lambda b, i: (b, layer, 0, 0))]
        args += list(cache)
    in_specs += [_layer_spec((KV_RANK, H_A * HEAD_PAD), layer), _layer_spec((WIDTH_A, KV_RANK), layer)]
    args += [lw["wuk"], lw["wuvt"]]
    n_keys = n_cache + n_tok
    return pl.pallas_call(
        functools.partial(_mla_kernel, n_cache=n_cache, n_seq=n_seq, n_tok=n_tok),
        grid=(n_batch // n_seq, nqt),
        in_specs=in_specs,
        out_specs=pl.BlockSpec((n_seq * q_tile, WIDTH_A), lambda b, i: (b * nqt + i, 0)),
        out_shape=jax.ShapeDtypeStruct((n_batch * n_tok, WIDTH_A), F32),
        scratch_shapes=[pltpu.VMEM((n_seq, n_keys, H_A * HEAD_PAD), BF16),
                        pltpu.VMEM((n_seq * H_A, DV_A + ONES_ROWS, n_keys), BF16)],
        compiler_params=_params(2),
    )(*args)


def _diff_kernel(*refs, n_cache, n_seq, n_tok, lam_init):
    if n_cache:
        q_ref, k_ref, v_ref, ck_ref, cv_ref, lam_ref, o_ref, k_scr, vt_scr = refs
    else:
        q_ref, k_ref, v_ref, lam_ref, o_ref, k_scr, vt_scr = refs

    @pl.when(pl.program_id(1) == 0)
    def _():
        def fill(s, row0, k, v):
            rows = k.shape[0]
            k_scr[s, row0:row0 + rows, :] = k.astype(BF16)
            vt = v.T
            for h in range(H_B):
                vt_scr[s * H_B + h, 0:DV_B, row0:row0 + rows] = vt[h * DV_B:(h + 1) * DV_B].astype(BF16)

        for s in range(n_seq):
            if n_cache:
                fill(s, 0, ck_ref[...], cv_ref[...])
            tok = slice(s * n_tok, (s + 1) * n_tok)
            fill(s, n_cache, k_ref[tok, :], v_ref[tok, :])
        for u in range(n_seq * H_B):
            vt_scr[u, DV_B:, :] = jnp.ones((ONES_ROWS, vt_scr.shape[2]), BF16)

    lv = lam_ref[...]
    lam = (jnp.exp(jnp.sum(lv[0:1] * lv[1:2], axis=-1, keepdims=True))
           - jnp.exp(jnp.sum(lv[2:3] * lv[3:4], axis=-1, keepdims=True)) + lam_init)
    n_q = q_ref.shape[0] // n_seq
    lane = lax.broadcasted_iota(jnp.int32, (n_q, LANE), 1)

    def unit(s, h, j):
        blk = slice((h // 2) * LANE, (h // 2 + 1) * LANE)
        lo = (h % 2) * 2 * DH_B + j * DH_B

        def fn():
            qh = q_ref[s * n_q:(s + 1) * n_q, blk]
            qm = jnp.where((lane >= lo) & (lane < lo + DH_B), qh, jnp.zeros_like(qh))
            return qm, k_scr[s, :, blk], vt_scr[s * H_B + h]
        return fn

    maps = _attend_t([unit(s, h, j) for h in range(H_B) for j in range(2) for s in range(n_seq)], DV_B)
    for s in range(n_seq):
        mine = maps[s::n_seq]
        heads = [mine[2 * h] - lam * mine[2 * h + 1] for h in range(H_B)]
        o_ref[s * n_q:(s + 1) * n_q, :] = jnp.concatenate(heads, axis=0).T


def _diff(p, lw, n_batch, n_tok, cache, layer, lam_init):
    n_cache = 0 if cache is None else cache[0].shape[2]
    q_tile = min(Q_TILE, n_tok)
    nqt = n_tok // q_tile
    n_seq = _attn_seqs(n_cache, nqt)
    in_specs = [pl.BlockSpec((n_seq * q_tile, WIDTH_B), lambda b, i: (b * nqt + i, 0)),
                pl.BlockSpec((n_seq * n_tok, WIDTH_B), lambda b, i: (b, 0)),
                pl.BlockSpec((n_seq * n_tok, WIDTH_B), lambda b, i: (b, 0))]
    args = [p["qb"], p["kb"], p["vb"]]
    if n_cache:
        in_specs += [pl.BlockSpec((None, None, n_cache, WIDTH_B), lambda b, i: (b, layer, 0, 0))] * 2
        args += list(cache)
    in_specs.append(_layer_spec((4, DH_B), layer))
    args.append(lw["lam"])
    n_keys = n_cache + n_tok
    return pl.pallas_call(
        functools.partial(_diff_kernel, n_cache=n_cache, n_seq=n_seq, n_tok=n_tok, lam_init=lam_init),
        grid=(n_batch // n_seq, nqt),
        in_specs=in_specs,
        out_specs=pl.BlockSpec((n_seq * q_tile, WIDTH_B), lambda b, i: (b * nqt + i, 0)),
        out_shape=jax.ShapeDtypeStruct((n_batch * n_tok, WIDTH_B), F32),
        scratch_shapes=[pltpu.VMEM((n_seq, n_keys, WIDTH_B), BF16),
                        pltpu.VMEM((n_seq * H_B, DV_B + ONES_ROWS, n_keys), BF16)],
        compiler_params=_params(2),
    )(*args)


def _split3(x):
    hi = x.astype(BF16)
    r1 = x - hi.astype(F32)
    mid = r1.astype(BF16)
    lo = (r1 - mid.astype(F32)).astype(BF16)
    return hi, mid, lo


def _dot_exact_lhs(a_bf16, x):
    hi, mid, lo = _split3(x)
    return _dot(a_bf16, hi) + _dot(a_bf16, mid) + _dot(a_bf16, lo)


def _gla_kernel(*refs, n_tok, n_seq, has_init, emit_state, n_carried, layer):
    q_ref, k_ref, v_ref, gf_ref, gb_ref = refs[:5]
    refs = refs[5:]
    if has_init:
        s0f_ref, s0b_ref = refs[:2]
        refs = refs[2:]
    refs = refs[n_carried:]
    o_ref = refs[0]
    refs = refs[1:]
    if emit_state:
        sf_ref, sb_ref = refs[:2]
        refs = refs[2:]
    st_scr, bd_scr = refs


    n_chunks = n_tok // CHUNK
    n_groups = n_chunks // GLA_UNROLL
    t_idx = lax.broadcasted_iota(jnp.int32, (CHUNK, CHUNK), 0)
    s_idx = lax.broadcasted_iota(jnp.int32, (CHUNK, CHUNK), 1)
    tri = [jnp.where(s_idx <= t_idx, 1.0, 0.0).astype(F32), jnp.where(s_idx >= t_idx, 1.0, 0.0).astype(F32)]
    tri4 = [jnp.concatenate([t] * H_C, axis=1) for t in tri]
    tri_b = [t.astype(BF16) for t in tri]
    head_k = [_lane_mask(QK_C_PAD, h * DK_C, (h + 1) * DK_C) for h in range(H_C)]
    head_v = [_lane_mask(WIDTH_C, h * DV_C, (h + 1) * DV_C) for h in range(H_C)]
    row_v = lax.broadcasted_iota(jnp.int32, (WIDTH_C, QK_C_PAD), 0)
    col_k = lax.broadcasted_iota(jnp.int32, (WIDTH_C, QK_C_PAD), 1)
    diag = jnp.zeros((WIDTH_C, QK_C_PAD), F32)
    for h in range(H_C):
        inside = ((row_v >= h * DV_C) & (row_v < (h + 1) * DV_C)
                  & (col_k >= h * DK_C) & (col_k < (h + 1) * DK_C))
        diag = jnp.where(inside, 1.0, diag)

    for s in range(n_seq):
        for d, s0_ref in enumerate((s0f_ref, s0b_ref) if has_init else (None, None)):
            if s0_ref is None:
                st_scr[s, d] = jnp.zeros((WIDTH_C, QK_C_PAD), F32)
            else:
                bd_scr[...] = jnp.zeros_like(bd_scr)
                for h in range(H_C):
                    bd_scr[h * DK_C:(h + 1) * DK_C, h * DV_C:(h + 1) * DV_C] = s0_ref[s, h]
                st_scr[s, d] = bd_scr[...].T
    o_ref[...] = jnp.zeros_like(o_ref)

    def chunk_steps(chains):
        loaded = []
        for s, d, c in chains:
            g_ref = gb_ref if d else gf_ref
            row0 = s * n_tok + c * CHUNK
            rows = pl.ds(row0 if isinstance(row0, int) else pl.multiple_of(row0, CHUNK), CHUNK)
            bcum = _dot_exact_lhs(tri_b[d], g_ref[rows, :])
            loaded.append((rows, q_ref[rows, :], k_ref[rows, :], v_ref[rows, :], bcum))
        scaled = []
        for (s, d, c), (rows, q, k, v, bcum) in zip(chains, loaded):
            blast = bcum[0:1, :] if d else bcum[CHUNK - 1:CHUNK, :]
            qe = (q * jnp.exp(bcum)).astype(BF16)
            ke = k * jnp.exp(-bcum)
            kd = (k * jnp.exp(blast - bcum)).astype(BF16)
            ke_heads = jnp.concatenate([ke * head_k[h] for h in range(H_C)], axis=0).astype(BF16)
            v_heads = jnp.concatenate([v * head_v[h] for h in range(H_C)], axis=0).astype(BF16)
            scaled.append((qe, ke_heads, kd, v.astype(BF16), v_heads, jnp.exp(blast)))
        products = []
        for (s, d, c), (qe, ke_heads, kd, vb, v_heads, decay) in zip(chains, scaled):
            st = st_scr[s, d]
            attn = _dot_nt(qe, ke_heads)
            inter = _dot_nt(qe, st.astype(BF16))
            upd = _dot_tn(vb, kd)
            products.append((st, attn, inter, upd))
        masked = []
        for (s, d, c), (qe, ke_heads, kd, vb, v_heads, decay), (st, attn, inter, upd) in zip(
                chains, scaled, products):
            st_scr[s, d] = st * decay + upd * diag
            masked.append((attn * tri4[d]).astype(BF16))
        for (rows, *_), (_, _, _, _, v_heads, _), (_, _, inter, _), attn_b in zip(
                loaded, scaled, products, masked):
            o_ref[rows, :] += _dot(attn_b, v_heads) + inter

    def group(i):
        for j in range(GLA_UNROLL):
            fwd = i * GLA_UNROLL + j
            chunk_steps([(s, d, n_chunks - 1 - fwd if d else fwd) for s in range(n_seq) for d in (0, 1)])

    if n_groups == 1:
        group(0)
    else:
        def body(i, carry):
            group(i)
            return carry
        lax.fori_loop(0, n_groups, body, 0)

    if emit_state:
        for s in range(n_seq):
            for d, s_ref in enumerate((sf_ref, sb_ref)):
                bd_scr[...] = st_scr[s, d].T
                for h in range(H_C):
                    _put_layer(s_ref, (s,), (h,), layer, n_carried == 0,
                               bd_scr[h * DK_C:(h + 1) * DK_C, h * DV_C:(h + 1) * DV_C])


def _gla(p, n_batch, n_tok, layer, init=None, emit_state=False, carried=None):
    n_seq = max(GLA_SEQS, GLA_ROWS // n_tok)
    tok = lambda w: pl.BlockSpec((n_seq * n_tok, w), lambda b: (b, 0))
    st_spec = pl.BlockSpec((n_seq, None, H_C, DK_C, DV_C), lambda b: (b, layer, 0, 0, 0))
    in_specs = [tok(QK_C_PAD), tok(QK_C_PAD), tok(WIDTH_C), tok(QK_C_PAD), tok(QK_C_PAD)]
    args = [p["qc"], p["kc"], p["vc"], p["gf"], p["gb"]]
    if init is not None:
        in_specs += [st_spec] * 2
        args += list(init)
    out_specs = [tok(WIDTH_C)]
    out_shape = [jax.ShapeDtypeStruct((n_batch * n_tok, WIDTH_C), F32)]
    aliases = {}
    if emit_state:
        if carried is not None:
            for k in range(len(carried)):
                aliases[len(args) + k] = len(out_specs) + k
            in_specs += [pl.BlockSpec(memory_space=pl.ANY)] * len(carried)
            args += list(carried)
        out_specs += [_all_layer_spec(n_seq, (H_C, DK_C, DV_C), layer, carried is None)] * 2
        out_shape += [jax.ShapeDtypeStruct((n_batch, DEPTH, H_C, DK_C, DV_C), F32)] * 2
    return pl.pallas_call(
        functools.partial(_gla_kernel, n_tok=n_tok, n_seq=n_seq, has_init=init is not None,
                          emit_state=emit_state, n_carried=0 if carried is None else len(carried), layer=layer),
        grid=(n_batch // n_seq,),
        in_specs=in_specs,
        out_specs=out_specs,
        out_shape=out_shape,
        input_output_aliases=aliases,
        scratch_shapes=[pltpu.VMEM((n_seq, 2, WIDTH_C, QK_C_PAD), F32), pltpu.VMEM((QK_C_PAD, WIDTH_C), F32)],
        compiler_params=_params(1),
    )(*args)


def _group_rms(x, same_group_bf16, group, gain):
    sq = x * x
    hi = sq.astype(BF16)
    mid = (sq - hi.astype(F32)).astype(BF16)
    ms = (_dot(hi, same_group_bf16) + _dot(mid, same_group_bf16)) * (1.0 / group)
    return x * lax.rsqrt(ms + EPS) * gain


def _merge_ffn_kernel(x_ref, mod_ref, oa_ref, ob_ref, oc_ref, rc_ref, na_ref, nb_ref, nc_ref,
                      gb_ref, gc_ref, wo_ref, gain_ref, w13_ref, w2_ref, fin_ref, o_ref, acc_ref,
                      *, lam_init, final):
    a = _rms(oa_ref[...], na_ref[...])
    b = _group_rms(ob_ref[...], gb_ref[...], DV_B, nb_ref[...]) * (1.0 - lam_init)
    c = _group_rms(oc_ref[...], gc_ref[...], DV_C, nc_ref[...]) * _silu(rc_ref[...])
    mixed = (_dot(a.astype(BF16), wo_ref[0:WIDTH_A, :])
             + _dot(b.astype(BF16), wo_ref[WIDTH_A:WIDTH_A + WIDTH_B, :])
             + _dot(c.astype(BF16), wo_ref[WIDTH_A + WIDTH_B:, :]))
    x = x_ref[...] + mod_ref[0, 5:6, :] * mixed
    y = _ffn_block(x, mod_ref, gain_ref[...], w13_ref, w2_ref, acc_ref, 6)
    if final:
        y = _rms(y, fin_ref[...])
    o_ref[...] = y


def _merge_ffn(x, mod, tokens_per_group, oa, ob, oc, rc, lw, consts, fin, layer, lam_init, final):
    n = x.shape[0]
    row = lambda w: pl.BlockSpec((TOKEN_TILE, w), lambda i: (i, 0))
    return pl.pallas_call(
        functools.partial(_merge_ffn_kernel, lam_init=lam_init, final=final),
        grid=(n // TOKEN_TILE,),
        in_specs=[row(D_MODEL), _mod_spec(tokens_per_group), row(WIDTH_A), row(WIDTH_B), row(WIDTH_C),
                  row(WIDTH_C), _layer_spec((1, WIDTH_A), layer), _layer_spec((1, WIDTH_B), layer),
                  _layer_spec((1, WIDTH_C), layer), _const_spec((WIDTH_B, WIDTH_B)),
                  _const_spec((WIDTH_C, WIDTH_C)), _layer_spec((D_MODEL, D_MODEL), layer, 1),
                  _layer_spec((1, D_MODEL), layer), _layer_spec((D_MODEL, 2 * D_FF), layer, 1),
                  _layer_spec((D_FF, D_MODEL), layer, 1), _const_spec((1, D_MODEL))],
        out_specs=row(D_MODEL),
        out_shape=jax.ShapeDtypeStruct((n, D_MODEL), F32),
        scratch_shapes=[pltpu.VMEM((TOKEN_TILE, D_MODEL), F32)],
        compiler_params=_params(1),
    )(x, mod, oa, ob, oc, rc, lw["out_norm_a"], lw["out_norm_b"], lw["out_norm_c"],
      consts["group_b"], consts["group_c"], lw["wo"], lw["norm_ffn2"], lw["ffn2_w13"], lw["ffn2_w2"], fin)


def _prepare_weights(w):
    n_layers = w["w_in"].shape[0]
    row = lambda v: v.reshape(n_layers, 1, -1)

    def pad_heads(m, width):
        m = m.reshape(n_layers, m.shape[1], H_A, width)
        m = jnp.pad(m, ((0, 0), (0, 0), (0, 0), (0, HEAD_PAD - width)))
        return m.reshape(n_layers, m.shape[1], H_A * HEAD_PAD)

    def gate_block(wg, first):
        lo = 0 if first else QK_C_PAD
        return jnp.pad(wg, ((0, 0), (0, 0), (lo, 2 * QK_C_PAD - lo - H_C * DK_C)))

    wg = jnp.concatenate([gate_block(w["gla_wg_f"], True), gate_block(w["gla_wg_b"], False),
                          jnp.zeros((n_layers, LANE - 2 * GATE_RANK, 2 * QK_C_PAD), F32)], axis=1)
    bg = gate_block(w["gla_bg_f"][:, None, :], True) + gate_block(w["gla_bg_b"][:, None, :], False)
    return {
        "norm_ffn1": row(w["norm_ffn1"]), "ffn1_w13": w["ffn1_w13"].astype(BF16),
        "ffn1_w2": w["ffn1_w2"].astype(BF16),
        "norm_ffn2": row(w["norm_ffn2"]), "ffn2_w13": w["ffn2_w13"].astype(BF16),
        "ffn2_w2": w["ffn2_w2"].astype(BF16),
        "norm_mix": row(w["norm_mix"]), "w_in_t": jnp.swapaxes(w["w_in"], 1, 2),
        "q_norm": row(w["mla_q_norm"]),
        "wuq": pad_heads(w["mla_w_uq"], DN_A + DR_A).astype(BF16),
        "kv_norm": row(w["mla_kv_norm"]),
        "wuk": pad_heads(w["mla_w_uk"], DN_A).astype(BF16),
        "wuvt": jnp.swapaxes(w["mla_w_uv"], 1, 2).astype(BF16),
        "wg": wg.astype(BF16), "bg": bg,
        "lam": jnp.stack([w["diff_lq1"], w["diff_lk1"], w["diff_lq2"], w["diff_lk2"]], axis=1),
        "out_norm_a": row(w["mla_out_norm"]),
        "out_norm_b": row(jnp.tile(w["diff_norm"], (1, H_B))),
        "out_norm_c": row(jnp.tile(w["gla_norm"], (1, H_C))),
        "wo": w["w_out"].astype(BF16),
    }


def _rope_tables(n_pos):
    pos = jnp.arange(n_pos)
    rows = (pos // GRID_W).astype(F32)
    cols = (pos % GRID_W).astype(F32)
    half = DH_B // 2
    inv = ROPE_BASE ** (-jnp.arange(0, half, 2, dtype=F32) / half)
    ang = jnp.concatenate([rows[:, None] * inv, rows[:, None] * inv,
                           cols[:, None] * inv, cols[:, None] * inv], axis=1)
    sign = jnp.tile(jnp.concatenate([-jnp.ones(8, F32), jnp.ones(8, F32)]), 2)
    cos32, sin32 = jnp.cos(ang), jnp.sin(ang) * sign
    cos_full, sin_full = jnp.tile(cos32, (1, LANE // 32)), jnp.tile(sin32, (1, LANE // 32))
    ones = lambda n: jnp.ones((n_pos, n), F32)
    zeros = lambda n: jnp.zeros((n_pos, n), F32)
    cos_head = jnp.concatenate([ones(ROPE_OFF), cos32, ones(LANE - ROPE_OFF - DR_A)], axis=1)
    sin_head = jnp.concatenate([zeros(ROPE_OFF), sin32, zeros(LANE - ROPE_OFF - DR_A)], axis=1)
    return cos_head, sin_head, cos_full, sin_full


def _same_group_matrix(width, group):
    idx = jnp.arange(width) // group
    return jnp.where(idx[:, None] == idx[None, :], 1.0, 0.0).astype(BF16)


def kernel(x_prompt, x_sample, cache_mla_ckv, cache_mla_krope, cache_diff_k, cache_diff_v, state_gla_fwd, state_gla_bwd, c, c_ctx, w_mod, b_mod, norm_ffn1, ffn1_w13, ffn1_w2, norm_mix, w_in, mla_q_norm, mla_w_uq, mla_kv_norm, mla_w_uk, mla_w_uv, mla_out_norm, diff_lq1, diff_lk1, diff_lq2, diff_lk2, diff_norm, gla_wg_f, gla_bg_f, gla_wg_b, gla_bg_b, gla_norm, w_out, norm_ffn2, ffn2_w13, ffn2_w2, final_norm):
    w = dict(norm_ffn1=norm_ffn1, ffn1_w13=ffn1_w13, ffn1_w2=ffn1_w2, norm_mix=norm_mix, w_in=w_in,
             mla_q_norm=mla_q_norm, mla_w_uq=mla_w_uq, mla_kv_norm=mla_kv_norm, mla_w_uk=mla_w_uk,
             mla_w_uv=mla_w_uv, mla_out_norm=mla_out_norm, diff_lq1=diff_lq1, diff_lk1=diff_lk1,
             diff_lq2=diff_lq2, diff_lk2=diff_lk2, diff_norm=diff_norm, gla_wg_f=gla_wg_f,
             gla_bg_f=gla_bg_f, gla_wg_b=gla_wg_b, gla_bg_b=gla_bg_b, gla_norm=gla_norm, w_out=w_out,
             norm_ffn2=norm_ffn2, ffn2_w13=ffn2_w13, ffn2_w2=ffn2_w2)
    n_ctx_b, n_ctx_t, _ = x_prompt.shape
    n_lat_b, n_lat_t, _ = x_sample.shape
    n_past = cache_mla_ckv.shape[2]

    cvec = jnp.concatenate([c_ctx[None, :], c, jnp.zeros((16 - 1 - n_lat_b, D_MODEL), F32)], axis=0)
    mod = _modulation(cvec, w_mod, b_mod).reshape(DEPTH, 16, N_MOD, D_MODEL)

    rope_tabs = _rope_tables(n_lat_t)
    consts = {"group_b": _same_group_matrix(WIDTH_B, DV_B), "group_c": _same_group_matrix(WIDTH_C, DV_C)}
    cache_kr = jnp.pad(cache_mla_krope, ((0, 0), (0, 0), (0, 0), (ROPE_OFF, LANE - ROPE_OFF - DR_A)))
    cache_dk = cache_diff_k.reshape(n_lat_b, DEPTH, n_past, WIDTH_B)
    cache_dv = cache_diff_v.reshape(n_lat_b, DEPTH, n_past, WIDTH_B)
    fin = final_norm.reshape(1, D_MODEL)

    xp = x_prompt.reshape(n_ctx_b * n_ctx_t, D_MODEL)
    xs = x_sample.reshape(n_lat_b * n_lat_t, D_MODEL)
    new_kr = []
    new_cache = new_states = None
    lw = _prepare_weights(w)
    for l in range(DEPTH):
        lam_init = 0.8 - 0.6 * math.exp(-0.3 * l)
        last = l == DEPTH - 1
        mod_ctx, mod_lat = mod[l, 0:1], mod[l, 1:1 + n_lat_b]

        def trunk(x, m, tpg, n_batch, n_tok, rope, mla_cache, diff_cache, gla_init, is_ctx):
            x = _ffn(x, m, tpg, lw["norm_ffn1"], lw["ffn1_w13"], lw["ffn1_w2"], l, mod_row=0)
            p = _project(x, m, tpg, lw, lw["w_in_t"], l, rope, seq_len=n_tok if is_ctx else 0,
                         carried=new_cache if is_ctx else None)
            oa = _mla(p, lw, n_batch, n_tok, mla_cache, l)
            ob = _diff(p, lw, n_batch, n_tok, diff_cache, l, lam_init)
            gla_out = _gla(p, n_batch, n_tok, l, init=gla_init, emit_state=is_ctx,
                           carried=new_states if is_ctx else None)
            x = _merge_ffn(x, m, tpg, oa, ob, gla_out[0], p["rc"], lw, consts, fin, l, lam_init, last)
            return x, p, gla_out

        xp, p, gla_out = trunk(xp, mod_ctx, n_ctx_b * n_ctx_t, n_ctx_b, n_ctx_t, None, None, None, None, True)
        new_cache = p["cache"]
        new_states = tuple(gla_out[1:])
        new_kr.append(p["kr"][:, ROPE_OFF:ROPE_OFF + DR_A].reshape(n_ctx_b, n_ctx_t, DR_A))

        xs, _, _ = trunk(xs, mod_lat, n_lat_t, n_lat_b, n_lat_t, rope_tabs, (cache_mla_ckv, cache_kr),
                         (cache_dk, cache_dv), (state_gla_fwd, state_gla_bwd), False)

    new_ckv, new_dkt, new_dvt = new_cache

    def token_major(t, dh):
        return jnp.transpose(t.reshape(n_ctx_b, DEPTH, H_B, dh, n_ctx_t), (0, 1, 4, 2, 3))

    return (xp.reshape(x_prompt.shape), xs.reshape(x_sample.shape), new_ckv, jnp.stack(new_kr, axis=1),
            token_major(new_dkt, 2 * DH_B), token_major(new_dvt, DV_B), new_states[0], new_states[1])
```

```python
import functools
import math

import jax
import jax.numpy as jnp
from jax import lax
from jax.experimental import pallas as pl
from jax.experimental.pallas import tpu as pltpu

F32 = jnp.float32
BF16 = jnp.bfloat16

D_MODEL = 1024
DEPTH = 2
GRID_W = 64
ROPE_BASE = 10000.0
EPS = 1e-6
H_A, DN_A, DR_A, DV_A = 6, 64, 32, 64
Q_RANK, KV_RANK = 384, 256
H_B, DH_B, DV_B = 4, 32, 64
H_C, DK_C, DV_C = 4, 48, 96
GATE_RANK = 16
GATE_NORM = 16.0
CHUNK = 64
WIDTH_A = H_A * DV_A
WIDTH_B = H_B * DV_B
WIDTH_C = H_C * DV_C
D_FF = 2816
N_MOD = 9

LANE = 128
FF_CHUNK = 256
TOKEN_TILE = 512
FFN_TILE = 1024
Q_TILE = 1024
KEY_BLOCK = 256
ONES_ROWS = 16
MOD_TILE = 1152
GLA_UNROLL = 4
ATTN_SEQS = 4
GLA_SEQS = 2
GLA_ROWS = 1024
VMEM_LIMIT = 56 * 1024 * 1024

HEAD_PAD = LANE
ROPE_OFF = DN_A
QK_C_PAD = 256
SEG = {}
_off = 0
for _name, _w in (("cq", Q_RANK), ("krope", LANE), ("ckv", KV_RANK), ("qb", 256), ("kb", 256),
                  ("vb", 256), ("qc", QK_C_PAD), ("kc", QK_C_PAD), ("vc", WIDTH_C), ("gl", LANE),
                  ("rc", WIDTH_C)):
    SEG[_name] = (_off, _off + _w)
    _off += _w
PROJ_PAD = _off

PROJ_SPLITS = (Q_RANK, KV_RANK, DR_A, 2 * H_B * DH_B, 2 * H_B * DH_B, H_B * DV_B, H_C * DK_C, H_C * DK_C,
               WIDTH_C, WIDTH_C, GATE_RANK, GATE_RANK)
PROJ_COLS = sum(PROJ_SPLITS)
_SRC = [sum(PROJ_SPLITS[:i]) for i in range(len(PROJ_SPLITS) + 1)]


def _src(i, j=None):
    return ("src", _SRC[i], _SRC[(i if j is None else j) + 1])


PROJ_LAYOUT = (
    ("cq", (_src(0),)), ("ckv", (_src(1),)),
    ("krope", (("zero", 0, ROPE_OFF), _src(2), ("zero", 0, LANE - ROPE_OFF - DR_A))),
    ("qb", (_src(3),)), ("kb", (_src(4),)), ("vb", (_src(5),)),
    ("qc", (_src(6), ("zero", 0, QK_C_PAD - H_C * DK_C))),
    ("kc", (_src(7), ("zero", 0, QK_C_PAD - H_C * DK_C))),
    ("vc", (_src(8),)), ("rc", (_src(9),)),
    ("gl", (_src(10, 11), ("zero", 0, LANE - 2 * GATE_RANK))),
)


def _dot(a, b):
    return jnp.dot(a, b, preferred_element_type=F32)


def _dot_nt(a, b):
    return lax.dot_general(a, b, (((1,), (1,)), ((), ())), preferred_element_type=F32)


def _dot_tn(a, b):
    return lax.dot_general(a, b, (((0,), (0,)), ((), ())), preferred_element_type=F32)


def _rms(x, gain):
    ms = jnp.mean(x * x, axis=-1, keepdims=True)
    return (x * lax.rsqrt(ms + EPS)) * gain


def _silu(x):
    return x * jax.nn.sigmoid(x)


def _const_spec(shape):
    nd = len(shape)
    return pl.BlockSpec(shape, lambda *_: (0,) * nd)


def _layer_spec(shape, layer, buffers=None):
    nd = len(shape)
    mode = {} if buffers is None else {"pipeline_mode": pl.Buffered(buffers)}
    return pl.BlockSpec((None,) + tuple(shape), lambda *_: (layer,) + (0,) * nd, **mode)


def _params(n_axes):
    return pltpu.CompilerParams(dimension_semantics=("arbitrary",) * n_axes,
                                vmem_limit_bytes=VMEM_LIMIT)


def _mod_kernel(c_ref, w_ref, b_ref, o_ref):
    s = _silu(c_ref[...]).astype(BF16)
    o_ref[0] = _dot(s, w_ref[0].astype(BF16)) + b_ref[0]


def _modulation(cvec, w_mod, b_mod):
    n_rows = cvec.shape[0]
    width = N_MOD * D_MODEL
    return pl.pallas_call(
        _mod_kernel,
        grid=(DEPTH, width // MOD_TILE),
        in_specs=[
            pl.BlockSpec((n_rows, D_MODEL), lambda l, j: (0, 0)),
            pl.BlockSpec((1, D_MODEL, MOD_TILE), lambda l, j: (l, 0, j)),
            pl.BlockSpec((1, 1, MOD_TILE), lambda l, j: (l, 0, j)),
        ],
        out_specs=pl.BlockSpec((1, n_rows, MOD_TILE), lambda l, j: (l, 0, j)),
        out_shape=jax.ShapeDtypeStruct((DEPTH, n_rows, width), F32),
        compiler_params=_params(2),
    )(cvec, w_mod, b_mod.reshape(DEPTH, 1, width))


def _all_layer_spec(lead, shape, layer, first):
    zeros = (0,) * len(shape)
    if first:
        return pl.BlockSpec((lead, DEPTH) + tuple(shape), lambda i: (i, 0) + zeros)
    return pl.BlockSpec((lead, None) + tuple(shape), lambda i: (i, layer) + zeros)


def _put_layer(ref, lead, tail, layer, first, val):
    if not first:
        ref[lead + tail] = val
        return
    for l in range(DEPTH):
        ref[lead + (l,) + tail] = val if l == layer else jnp.zeros_like(val)


def _mod_spec(tokens_per_group, tile=TOKEN_TILE):
    return pl.BlockSpec((1, N_MOD, D_MODEL), lambda i: ((i * tile) // tokens_per_group, 0, 0))


def _ada_norm(x, gain, mod_ref, first_row):
    shift = mod_ref[0, first_row:first_row + 1, :]
    scale = mod_ref[0, first_row + 1:first_row + 2, :]
    return _rms(x, gain) * (1.0 + scale) + shift


def _ffn_block(x, mod_ref, gain, w13_ref, w2_ref, acc_ref, mod_row):
    u = _ada_norm(x, gain, mod_ref, mod_row).astype(BF16)
    for c in range(D_FF // FF_CHUNK):
        lo, hi = c * FF_CHUNK, (c + 1) * FF_CHUNK
        a = _dot(u, w13_ref[:, lo:hi])
        b = _dot(u, w13_ref[:, D_FF + lo:D_FF + hi])
        t = _dot((_silu(a) * b).astype(BF16), w2_ref[lo:hi, :])
        if c == 0:
            acc_ref[...] = t
        else:
            acc_ref[...] += t
    gate = mod_ref[0, mod_row + 2:mod_row + 3, :]
    return x + (0.5 * gate) * acc_ref[...]


def _ffn_kernel(*refs, mod_row, n_cast):
    x_ref, mod_ref, gain_ref, w13_ref, w2_ref = refs[:5]
    cast_in = refs[5:5 + n_cast]
    o_ref = refs[5 + n_cast]
    cast_out = refs[6 + n_cast:6 + 2 * n_cast]
    acc_ref = refs[6 + 2 * n_cast]
    for src, dst in zip(cast_in, cast_out):
        dst[...] = src[...].astype(BF16)
    o_ref[...] = _ffn_block(x_ref[...], mod_ref, gain_ref[...], w13_ref, w2_ref, acc_ref, mod_row)


def _weight_spec(arr, layer):
    if arr.ndim == 2:
        return pl.BlockSpec(arr.shape, lambda *_: (0, 0), pipeline_mode=pl.Buffered(1))
    return _layer_spec(arr.shape[1:], layer, 1)


def _ffn(x, mod, tokens_per_group, gain, w13, w2, layer, *, mod_row, cast=None):
    n = x.shape[0]
    steps = n // FFN_TILE
    tile = pl.BlockSpec((FFN_TILE, D_MODEL), lambda i: (i, 0))
    in_specs = [tile, _mod_spec(tokens_per_group, FFN_TILE), _layer_spec((1, D_MODEL), layer),
                _weight_spec(w13, layer), _weight_spec(w2, layer)]
    args = [x, mod, gain, w13, w2]
    out_specs = [tile]
    out_shape = [jax.ShapeDtypeStruct((n, D_MODEL), F32)]
    n_cast = 0
    if cast is not None:
        sources, cast_layer = cast
        n_cast = len(sources)
        for src in sources:
            rows, cols = src.shape[1] // steps, src.shape[2]
            in_specs.append(pl.BlockSpec((None, rows, cols), lambda i: (cast_layer, i, 0)))
            out_specs.append(pl.BlockSpec((rows, cols), lambda i: (i, 0)))
            out_shape.append(jax.ShapeDtypeStruct(src.shape[1:], BF16))
        args += list(sources)
    outs = pl.pallas_call(
        functools.partial(_ffn_kernel, mod_row=mod_row, n_cast=n_cast),
        grid=(steps,),
        in_specs=in_specs,
        out_specs=out_specs,
        out_shape=out_shape,
        scratch_shapes=[pltpu.VMEM((FFN_TILE, D_MODEL), F32)],
        compiler_params=_params(1),
    )(*args)
    return outs[0], tuple(outs[1:])


def _rope(x, cos, sin_signed):
    first = (lax.broadcasted_iota(jnp.int32, (x.shape[0], LANE), 1) % 16) < 8
    outs = []
    for j in range(x.shape[1] // LANE):
        xj = x[:, j * LANE:(j + 1) * LANE]
        partner = jnp.where(first, pltpu.roll(xj, LANE - 8, 1), pltpu.roll(xj, 8, 1))
        outs.append(xj * cos + partner * sin_signed)
    return outs[0] if len(outs) == 1 else jnp.concatenate(outs, axis=1)


def _log_sigmoid(z):
    return jnp.minimum(z, 0.0) - jnp.log1p(jnp.exp(-jnp.abs(z)))


def _proj_kernel(*refs, rope, n_carried, seq_len, layer):
    (x_ref, mod_ref, gain_ref, wint_ref, qn_ref, wuq_ref, kvn_ref, wg_ref, bg_ref) = refs[:9]
    refs = refs[9:]
    if rope:
        ch_ref, sh_ref, cf_ref, sf_ref = refs[:4]
        refs = refs[4:]
    refs = refs[n_carried:]
    (q_o, ckv_o, kr_o, qb_o, kb_o, vb_o, qc_o, kc_o, vc_o, rc_o, gf_o, gb_o) = refs[:12]
    refs = refs[12:]
    if seq_len:
        ckv_all_o, kbt_all_o, vbt_all_o = refs[:3]
        refs = refs[3:]
    wpt_ref, = refs

    @pl.when(pl.program_id(0) == 0)
    def _():
        for name, pieces in PROJ_LAYOUT:
            parts = []
            for kind, a, b in pieces:
                parts.append(jnp.zeros((b - a, D_MODEL), F32) if kind == "zero" else wint_ref[a:b, :])
            blk = parts[0] if len(parts) == 1 else jnp.concatenate(parts, axis=0)
            lo, hi = SEG[name]
            wpt_ref[lo:hi, :] = blk.astype(BF16)

    u = _ada_norm(x_ref[...], gain_ref[...], mod_ref, 3).astype(BF16)

    def seg(first, last=None):
        lo, hi = SEG[first][0], SEG[last or first][1]
        return _dot_nt(u, wpt_ref[lo:hi, :])

    cq_kr = seg("cq", "krope")
    q = _dot(_rms(cq_kr[:, :Q_RANK], qn_ref[...]).astype(BF16), wuq_ref[...])
    kr = cq_kr[:, Q_RANK:]
    qb = seg("qb")
    kb = seg("kb")
    if rope:
        q = _rope(q, ch_ref[...], sh_ref[...])
        kr = _rope(kr, ch_ref[...], sh_ref[...])
        qb = _rope(qb, cf_ref[...], sf_ref[...])
        kb = _rope(kb, cf_ref[...], sf_ref[...])
    q_o[...] = (q * (DN_A + DR_A) ** -0.5).astype(BF16)
    ckv = _rms(seg("ckv"), kvn_ref[...])
    ckv_o[...] = ckv
    if seq_len:
        kbt = _dot_nt(wpt_ref[SEG["kb"][0]:SEG["kb"][1], :], u)
        vbt = _dot_nt(wpt_ref[SEG["vb"][0]:SEG["vb"][1], :], u)
        for j in range(u.shape[0] // seq_len):
            tok = slice(j * seq_len, (j + 1) * seq_len)
            _put_layer(ckv_all_o, (j,), (), layer, n_carried == 0, ckv[tok, :])
            _put_layer(kbt_all_o, (j,), (), layer, n_carried == 0, kbt[:, tok])
            _put_layer(vbt_all_o, (j,), (), layer, n_carried == 0, vbt[:, tok])
    kr_o[...] = kr
    qb_o[...] = (qb * DH_B ** -0.5).astype(BF16)
    kb_o[...] = kb
    vb_o[...] = seg("vb")
    qc_o[...] = seg("qc") * (DK_C ** -0.5)
    kc_o[...] = seg("kc")
    vc_gl = seg("vc", "gl")
    vc_o[...] = vc_gl[:, :WIDTH_C]
    rc_o[...] = seg("rc")
    z = _dot(vc_gl[:, WIDTH_C:].astype(BF16), wg_ref[...]) + bg_ref[...]
    g = _log_sigmoid(z) / GATE_NORM
    gf_o[...] = g[:, :QK_C_PAD]
    gb_o[...] = g[:, QK_C_PAD:]


PROJ_OUT = (("q", H_A * HEAD_PAD, BF16), ("ckv", KV_RANK, F32), ("kr", LANE, F32), ("qb", 256, BF16),
            ("kb", 256, F32), ("vb", 256, F32), ("qc", QK_C_PAD, F32), ("kc", QK_C_PAD, F32),
            ("vc", WIDTH_C, F32), ("rc", WIDTH_C, F32), ("gf", QK_C_PAD, F32), ("gb", QK_C_PAD, F32))


def _project(x, mod, tokens_per_group, lw, w_in, layer, rope_tabs, seq_len=0, carried=None):
    n = x.shape[0]
    rope = rope_tabs is not None
    row = lambda w: pl.BlockSpec((TOKEN_TILE, w), lambda i: (i, 0))
    in_specs = [row(D_MODEL), _mod_spec(tokens_per_group), _layer_spec((1, D_MODEL), layer),
                _layer_spec((PROJ_COLS, D_MODEL), layer, 1), _layer_spec((1, Q_RANK), layer),
                _layer_spec((Q_RANK, H_A * HEAD_PAD), layer), _layer_spec((1, KV_RANK), layer),
                _layer_spec((LANE, 2 * QK_C_PAD), layer), _layer_spec((1, 2 * QK_C_PAD), layer)]
    args = [x, mod, lw["norm_mix"], w_in, lw["q_norm"], lw["wuq"], lw["kv_norm"], lw["wg"], lw["bg"]]
    if rope:
        n_pos = rope_tabs[0].shape[0]
        tab = pl.BlockSpec((TOKEN_TILE, LANE), lambda i: (i % (n_pos // TOKEN_TILE), 0))
        in_specs += [tab] * 4
        args += list(rope_tabs)
    out_specs = [row(w) for _, w, _ in PROJ_OUT]
    out_shape = [jax.ShapeDtypeStruct((n, w), dt) for _, w, dt in PROJ_OUT]
    aliases = {}
    if seq_len:
        per_tile = TOKEN_TILE // seq_len
        if carried is not None:
            for k in range(len(carried)):
                aliases[len(args) + k] = len(out_specs) + k
            in_specs += [pl.BlockSpec(memory_space=pl.ANY)] * len(carried)
            args += list(carried)
        for shape in ((seq_len, KV_RANK), (WIDTH_B, seq_len), (WIDTH_B, seq_len)):
            out_specs.append(_all_layer_spec(per_tile, shape, layer, carried is None))
            out_shape.append(jax.ShapeDtypeStruct((n // seq_len, DEPTH) + shape, F32))
    outs = pl.pallas_call(
        functools.partial(_proj_kernel, rope=rope, n_carried=0 if carried is None else len(carried),
                          seq_len=seq_len, layer=layer),
        grid=(n // TOKEN_TILE,),
        in_specs=in_specs,
        out_specs=out_specs,
        out_shape=out_shape,
        input_output_aliases=aliases,
        scratch_shapes=[pltpu.VMEM((PROJ_PAD, D_MODEL), BF16)],
        compiler_params=_params(1),
    )(*args)
    result = {name: o for (name, _, _), o in zip(PROJ_OUT, outs)}
    if seq_len:
        result["cache"] = tuple(outs[len(PROJ_OUT):])
    return result


def _lane_mask(width, lo, hi):
    lane = lax.broadcasted_iota(jnp.int32, (1, width), 1)
    return jnp.where((lane >= lo) & (lane < hi), 1.0, 0.0).astype(F32)


def _attend_t(units, dv):
    outs = []
    q, keys, vt_ext = units[0]()
    s = _dot_nt(q, keys)
    for u in range(len(units)):
        s_now, vt_now = s, vt_ext
        if u + 1 < len(units):
            q, keys, vt_ext = units[u + 1]()
            s = _dot_nt(q, keys)
        e = jnp.exp((s_now - jnp.max(s_now, axis=-1, keepdims=True)).astype(BF16))
        r = _dot_nt(vt_now, e)
        outs.append(r[0:dv] * (1.0 / r[dv:dv + 1]))
    return outs


def _attn_seqs(n_cache, n_q_tiles):
    return ATTN_SEQS if (n_cache == 0 and n_q_tiles == 1) else 1


def _mla_kernel(*refs, n_cache, n_seq, n_tok):
    if n_cache:
        q_ref, ckv_ref, kr_ref, cckv_ref, ckr_ref, wuk_ref, wuvt_ref, o_ref, kf_scr, vt_scr = refs
    else:
        q_ref, ckv_ref, kr_ref, wuk_ref, wuvt_ref, o_ref, kf_scr, vt_scr = refs

    @pl.when(pl.program_id(1) == 0)
    def _():
        def fill(s, row0, ckv, kr):
            rows = ckv.shape[0]
            cb = ckv.astype(BF16)
            kr_all = jnp.concatenate([kr] * H_A, axis=1)
            kf_scr[s, row0:row0 + rows, :] = (_dot(cb, wuk_ref[...]) + kr_all).astype(BF16)
            vt = _dot_nt(wuvt_ref[...], cb)
            for h in range(H_A):
                vt_scr[s * H_A + h, 0:DV_A, row0:row0 + rows] = vt[h * DV_A:(h + 1) * DV_A].astype(BF16)

        for s in range(n_seq):
            if n_cache:
                fill(s, 0, cckv_ref[...], ckr_ref[...])
            tok = slice(s * n_tok, (s + 1) * n_tok)
            fill(s, n_cache, ckv_ref[tok, :], kr_ref[tok, :])
        for u in range(n_seq * H_A):
            vt_scr[u, DV_A:, :] = jnp.ones((ONES_ROWS, vt_scr.shape[2]), BF16)

    n_q = q_ref.shape[0] // n_seq

    def unit(s, h):
        sl = slice(h * HEAD_PAD, (h + 1) * HEAD_PAD)
        return lambda: (q_ref[s * n_q:(s + 1) * n_q, sl], kf_scr[s, :, sl], vt_scr[s * H_A + h])

    heads = _attend_t([unit(s, h) for h in range(H_A) for s in range(n_seq)], DV_A)
    for s in range(n_seq):
        o_ref[s * n_q:(s + 1) * n_q, :] = jnp.concatenate(heads[s::n_seq], axis=0).T


def _mla(p, lw, n_batch, n_tok, cache, layer):
    n_cache = 0 if cache is None else cache[0].shape[2]
    q_tile = min(Q_TILE, n_tok)
    nqt = n_tok // q_tile
    n_seq = _attn_seqs(n_cache, nqt)
    in_specs = [pl.BlockSpec((n_seq * q_tile, H_A * HEAD_PAD), lambda b, i: (b * nqt + i, 0)),
                pl.BlockSpec((n_seq * n_tok, KV_RANK), lambda b, i: (b, 0)),
                pl.BlockSpec((n_seq * n_tok, LANE), lambda b, i: (b, 0))]
    args = [p["q"], p["ckv"], p["kr"]]
    if n_cache:
        in_specs += [pl.BlockSpec((None, None, n_cache, KV_RANK), lambda b, i: (b, layer, 0, 0)),
                     pl.BlockSpec((None, None, n_cache, LANE), lambda b, i: (b, layer, 0, 0))]
        args += list(cache)
    in_specs += [_layer_spec((KV_RANK, H_A * HEAD_PAD), layer), _layer_spec((WIDTH_A, KV_RANK), layer)]
    args += [lw["wuk"], lw["wuvt"]]
    n_keys = n_cache + n_tok
    return pl.pallas_call(
        functools.partial(_mla_kernel, n_cache=n_cache, n_seq=n_seq, n_tok=n_tok),
        grid=(n_batch // n_seq, nqt),
        in_specs=in_specs,
        out_specs=pl.BlockSpec((n_seq * q_tile, WIDTH_A), lambda b, i: (b * nqt + i, 0)),
        out_shape=jax.ShapeDtypeStruct((n_batch * n_tok, WIDTH_A), F32),
        scratch_shapes=[pltpu.VMEM((n_seq, n_keys, H_A * HEAD_PAD), BF16),
                        pltpu.VMEM((n_seq * H_A, DV_A + ONES_ROWS, n_keys), BF16)],
        compiler_params=_params(2),
    )(*args)


def _diff_kernel(*refs, n_cache, n_seq, n_tok, lam_init):
    if n_cache:
        q_ref, k_ref, v_ref, ck_ref, cv_ref, lam_ref, o_ref, k_scr, vt_scr = refs
    else:
        q_ref, k_ref, v_ref, lam_ref, o_ref, k_scr, vt_scr = refs

    @pl.when(pl.program_id(1) == 0)
    def _():
        def fill(s, row0, k, v):
            rows = k.shape[0]
            k_scr[s, row0:row0 + rows, :] = k.astype(BF16)
            vt = v.T
            for h in range(H_B):
                vt_scr[s * H_B + h, 0:DV_B, row0:row0 + rows] = vt[h * DV_B:(h + 1) * DV_B].astype(BF16)

        for s in range(n_seq):
            if n_cache:
                fill(s, 0, ck_ref[...], cv_ref[...])
            tok = slice(s * n_tok, (s + 1) * n_tok)
            fill(s, n_cache, k_ref[tok, :], v_ref[tok, :])
        for u in range(n_seq * H_B):
            vt_scr[u, DV_B:, :] = jnp.ones((ONES_ROWS, vt_scr.shape[2]), BF16)

    lv = lam_ref[...]
    lam = (jnp.exp(jnp.sum(lv[0:1] * lv[1:2], axis=-1, keepdims=True))
           - jnp.exp(jnp.sum(lv[2:3] * lv[3:4], axis=-1, keepdims=True)) + lam_init)
    n_q = q_ref.shape[0] // n_seq
    lane = lax.broadcasted_iota(jnp.int32, (n_q, LANE), 1)

    def unit(s, h, j):
        blk = slice((h // 2) * LANE, (h // 2 + 1) * LANE)
        lo = (h % 2) * 2 * DH_B + j * DH_B

        def fn():
            qh = q_ref[s * n_q:(s + 1) * n_q, blk]
            qm = jnp.where((lane >= lo) & (lane < lo + DH_B), qh, jnp.zeros_like(qh))
            return qm, k_scr[s, :, blk], vt_scr[s * H_B + h]
        return fn

    maps = _attend_t([unit(s, h, j) for h in range(H_B) for j in range(2) for s in range(n_seq)], DV_B)
    for s in range(n_seq):
        mine = maps[s::n_seq]
        heads = [mine[2 * h] - lam * mine[2 * h + 1] for h in range(H_B)]
        o_ref[s * n_q:(s + 1) * n_q, :] = jnp.concatenate(heads, axis=0).T


def _diff(p, lw, n_batch, n_tok, cache, layer, lam_init):
    n_cache = 0 if cache is None else cache[0].shape[2]
    q_tile = min(Q_TILE, n_tok)
    nqt = n_tok // q_tile
    n_seq = _attn_seqs(n_cache, nqt)
    in_specs = [pl.BlockSpec((n_seq * q_tile, WIDTH_B), lambda b, i: (b * nqt + i, 0)),
                pl.BlockSpec((n_seq * n_tok, WIDTH_B), lambda b, i: (b, 0)),
                pl.BlockSpec((n_seq * n_tok, WIDTH_B), lambda b, i: (b, 0))]
    args = [p["qb"], p["kb"], p["vb"]]
    if n_cache:
        in_specs += [pl.BlockSpec((None, None, n_cache, WIDTH_B), lambda b, i: (b, layer, 0, 0))] * 2
        args += list(cache)
    in_specs.append(_layer_spec((4, DH_B), layer))
    args.append(lw["lam"])
    n_keys = n_cache + n_tok
    return pl.pallas_call(
        functools.partial(_diff_kernel, n_cache=n_cache, n_seq=n_seq, n_tok=n_tok, lam_init=lam_init),
        grid=(n_batch // n_seq, nqt),
        in_specs=in_specs,
        out_specs=pl.BlockSpec((n_seq * q_tile, WIDTH_B), lambda b, i: (b * nqt + i, 0)),
        out_shape=jax.ShapeDtypeStruct((n_batch * n_tok, WIDTH_B), F32),
        scratch_shapes=[pltpu.VMEM((n_seq, n_keys, WIDTH_B), BF16),
                        pltpu.VMEM((n_seq * H_B, DV_B + ONES_ROWS, n_keys), BF16)],
        compiler_params=_params(2),
    )(*args)


def _split3(x):
    hi = x.astype(BF16)
    r1 = x - hi.astype(F32)
    mid = r1.astype(BF16)
    lo = (r1 - mid.astype(F32)).astype(BF16)
    return hi, mid, lo


def _dot_exact_lhs(a_bf16, x):
    hi, mid, lo = _split3(x)
    return _dot(a_bf16, hi) + _dot(a_bf16, mid) + _dot(a_bf16, lo)


def _gla_kernel(*refs, n_tok, n_seq, has_init, emit_state, n_carried, layer):
    q_ref, k_ref, v_ref, gf_ref, gb_ref = refs[:5]
    refs = refs[5:]
    if has_init:
        s0f_ref, s0b_ref = refs[:2]
        refs = refs[2:]
    refs = refs[n_carried:]
    o_ref = refs[0]
    refs = refs[1:]
    if emit_state:
        sf_ref, sb_ref = refs[:2]
        refs = refs[2:]
    st_scr, bd_scr = refs


    n_chunks = n_tok // CHUNK
    n_groups = n_chunks // GLA_UNROLL
    t_idx = lax.broadcasted_iota(jnp.int32, (CHUNK, CHUNK), 0)
    s_idx = lax.broadcasted_iota(jnp.int32, (CHUNK, CHUNK), 1)
    tri = [jnp.where(s_idx <= t_idx, 1.0, 0.0).astype(F32), jnp.where(s_idx >= t_idx, 1.0, 0.0).astype(F32)]
    tri4 = [jnp.concatenate([t] * H_C, axis=1) for t in tri]
    tri_b = [t.astype(BF16) for t in tri]
    head_k = [_lane_mask(QK_C_PAD, h * DK_C, (h + 1) * DK_C) for h in range(H_C)]
    head_v = [_lane_mask(WIDTH_C, h * DV_C, (h + 1) * DV_C) for h in range(H_C)]
    row_v = lax.broadcasted_iota(jnp.int32, (WIDTH_C, QK_C_PAD), 0)
    col_k = lax.broadcasted_iota(jnp.int32, (WIDTH_C, QK_C_PAD), 1)
    diag = jnp.zeros((WIDTH_C, QK_C_PAD), F32)
    for h in range(H_C):
        inside = ((row_v >= h * DV_C) & (row_v < (h + 1) * DV_C)
                  & (col_k >= h * DK_C) & (col_k < (h + 1) * DK_C))
        diag = jnp.where(inside, 1.0, diag)

    for s in range(n_seq):
        for d, s0_ref in enumerate((s0f_ref, s0b_ref) if has_init else (None, None)):
            if s0_ref is None:
                st_scr[s, d] = jnp.zeros((WIDTH_C, QK_C_PAD), F32)
            else:
                bd_scr[...] = jnp.zeros_like(bd_scr)
                for h in range(H_C):
                    bd_scr[h * DK_C:(h + 1) * DK_C, h * DV_C:(h + 1) * DV_C] = s0_ref[s, h]
                st_scr[s, d] = bd_scr[...].T
    o_ref[...] = jnp.zeros_like(o_ref)

    def chunk_steps(chains):
        loaded = []
        for s, d, c in chains:
            g_ref = gb_ref if d else gf_ref
            row0 = s * n_tok + c * CHUNK
            rows = pl.ds(row0 if isinstance(row0, int) else pl.multiple_of(row0, CHUNK), CHUNK)
            bcum = _dot_exact_lhs(tri_b[d], g_ref[rows, :])
            loaded.append((rows, q_ref[rows, :], k_ref[rows, :], v_ref[rows, :], bcum))
        scaled = []
        for (s, d, c), (rows, q, k, v, bcum) in zip(chains, loaded):
            blast = bcum[0:1, :] if d else bcum[CHUNK - 1:CHUNK, :]
            qe = (q * jnp.exp(bcum)).astype(BF16)
            ke = k * jnp.exp(-bcum)
            kd = (k * jnp.exp(blast - bcum)).astype(BF16)
            ke_heads = jnp.concatenate([ke * head_k[h] for h in range(H_C)], axis=0).astype(BF16)
            v_heads = jnp.concatenate([v * head_v[h] for h in range(H_C)], axis=0).astype(BF16)
            scaled.append((qe, ke_heads, kd, v.astype(BF16), v_heads, jnp.exp(blast)))
        products = []
        for (s, d, c), (qe, ke_heads, kd, vb, v_heads, decay) in zip(chains, scaled):
            st = st_scr[s, d]
            attn = _dot_nt(qe, ke_heads)
            inter = _dot_nt(qe, st.astype(BF16))
            upd = _dot_tn(vb, kd)
            products.append((st, attn, inter, upd))
        masked = []
        for (s, d, c), (qe, ke_heads, kd, vb, v_heads, decay), (st, attn, inter, upd) in zip(
                chains, scaled, products):
            st_scr[s, d] = st * decay + upd * diag
            masked.append((attn * tri4[d]).astype(BF16))
        for (rows, *_), (_, _, _, _, v_heads, _), (_, _, inter, _), attn_b in zip(
                loaded, scaled, products, masked):
            o_ref[rows, :] += _dot(attn_b, v_heads) + inter

    def group(i):
        for j in range(GLA_UNROLL):
            fwd = i * GLA_UNROLL + j
            chunk_steps([(s, d, n_chunks - 1 - fwd if d else fwd) for s in range(n_seq) for d in (0, 1)])

    if n_groups == 1:
        group(0)
    else:
        def body(i, carry):
            group(i)
            return carry
        lax.fori_loop(0, n_groups, body, 0)

    if emit_state:
        for s in range(n_seq):
            for d, s_ref in enumerate((sf_ref, sb_ref)):
                bd_scr[...] = st_scr[s, d].T
                for h in range(H_C):
                    _put_layer(s_ref, (s,), (h,), layer, n_carried == 0,
                               bd_scr[h * DK_C:(h + 1) * DK_C, h * DV_C:(h + 1) * DV_C])


def _gla(p, n_batch, n_tok, layer, init=None, emit_state=False, carried=None):
    n_seq = max(GLA_SEQS, GLA_ROWS // n_tok)
    tok = lambda w: pl.BlockSpec((n_seq * n_tok, w), lambda b: (b, 0))
    st_spec = pl.BlockSpec((n_seq, None, H_C, DK_C, DV_C), lambda b: (b, layer, 0, 0, 0))
    in_specs = [tok(QK_C_PAD), tok(QK_C_PAD), tok(WIDTH_C), tok(QK_C_PAD), tok(QK_C_PAD)]
    args = [p["qc"], p["kc"], p["vc"], p["gf"], p["gb"]]
    if init is not None:
        in_specs += [st_spec] * 2
        args += list(init)
    out_specs = [tok(WIDTH_C)]
    out_shape = [jax.ShapeDtypeStruct((n_batch * n_tok, WIDTH_C), F32)]
    aliases = {}
    if emit_state:
        if carried is not None:
            for k in range(len(carried)):
                aliases[len(args) + k] = len(out_specs) + k
            in_specs += [pl.BlockSpec(memory_space=pl.ANY)] * len(carried)
            args += list(carried)
        out_specs += [_all_layer_spec(n_seq, (H_C, DK_C, DV_C), layer, carried is None)] * 2
        out_shape += [jax.ShapeDtypeStruct((n_batch, DEPTH, H_C, DK_C, DV_C), F32)] * 2
    return pl.pallas_call(
        functools.partial(_gla_kernel, n_tok=n_tok, n_seq=n_seq, has_init=init is not None,
                          emit_state=emit_state, n_carried=0 if carried is None else len(carried), layer=layer),
        grid=(n_batch // n_seq,),
        in_specs=in_specs,
        out_specs=out_specs,
        out_shape=out_shape,
        input_output_aliases=aliases,
        scratch_shapes=[pltpu.VMEM((n_seq, 2, WIDTH_C, QK_C_PAD), F32), pltpu.VMEM((QK_C_PAD, WIDTH_C), F32)],
        compiler_params=_params(1),
    )(*args)


def _group_rms(x, same_group_bf16, group, gain):
    sq = x * x
    hi = sq.astype(BF16)
    mid = (sq - hi.astype(F32)).astype(BF16)
    ms = (_dot(hi, same_group_bf16) + _dot(mid, same_group_bf16)) * (1.0 / group)
    return x * lax.rsqrt(ms + EPS) * gain


def _merge_ffn_kernel(x_ref, mod_ref, oa_ref, ob_ref, oc_ref, rc_ref, na_ref, nb_ref, nc_ref,
                      gb_ref, gc_ref, wo_ref, gain_ref, w13_ref, w2_ref, fin_ref, o_ref, acc_ref,
                      *, lam_init, final):
    a = _rms(oa_ref[...], na_ref[...])
    b = _group_rms(ob_ref[...], gb_ref[...], DV_B, nb_ref[...]) * (1.0 - lam_init)
    c = _group_rms(oc_ref[...], gc_ref[...], DV_C, nc_ref[...]) * _silu(rc_ref[...])
    mixed = (_dot(a.astype(BF16), wo_ref[0:WIDTH_A, :])
             + _dot(b.astype(BF16), wo_ref[WIDTH_A:WIDTH_A + WIDTH_B, :])
             + _dot(c.astype(BF16), wo_ref[WIDTH_A + WIDTH_B:, :]))
    x = x_ref[...] + mod_ref[0, 5:6, :] * mixed
    y = _ffn_block(x, mod_ref, gain_ref[...], w13_ref, w2_ref, acc_ref, 6)
    if final:
        y = _rms(y, fin_ref[...])
    o_ref[...] = y


def _merge_ffn(x, mod, tokens_per_group, oa, ob, oc, rc, lw, w13, w2, consts, fin, layer, lam_init, final):
    n = x.shape[0]
    row = lambda w: pl.BlockSpec((TOKEN_TILE, w), lambda i: (i, 0))
    return pl.pallas_call(
        functools.partial(_merge_ffn_kernel, lam_init=lam_init, final=final),
        grid=(n // TOKEN_TILE,),
        in_specs=[row(D_MODEL), _mod_spec(tokens_per_group), row(WIDTH_A), row(WIDTH_B), row(WIDTH_C),
                  row(WIDTH_C), _layer_spec((1, WIDTH_A), layer), _layer_spec((1, WIDTH_B), layer),
                  _layer_spec((1, WIDTH_C), layer), _const_spec((WIDTH_B, WIDTH_B)),
                  _const_spec((WIDTH_C, WIDTH_C)), _layer_spec((D_MODEL, D_MODEL), layer, 1),
                  _layer_spec((1, D_MODEL), layer), _weight_spec(w13, layer), _weight_spec(w2, layer),
                  _const_spec((1, D_MODEL))],
        out_specs=row(D_MODEL),
        out_shape=jax.ShapeDtypeStruct((n, D_MODEL), F32),
        scratch_shapes=[pltpu.VMEM((TOKEN_TILE, D_MODEL), F32)],
        compiler_params=_params(1),
    )(x, mod, oa, ob, oc, rc, lw["out_norm_a"], lw["out_norm_b"], lw["out_norm_c"],
      consts["group_b"], consts["group_c"], lw["wo"], lw["norm_ffn2"], w13, w2, fin)


def _prepare_weights(w):
    n_layers = w["w_in"].shape[0]
    row = lambda v: v.reshape(n_layers, 1, -1)

    def pad_heads(m, width):
        m = m.reshape(n_layers, m.shape[1], H_A, width)
        m = jnp.pad(m, ((0, 0), (0, 0), (0, 0), (0, HEAD_PAD - width)))
        return m.reshape(n_layers, m.shape[1], H_A * HEAD_PAD)

    def gate_block(wg, first):
        lo = 0 if first else QK_C_PAD
        return jnp.pad(wg, ((0, 0), (0, 0), (lo, 2 * QK_C_PAD - lo - H_C * DK_C)))

    wg = jnp.concatenate([gate_block(w["gla_wg_f"], True), gate_block(w["gla_wg_b"], False),
                          jnp.zeros((n_layers, LANE - 2 * GATE_RANK, 2 * QK_C_PAD), F32)], axis=1)
    bg = gate_block(w["gla_bg_f"][:, None, :], True) + gate_block(w["gla_bg_b"][:, None, :], False)
    return {
        "norm_ffn1": row(w["norm_ffn1"]), "norm_ffn2": row(w["norm_ffn2"]),
        "norm_mix": row(w["norm_mix"]), "w_in_t": jnp.swapaxes(w["w_in"], 1, 2),
        "q_norm": row(w["mla_q_norm"]),
        "wuq": pad_heads(w["mla_w_uq"], DN_A + DR_A).astype(BF16),
        "kv_norm": row(w["mla_kv_norm"]),
        "wuk": pad_heads(w["mla_w_uk"], DN_A).astype(BF16),
        "wuvt": jnp.swapaxes(w["mla_w_uv"], 1, 2).astype(BF16),
        "wg": wg.astype(BF16), "bg": bg,
        "lam": jnp.stack([w["diff_lq1"], w["diff_lk1"], w["diff_lq2"], w["diff_lk2"]], axis=1),
        "out_norm_a": row(w["mla_out_norm"]),
        "out_norm_b": row(jnp.tile(w["diff_norm"], (1, H_B))),
        "out_norm_c": row(jnp.tile(w["gla_norm"], (1, H_C))),
        "wo": w["w_out"].astype(BF16),
    }


def _rope_tables(n_pos):
    pos = jnp.arange(n_pos)
    rows = (pos // GRID_W).astype(F32)
    cols = (pos % GRID_W).astype(F32)
    half = DH_B // 2
    inv = ROPE_BASE ** (-jnp.arange(0, half, 2, dtype=F32) / half)
    ang = jnp.concatenate([rows[:, None] * inv, rows[:, None] * inv,
                           cols[:, None] * inv, cols[:, None] * inv], axis=1)
    sign = jnp.tile(jnp.concatenate([-jnp.ones(8, F32), jnp.ones(8, F32)]), 2)
    cos32, sin32 = jnp.cos(ang), jnp.sin(ang) * sign
    cos_full, sin_full = jnp.tile(cos32, (1, LANE // 32)), jnp.tile(sin32, (1, LANE // 32))
    ones = lambda n: jnp.ones((n_pos, n), F32)
    zeros = lambda n: jnp.zeros((n_pos, n), F32)
    cos_head = jnp.concatenate([ones(ROPE_OFF), cos32, ones(LANE - ROPE_OFF - DR_A)], axis=1)
    sin_head = jnp.concatenate([zeros(ROPE_OFF), sin32, zeros(LANE - ROPE_OFF - DR_A)], axis=1)
    return cos_head, sin_head, cos_full, sin_full


def _same_group_matrix(width, group):
    idx = jnp.arange(width) // group
    return jnp.where(idx[:, None] == idx[None, :], 1.0, 0.0).astype(BF16)


def kernel(x_prompt, x_sample, cache_mla_ckv, cache_mla_krope, cache_diff_k, cache_diff_v, state_gla_fwd, state_gla_bwd, c, c_ctx, w_mod, b_mod, norm_ffn1, ffn1_w13, ffn1_w2, norm_mix, w_in, mla_q_norm, mla_w_uq, mla_kv_norm, mla_w_uk, mla_w_uv, mla_out_norm, diff_lq1, diff_lk1, diff_lq2, diff_lk2, diff_norm, gla_wg_f, gla_bg_f, gla_wg_b, gla_bg_b, gla_norm, w_out, norm_ffn2, ffn2_w13, ffn2_w2, final_norm):
    w = dict(norm_ffn1=norm_ffn1, ffn1_w13=ffn1_w13, ffn1_w2=ffn1_w2, norm_mix=norm_mix, w_in=w_in,
             mla_q_norm=mla_q_norm, mla_w_uq=mla_w_uq, mla_kv_norm=mla_kv_norm, mla_w_uk=mla_w_uk,
             mla_w_uv=mla_w_uv, mla_out_norm=mla_out_norm, diff_lq1=diff_lq1, diff_lk1=diff_lk1,
             diff_lq2=diff_lq2, diff_lk2=diff_lk2, diff_norm=diff_norm, gla_wg_f=gla_wg_f,
             gla_bg_f=gla_bg_f, gla_wg_b=gla_wg_b, gla_bg_b=gla_bg_b, gla_norm=gla_norm, w_out=w_out,
             norm_ffn2=norm_ffn2, ffn2_w13=ffn2_w13, ffn2_w2=ffn2_w2)
    n_ctx_b, n_ctx_t, _ = x_prompt.shape
    n_lat_b, n_lat_t, _ = x_sample.shape
    n_past = cache_mla_ckv.shape[2]

    cvec = jnp.concatenate([c_ctx[None, :], c, jnp.zeros((16 - 1 - n_lat_b, D_MODEL), F32)], axis=0)
    mod = _modulation(cvec, w_mod, b_mod).reshape(DEPTH, 16, N_MOD, D_MODEL)

    rope_tabs = _rope_tables(n_lat_t)
    consts = {"group_b": _same_group_matrix(WIDTH_B, DV_B), "group_c": _same_group_matrix(WIDTH_C, DV_C)}
    cache_kr = jnp.pad(cache_mla_krope, ((0, 0), (0, 0), (0, 0), (ROPE_OFF, LANE - ROPE_OFF - DR_A)))
    cache_dk = cache_diff_k.reshape(n_lat_b, DEPTH, n_past, WIDTH_B)
    cache_dv = cache_diff_v.reshape(n_lat_b, DEPTH, n_past, WIDTH_B)
    fin = final_norm.reshape(1, D_MODEL)

    xp = x_prompt.reshape(n_ctx_b * n_ctx_t, D_MODEL)
    xs = x_sample.reshape(n_lat_b * n_lat_t, D_MODEL)
    new_kr = []
    new_cache = new_states = None
    lw = _prepare_weights(w)
    ffn1_w = (ffn1_w13[0].astype(BF16), ffn1_w2[0].astype(BF16))
    for l in range(DEPTH):
        lam_init = 0.8 - 0.6 * math.exp(-0.3 * l)
        last = l == DEPTH - 1
        mod_ctx, mod_lat = mod[l, 0:1], mod[l, 1:1 + n_lat_b]

        def trunk(x, m, tpg, n_batch, n_tok, rope, mla_cache, diff_cache, gla_init, is_ctx, ffn2_w):
            if is_ctx:
                cast = ((ffn2_w13, ffn2_w2), l)
            else:
                cast = None if last else ((ffn1_w13, ffn1_w2), l + 1)
            x, converted = _ffn(x, m, tpg, lw["norm_ffn1"], ffn1_w[0], ffn1_w[1], l, mod_row=0, cast=cast)
            if is_ctx:
                ffn2_w = converted
            p = _project(x, m, tpg, lw, lw["w_in_t"], l, rope, seq_len=n_tok if is_ctx else 0,
                         carried=new_cache if is_ctx else None)
            oa = _mla(p, lw, n_batch, n_tok, mla_cache, l)
            ob = _diff(p, lw, n_batch, n_tok, diff_cache, l, lam_init)
            gla_out = _gla(p, n_batch, n_tok, l, init=gla_init, emit_state=is_ctx,
                           carried=new_states if is_ctx else None)
            x = _merge_ffn(x, m, tpg, oa, ob, gla_out[0], p["rc"], lw, ffn2_w[0], ffn2_w[1], consts, fin, l,
                           lam_init, last)
            return x, p, gla_out, converted

        xp, p, gla_out, ffn2_w = trunk(xp, mod_ctx, n_ctx_b * n_ctx_t, n_ctx_b, n_ctx_t, None, None, None,
                                       None, True, None)
        new_cache = p["cache"]
        new_states = tuple(gla_out[1:])
        new_kr.append(p["kr"][:, ROPE_OFF:ROPE_OFF + DR_A].reshape(n_ctx_b, n_ctx_t, DR_A))

        xs, _, _, next_ffn1_w = trunk(xs, mod_lat, n_lat_t, n_lat_b, n_lat_t, rope_tabs,
                                      (cache_mla_ckv, cache_kr), (cache_dk, cache_dv),
                                      (state_gla_fwd, state_gla_bwd), False, ffn2_w)
        ffn1_w = next_ffn1_w

    new_ckv, new_dkt, new_dvt = new_cache

    def token_major(t, dh):
        return jnp.transpose(t.reshape(n_ctx_b, DEPTH, H_B, dh, n_ctx_t), (0, 1, 4, 2, 3))

    return (xp.reshape(x_prompt.shape), xs.reshape(x_sample.shape), new_ckv, jnp.stack(new_kr, axis=1),
            token_major(new_dkt, 2 * DH_B), token_major(new_dvt, DV_B), new_states[0], new_states[1])
```

```python
import functools
import math

import jax
import jax.numpy as jnp
from jax import lax
from jax.experimental import pallas as pl
from jax.experimental.pallas import tpu as pltpu

F32 = jnp.float32
BF16 = jnp.bfloat16

D_MODEL = 1024
DEPTH = 2
GRID_W = 64
ROPE_BASE = 10000.0
EPS = 1e-6
H_A, DN_A, DR_A, DV_A = 6, 64, 32, 64
Q_RANK, KV_RANK = 384, 256
H_B, DH_B, DV_B = 4, 32, 64
H_C, DK_C, DV_C = 4, 48, 96
GATE_RANK = 16
GATE_NORM = 16.0
CHUNK = 64
WIDTH_A = H_A * DV_A
WIDTH_B = H_B * DV_B
WIDTH_C = H_C * DV_C
D_FF = 2816
N_MOD = 9

LANE = 128
FF_CHUNK = 256
TOKEN_TILE = 512
FFN_TILE = 1024
Q_TILE = 1024
MOD_ROWS = 16
ONES_ROWS = 16
MOD_TILE = 2304
GLA_UNROLL = 4
ATTN_SEQS = 4
GLA_SEQS = 2
GLA_ROWS = 1024
VMEM_LIMIT = 56 * 1024 * 1024

HEAD_PAD = LANE
ROPE_OFF = DN_A
QK_C_PAD = 256
SEG = {}
_off = 0
for _name, _w in (("cq", Q_RANK), ("krope", LANE), ("ckv", KV_RANK), ("qb", WIDTH_B), ("kb", WIDTH_B),
                  ("vb", WIDTH_B), ("qc", QK_C_PAD), ("kc", QK_C_PAD), ("vc", WIDTH_C), ("gl", LANE),
                  ("rc", WIDTH_C)):
    SEG[_name] = (_off, _off + _w)
    _off += _w
PROJ_PAD = _off

PROJ_SPLITS = (Q_RANK, KV_RANK, DR_A, 2 * H_B * DH_B, 2 * H_B * DH_B, H_B * DV_B, H_C * DK_C, H_C * DK_C,
               WIDTH_C, WIDTH_C, GATE_RANK, GATE_RANK)
PROJ_COLS = sum(PROJ_SPLITS)
_SRC = [sum(PROJ_SPLITS[:i]) for i in range(len(PROJ_SPLITS) + 1)]


def _src(i, j=None):
    return ("src", _SRC[i], _SRC[(i if j is None else j) + 1])


PROJ_LAYOUT = (
    ("cq", (_src(0),)), ("ckv", (_src(1),)),
    ("krope", (("zero", 0, ROPE_OFF), _src(2), ("zero", 0, LANE - ROPE_OFF - DR_A))),
    ("qb", (_src(3),)), ("kb", (_src(4),)), ("vb", (_src(5),)),
    ("qc", (_src(6), ("zero", 0, QK_C_PAD - H_C * DK_C))),
    ("kc", (_src(7), ("zero", 0, QK_C_PAD - H_C * DK_C))),
    ("vc", (_src(8),)), ("rc", (_src(9),)),
    ("gl", (_src(10, 11), ("zero", 0, LANE - 2 * GATE_RANK))),
)


def _dot(a, b):
    return jnp.dot(a, b, preferred_element_type=F32)


def _dot_nt(a, b):
    return lax.dot_general(a, b, (((1,), (1,)), ((), ())), preferred_element_type=F32)


def _dot_tn(a, b):
    return lax.dot_general(a, b, (((0,), (0,)), ((), ())), preferred_element_type=F32)


def _rms(x, gain):
    ms = jnp.mean(x * x, axis=-1, keepdims=True)
    return (x * lax.rsqrt(ms + EPS)) * gain


def _silu(x):
    return x * jax.nn.sigmoid(x)


def _const_spec(shape):
    nd = len(shape)
    return pl.BlockSpec(shape, lambda *_: (0,) * nd)


def _layer_spec(shape, layer, buffers=None):
    nd = len(shape)
    mode = {} if buffers is None else {"pipeline_mode": pl.Buffered(buffers)}
    return pl.BlockSpec((None,) + tuple(shape), lambda *_: (layer,) + (0,) * nd, **mode)


def _params(n_axes):
    return pltpu.CompilerParams(dimension_semantics=("arbitrary",) * n_axes,
                                vmem_limit_bytes=VMEM_LIMIT)


def _mod_kernel(c_ref, w_ref, b_ref, o_ref):
    s = _silu(c_ref[...]).astype(BF16)
    o_ref[0] = _dot(s, w_ref[0].astype(BF16)) + b_ref[0]


def _modulation(cvec, w_mod, b_mod):
    n_rows = cvec.shape[0]
    width = N_MOD * D_MODEL
    return pl.pallas_call(
        _mod_kernel,
        grid=(DEPTH, width // MOD_TILE),
        in_specs=[
            pl.BlockSpec((n_rows, D_MODEL), lambda l, j: (0, 0)),
            pl.BlockSpec((1, D_MODEL, MOD_TILE), lambda l, j: (l, 0, j)),
            pl.BlockSpec((1, 1, MOD_TILE), lambda l, j: (l, 0, j)),
        ],
        out_specs=pl.BlockSpec((1, n_rows, MOD_TILE), lambda l, j: (l, 0, j)),
        out_shape=jax.ShapeDtypeStruct((DEPTH, n_rows, width), F32),
        compiler_params=_params(2),
    )(cvec, w_mod, b_mod.reshape(DEPTH, 1, width))


def _all_layer_spec(lead, shape, layer, first):
    zeros = (0,) * len(shape)
    if first:
        return pl.BlockSpec((lead, DEPTH) + tuple(shape), lambda i: (i, 0) + zeros)
    return pl.BlockSpec((lead, None) + tuple(shape), lambda i: (i, layer) + zeros)


def _put_layer(ref, lead, tail, layer, first, val):
    if not first:
        ref[lead + tail] = val
        return
    for l in range(DEPTH):
        ref[lead + (l,) + tail] = val if l == layer else jnp.zeros_like(val)


def _mod_spec(tokens_per_group, tile=TOKEN_TILE):
    return pl.BlockSpec((1, N_MOD, D_MODEL), lambda i: ((i * tile) // tokens_per_group, 0, 0))


def _ada_norm(x, gain, mod_ref, first_row):
    shift = mod_ref[0, first_row:first_row + 1, :]
    scale = mod_ref[0, first_row + 1:first_row + 2, :]
    return _rms(x, gain) * (1.0 + scale) + shift


def _ffn_block(x, mod_ref, gain, w13_ref, w2_ref, acc_ref, mod_row):
    u = _ada_norm(x, gain, mod_ref, mod_row).astype(BF16)
    for c in range(D_FF // FF_CHUNK):
        lo, hi = c * FF_CHUNK, (c + 1) * FF_CHUNK
        a = _dot(u, w13_ref[:, lo:hi])
        b = _dot(u, w13_ref[:, D_FF + lo:D_FF + hi])
        t = _dot((_silu(a) * b).astype(BF16), w2_ref[lo:hi, :])
        if c == 0:
            acc_ref[...] = t
        else:
            acc_ref[...] += t
    gate = mod_ref[0, mod_row + 2:mod_row + 3, :]
    return x + (0.5 * gate) * acc_ref[...]


def _ffn_kernel(*refs, mod_row, n_cast):
    x_ref, mod_ref, gain_ref, w13_ref, w2_ref = refs[:5]
    cast_in = refs[5:5 + n_cast]
    o_ref = refs[5 + n_cast]
    cast_out = refs[6 + n_cast:6 + 2 * n_cast]
    acc_ref = refs[6 + 2 * n_cast]
    for src, dst in zip(cast_in, cast_out):
        dst[...] = src[...].astype(BF16)
    o_ref[...] = _ffn_block(x_ref[...], mod_ref, gain_ref[...], w13_ref, w2_ref, acc_ref, mod_row)


def _weight_spec(arr, layer):
    if arr.ndim == 2:
        return pl.BlockSpec(arr.shape, lambda *_: (0, 0), pipeline_mode=pl.Buffered(1))
    return _layer_spec(arr.shape[1:], layer, 1)


def _ffn(x, mod, tokens_per_group, gain, w13, w2, layer, *, mod_row, cast=None):
    n = x.shape[0]
    steps = n // FFN_TILE
    tile = pl.BlockSpec((FFN_TILE, D_MODEL), lambda i: (i, 0))
    in_specs = [tile, _mod_spec(tokens_per_group, FFN_TILE), _layer_spec((1, D_MODEL), layer),
                _weight_spec(w13, layer), _weight_spec(w2, layer)]
    args = [x, mod, gain, w13, w2]
    out_specs = [tile]
    out_shape = [jax.ShapeDtypeStruct((n, D_MODEL), F32)]
    n_cast = 0
    if cast is not None:
        sources, cast_layer = cast
        n_cast = len(sources)
        for src in sources:
            rows, cols = src.shape[1] // steps, src.shape[2]
            in_specs.append(pl.BlockSpec((None, rows, cols), lambda i: (cast_layer, i, 0)))
            out_specs.append(pl.BlockSpec((rows, cols), lambda i: (i, 0)))
            out_shape.append(jax.ShapeDtypeStruct(src.shape[1:], BF16))
        args += list(sources)
    outs = pl.pallas_call(
        functools.partial(_ffn_kernel, mod_row=mod_row, n_cast=n_cast),
        grid=(steps,),
        in_specs=in_specs,
        out_specs=out_specs,
        out_shape=out_shape,
        scratch_shapes=[pltpu.VMEM((FFN_TILE, D_MODEL), F32)],
        compiler_params=_params(1),
    )(*args)
    return outs[0], tuple(outs[1:])


def _rope(x, cos, sin_signed):
    first = (lax.broadcasted_iota(jnp.int32, (x.shape[0], LANE), 1) % 16) < 8
    outs = []
    for j in range(x.shape[1] // LANE):
        xj = x[:, j * LANE:(j + 1) * LANE]
        partner = jnp.where(first, pltpu.roll(xj, LANE - 8, 1), pltpu.roll(xj, 8, 1))
        outs.append(xj * cos + partner * sin_signed)
    return outs[0] if len(outs) == 1 else jnp.concatenate(outs, axis=1)


def _log_sigmoid(z):
    return jnp.minimum(z, 0.0) - jnp.log1p(jnp.exp(-jnp.abs(z)))


def _proj_kernel(*refs, rope, n_carried, seq_len, layer):
    (x_ref, mod_ref, gain_ref, wint_ref, qn_ref, wuq_ref, kvn_ref, wg_ref, bg_ref) = refs[:9]
    refs = refs[9:]
    if rope:
        ch_ref, sh_ref, cf_ref, sf_ref = refs[:4]
        refs = refs[4:]
    refs = refs[n_carried:]
    (q_o, ckv_o, kr_o, qb_o, kb_o, vb_o, qc_o, kc_o, vc_o, rc_o, gf_o, gb_o) = refs[:12]
    refs = refs[12:]
    if seq_len:
        ckv_all_o, kbt_all_o, vbt_all_o = refs[:3]
        refs = refs[3:]
    wpt_ref, = refs

    @pl.when(pl.program_id(0) == 0)
    def _():
        for name, pieces in PROJ_LAYOUT:
            parts = []
            for kind, a, b in pieces:
                parts.append(jnp.zeros((b - a, D_MODEL), F32) if kind == "zero" else wint_ref[a:b, :])
            blk = parts[0] if len(parts) == 1 else jnp.concatenate(parts, axis=0)
            lo, hi = SEG[name]
            wpt_ref[lo:hi, :] = blk.astype(BF16)

    u = _ada_norm(x_ref[...], gain_ref[...], mod_ref, 3).astype(BF16)

    def seg(first, last=None):
        lo, hi = SEG[first][0], SEG[last or first][1]
        return _dot_nt(u, wpt_ref[lo:hi, :])

    cq_kr = seg("cq", "krope")
    q = _dot(_rms(cq_kr[:, :Q_RANK], qn_ref[...]).astype(BF16), wuq_ref[...])
    kr = cq_kr[:, Q_RANK:]
    qb = seg("qb")
    if seq_len:
        kbt = _dot_nt(wpt_ref[SEG["kb"][0]:SEG["kb"][1], :], u)
        vbt = _dot_nt(wpt_ref[SEG["vb"][0]:SEG["vb"][1], :], u)
        kb, vb = kbt.T, vbt.T
    else:
        kb, vb = seg("kb"), seg("vb")
    if rope:
        q = _rope(q, ch_ref[...], sh_ref[...])
        kr = _rope(kr, ch_ref[...], sh_ref[...])
        qb = _rope(qb, cf_ref[...], sf_ref[...])
        kb = _rope(kb, cf_ref[...], sf_ref[...])
    q_o[...] = (q * (DN_A + DR_A) ** -0.5).astype(BF16)
    ckv = _rms(seg("ckv"), kvn_ref[...])
    ckv_o[...] = ckv
    if seq_len:
        for j in range(u.shape[0] // seq_len):
            tok = slice(j * seq_len, (j + 1) * seq_len)
            _put_layer(ckv_all_o, (j,), (), layer, n_carried == 0, ckv[tok, :])
            _put_layer(kbt_all_o, (j,), (), layer, n_carried == 0, kbt[:, tok])
            _put_layer(vbt_all_o, (j,), (), layer, n_carried == 0, vbt[:, tok])
    kr_o[...] = kr
    qb_o[...] = (qb * DH_B ** -0.5).astype(BF16)
    kb_o[...] = kb
    vb_o[...] = vb
    qc_o[...] = seg("qc") * (DK_C ** -0.5)
    kc_o[...] = seg("kc")
    vc_gl = seg("vc", "gl")
    vc_o[...] = vc_gl[:, :WIDTH_C]
    rc_o[...] = seg("rc")
    z = _dot(vc_gl[:, WIDTH_C:].astype(BF16), wg_ref[...]) + bg_ref[...]
    g = _log_sigmoid(z) / GATE_NORM
    gf_o[...] = g[:, :QK_C_PAD]
    gb_o[...] = g[:, QK_C_PAD:]


PROJ_OUT = (("q", H_A * HEAD_PAD, BF16), ("ckv", KV_RANK, F32), ("kr", LANE, F32), ("qb", WIDTH_B, BF16),
            ("kb", WIDTH_B, F32), ("vb", WIDTH_B, F32), ("qc", QK_C_PAD, F32), ("kc", QK_C_PAD, F32),
            ("vc", WIDTH_C, F32), ("rc", WIDTH_C, F32), ("gf", QK_C_PAD, F32), ("gb", QK_C_PAD, F32))


def _project(x, mod, tokens_per_group, lw, w_in, layer, rope_tabs, seq_len=0, carried=None):
    n = x.shape[0]
    rope = rope_tabs is not None
    row = lambda w: pl.BlockSpec((TOKEN_TILE, w), lambda i: (i, 0))
    in_specs = [row(D_MODEL), _mod_spec(tokens_per_group), _layer_spec((1, D_MODEL), layer),
                _layer_spec((PROJ_COLS, D_MODEL), layer, 1), _layer_spec((1, Q_RANK), layer),
                _layer_spec((Q_RANK, H_A * HEAD_PAD), layer), _layer_spec((1, KV_RANK), layer),
                _layer_spec((LANE, 2 * QK_C_PAD), layer), _layer_spec((1, 2 * QK_C_PAD), layer)]
    args = [x, mod, lw["norm_mix"], w_in, lw["q_norm"], lw["wuq"], lw["kv_norm"], lw["wg"], lw["bg"]]
    if rope:
        n_pos = rope_tabs[0].shape[0]
        tab = pl.BlockSpec((TOKEN_TILE, LANE), lambda i: (i % (n_pos // TOKEN_TILE), 0))
        in_specs += [tab] * 4
        args += list(rope_tabs)
    out_specs = [row(w) for _, w, _ in PROJ_OUT]
    out_shape = [jax.ShapeDtypeStruct((n, w), dt) for _, w, dt in PROJ_OUT]
    aliases = {}
    if seq_len:
        per_tile = TOKEN_TILE // seq_len
        if carried is not None:
            for k in range(len(carried)):
                aliases[len(args) + k] = len(out_specs) + k
            in_specs += [pl.BlockSpec(memory_space=pl.ANY)] * len(carried)
            args += list(carried)
        for shape in ((seq_len, KV_RANK), (WIDTH_B, seq_len), (WIDTH_B, seq_len)):
            out_specs.append(_all_layer_spec(per_tile, shape, layer, carried is None))
            out_shape.append(jax.ShapeDtypeStruct((n // seq_len, DEPTH) + shape, F32))
    outs = pl.pallas_call(
        functools.partial(_proj_kernel, rope=rope, n_carried=0 if carried is None else len(carried),
                          seq_len=seq_len, layer=layer),
        grid=(n // TOKEN_TILE,),
        in_specs=in_specs,
        out_specs=out_specs,
        out_shape=out_shape,
        input_output_aliases=aliases,
        scratch_shapes=[pltpu.VMEM((PROJ_PAD, D_MODEL), BF16)],
        compiler_params=_params(1),
    )(*args)
    result = {name: o for (name, _, _), o in zip(PROJ_OUT, outs)}
    if seq_len:
        result["cache"] = tuple(outs[len(PROJ_OUT):])
    return result


def _lane_mask(width, lo, hi):
    lane = lax.broadcasted_iota(jnp.int32, (1, width), 1)
    return jnp.where((lane >= lo) & (lane < hi), 1.0, 0.0).astype(F32)


def _attend_t(units, dv):
    outs = []
    q, keys, vt_ext = units[0]()
    s = _dot_nt(q, keys)
    for u in range(len(units)):
        s_now, vt_now = s, vt_ext
        if u + 1 < len(units):
            q, keys, vt_ext = units[u + 1]()
            s = _dot_nt(q, keys)
        e = jnp.exp((s_now - jnp.max(s_now, axis=-1, keepdims=True)).astype(BF16))
        r = _dot_nt(vt_now, e)
        outs.append(r[0:dv] * (1.0 / r[dv:dv + 1]))
    return outs


def _attn_seqs(n_cache, n_q_tiles):
    return ATTN_SEQS if (n_cache == 0 and n_q_tiles == 1) else 1


def _mla_kernel(*refs, n_cache, n_seq, n_tok):
    if n_cache:
        q_ref, ckv_ref, kr_ref, cckv_ref, ckr_ref, wuk_ref, wuvt_ref, o_ref, kf_scr, vt_scr = refs
    else:
        q_ref, ckv_ref, kr_ref, wuk_ref, wuvt_ref, o_ref, kf_scr, vt_scr = refs

    @pl.when(pl.program_id(1) == 0)
    def _():
        def fill(s, row0, ckv, kr):
            rows = ckv.shape[0]
            cb = ckv.astype(BF16)
            kr_all = jnp.concatenate([kr] * H_A, axis=1)
            kf_scr[s, row0:row0 + rows, :] = (_dot(cb, wuk_ref[...]) + kr_all).astype(BF16)
            vt = _dot_nt(wuvt_ref[...], cb)
            for h in range(H_A):
                vt_scr[s * H_A + h, 0:DV_A, row0:row0 + rows] = vt[h * DV_A:(h + 1) * DV_A].astype(BF16)

        for s in range(n_seq):
            if n_cache:
                fill(s, 0, cckv_ref[...], ckr_ref[...])
            tok = slice(s * n_tok, (s + 1) * n_tok)
            fill(s, n_cache, ckv_ref[tok, :], kr_ref[tok, :])
        for u in range(n_seq * H_A):
            vt_scr[u, DV_A:, :] = jnp.ones((ONES_ROWS, vt_scr.shape[2]), BF16)

    n_q = q_ref.shape[0] // n_seq

    def unit(s, h):
        sl = slice(h * HEAD_PAD, (h + 1) * HEAD_PAD)
        return lambda: (q_ref[s * n_q:(s + 1) * n_q, sl], kf_scr[s, :, sl], vt_scr[s * H_A + h])

    heads = _attend_t([unit(s, h) for h in range(H_A) for s in range(n_seq)], DV_A)
    for s in range(n_seq):
        o_ref[s * n_q:(s + 1) * n_q, :] = jnp.concatenate(heads[s::n_seq], axis=0).T


def _mla(p, lw, n_batch, n_tok, cache, layer):
    n_cache = 0 if cache is None else cache[0].shape[2]
    q_tile = min(Q_TILE, n_tok)
    nqt = n_tok // q_tile
    n_seq = _attn_seqs(n_cache, nqt)
    in_specs = [pl.BlockSpec((n_seq * q_tile, H_A * HEAD_PAD), lambda b, i: (b * nqt + i, 0)),
                pl.BlockSpec((n_seq * n_tok, KV_RANK), lambda b, i: (b, 0)),
                pl.BlockSpec((n_seq * n_tok, LANE), lambda b, i: (b, 0))]
    args = [p["q"], p["ckv"], p["kr"]]
    if n_cache:
        in_specs += [pl.BlockSpec((None, None, n_cache, KV_RANK), lambda b, i: (b, layer, 0, 0)),
                     pl.BlockSpec((None, None, n_cache, LANE), lambda b, i: (b, layer, 0, 0))]
        args += list(cache)
    in_specs += [_layer_spec((KV_RANK, H_A * HEAD_PAD), layer), _layer_spec((WIDTH_A, KV_RANK), layer)]
    args += [lw["wuk"], lw["wuvt"]]
    n_keys = n_cache + n_tok
    return pl.pallas_call(
        functools.partial(_mla_kernel, n_cache=n_cache, n_seq=n_seq, n_tok=n_tok),
        grid=(n_batch // n_seq, nqt),
        in_specs=in_specs,
        out_specs=pl.BlockSpec((n_seq * q_tile, WIDTH_A), lambda b, i: (b * nqt + i, 0)),
        out_shape=jax.ShapeDtypeStruct((n_batch * n_tok, WIDTH_A), F32),
        scratch_shapes=[pltpu.VMEM((n_seq, n_keys, H_A * HEAD_PAD), BF16),
                        pltpu.VMEM((n_seq * H_A, DV_A + ONES_ROWS, n_keys), BF16)],
        compiler_params=_params(2),
    )(*args)


def _diff_kernel(*refs, n_cache, n_seq, n_tok, lam_init):
    if n_cache:
        q_ref, k_ref, v_ref, ck_ref, cv_ref, lam_ref, o_ref, k_scr, vt_scr = refs
    else:
        q_ref, k_ref, v_ref, lam_ref, o_ref, k_scr, vt_scr = refs

    @pl.when(pl.program_id(1) == 0)
    def _():
        def fill(s, row0, k, v):
            rows = k.shape[0]
            k_scr[s, row0:row0 + rows, :] = k.astype(BF16)
            vt = v.T
            for h in range(H_B):
                vt_scr[s * H_B + h, 0:DV_B, row0:row0 + rows] = vt[h * DV_B:(h + 1) * DV_B].astype(BF16)

        for s in range(n_seq):
            if n_cache:
                fill(s, 0, ck_ref[...], cv_ref[...])
            tok = slice(s * n_tok, (s + 1) * n_tok)
            fill(s, n_cache, k_ref[tok, :], v_ref[tok, :])
        for u in range(n_seq * H_B):
            vt_scr[u, DV_B:, :] = jnp.ones((ONES_ROWS, vt_scr.shape[2]), BF16)

    lv = lam_ref[...]
    lam = (jnp.exp(jnp.sum(lv[0:1] * lv[1:2], axis=-1, keepdims=True))
           - jnp.exp(jnp.sum(lv[2:3] * lv[3:4], axis=-1, keepdims=True)) + lam_init)
    n_q = q_ref.shape[0] // n_seq
    lane = lax.broadcasted_iota(jnp.int32, (n_q, LANE), 1)

    def unit(s, h, j):
        blk = slice((h // 2) * LANE, (h // 2 + 1) * LANE)
        lo = (h % 2) * 2 * DH_B + j * DH_B

        def fn():
            qh = q_ref[s * n_q:(s + 1) * n_q, blk]
            qm = jnp.where((lane >= lo) & (lane < lo + DH_B), qh, jnp.zeros_like(qh))
            return qm, k_scr[s, :, blk], vt_scr[s * H_B + h]
        return fn

    maps = _attend_t([unit(s, h, j) for h in range(H_B) for j in range(2) for s in range(n_seq)], DV_B)
    for s in range(n_seq):
        mine = maps[s::n_seq]
        heads = [mine[2 * h] - lam * mine[2 * h + 1] for h in range(H_B)]
        o_ref[s * n_q:(s + 1) * n_q, :] = jnp.concatenate(heads, axis=0).T


def _diff(p, lw, n_batch, n_tok, cache, layer, lam_init):
    n_cache = 0 if cache is None else cache[0].shape[2]
    q_tile = min(Q_TILE, n_tok)
    nqt = n_tok // q_tile
    n_seq = _attn_seqs(n_cache, nqt)
    in_specs = [pl.BlockSpec((n_seq * q_tile, WIDTH_B), lambda b, i: (b * nqt + i, 0)),
                pl.BlockSpec((n_seq * n_tok, WIDTH_B), lambda b, i: (b, 0)),
                pl.BlockSpec((n_seq * n_tok, WIDTH_B), lambda b, i: (b, 0))]
    args = [p["qb"], p["kb"], p["vb"]]
    if n_cache:
        in_specs += [pl.BlockSpec((None, None, n_cache, WIDTH_B), lambda b, i: (b, layer, 0, 0))] * 2
        args += list(cache)
    in_specs.append(_layer_spec((4, DH_B), layer))
    args.append(lw["lam"])
    n_keys = n_cache + n_tok
    return pl.pallas_call(
        functools.partial(_diff_kernel, n_cache=n_cache, n_seq=n_seq, n_tok=n_tok, lam_init=lam_init),
        grid=(n_batch // n_seq, nqt),
        in_specs=in_specs,
        out_specs=pl.BlockSpec((n_seq * q_tile, WIDTH_B), lambda b, i: (b * nqt + i, 0)),
        out_shape=jax.ShapeDtypeStruct((n_batch * n_tok, WIDTH_B), F32),
        scratch_shapes=[pltpu.VMEM((n_seq, n_keys, WIDTH_B), BF16),
                        pltpu.VMEM((n_seq * H_B, DV_B + ONES_ROWS, n_keys), BF16)],
        compiler_params=_params(2),
    )(*args)


def _split3(x):
    hi = x.astype(BF16)
    r1 = x - hi.astype(F32)
    mid = r1.astype(BF16)
    lo = (r1 - mid.astype(F32)).astype(BF16)
    return hi, mid, lo


def _dot_exact_lhs(a_bf16, x):
    hi, mid, lo = _split3(x)
    return _dot(a_bf16, hi) + _dot(a_bf16, mid) + _dot(a_bf16, lo)


def _gla_kernel(*refs, n_tok, n_seq, has_init, emit_state, n_carried, layer):
    q_ref, k_ref, v_ref, gf_ref, gb_ref = refs[:5]
    refs = refs[5:]
    if has_init:
        s0f_ref, s0b_ref = refs[:2]
        refs = refs[2:]
    refs = refs[n_carried:]
    o_ref = refs[0]
    refs = refs[1:]
    if emit_state:
        sf_ref, sb_ref = refs[:2]
        refs = refs[2:]
    st_scr, bd_scr = refs


    n_chunks = n_tok // CHUNK
    n_groups = n_chunks // GLA_UNROLL
    t_idx = lax.broadcasted_iota(jnp.int32, (CHUNK, CHUNK), 0)
    s_idx = lax.broadcasted_iota(jnp.int32, (CHUNK, CHUNK), 1)
    tri = [jnp.where(s_idx <= t_idx, 1.0, 0.0).astype(F32), jnp.where(s_idx >= t_idx, 1.0, 0.0).astype(F32)]
    tri4 = [jnp.concatenate([t] * H_C, axis=1) for t in tri]
    tri_b = [t.astype(BF16) for t in tri]
    head_k = [_lane_mask(QK_C_PAD, h * DK_C, (h + 1) * DK_C).astype(BF16) for h in range(H_C)]
    head_v = [_lane_mask(WIDTH_C, h * DV_C, (h + 1) * DV_C).astype(BF16) for h in range(H_C)]
    row_v = lax.broadcasted_iota(jnp.int32, (WIDTH_C, QK_C_PAD), 0)
    col_k = lax.broadcasted_iota(jnp.int32, (WIDTH_C, QK_C_PAD), 1)
    diag = jnp.zeros((WIDTH_C, QK_C_PAD), F32)
    for h in range(H_C):
        inside = ((row_v >= h * DV_C) & (row_v < (h + 1) * DV_C)
                  & (col_k >= h * DK_C) & (col_k < (h + 1) * DK_C))
        diag = jnp.where(inside, 1.0, diag)

    for s in range(n_seq):
        for d, s0_ref in enumerate((s0f_ref, s0b_ref) if has_init else (None, None)):
            if s0_ref is None:
                st_scr[s, d] = jnp.zeros((WIDTH_C, QK_C_PAD), F32)
            else:
                bd_scr[...] = jnp.zeros_like(bd_scr)
                for h in range(H_C):
                    bd_scr[h * DK_C:(h + 1) * DK_C, h * DV_C:(h + 1) * DV_C] = s0_ref[s, h]
                st_scr[s, d] = bd_scr[...].T
    o_ref[...] = jnp.zeros_like(o_ref)

    def chunk_steps(chains):
        loaded = []
        for s, d, c in chains:
            g_ref = gb_ref if d else gf_ref
            row0 = s * n_tok + c * CHUNK
            rows = pl.ds(row0 if isinstance(row0, int) else pl.multiple_of(row0, CHUNK), CHUNK)
            bcum = _dot_exact_lhs(tri_b[d], g_ref[rows, :])
            loaded.append((rows, q_ref[rows, :], k_ref[rows, :], v_ref[rows, :], bcum))
        scaled = []
        for (s, d, c), (rows, q, k, v, bcum) in zip(chains, loaded):
            blast = bcum[0:1, :] if d else bcum[CHUNK - 1:CHUNK, :]
            qe = (q * jnp.exp(bcum)).astype(BF16)
            ke = (k * jnp.exp(-bcum)).astype(BF16)
            kd = (k * jnp.exp(blast - bcum)).astype(BF16)
            vb = v.astype(BF16)
            ke_heads = jnp.concatenate([ke * head_k[h] for h in range(H_C)], axis=0)
            v_heads = jnp.concatenate([vb * head_v[h] for h in range(H_C)], axis=0)
            scaled.append((qe, ke_heads, kd, vb, v_heads, jnp.exp(blast)))
        products = []
        for (s, d, c), (qe, ke_heads, kd, vb, v_heads, decay) in zip(chains, scaled):
            st = st_scr[s, d]
            attn = _dot_nt(qe, ke_heads)
            inter = _dot_nt(qe, st.astype(BF16))
            upd = _dot_tn(vb, kd)
            products.append((st, attn, inter, upd))
        masked = []
        for (s, d, c), (qe, ke_heads, kd, vb, v_heads, decay), (st, attn, inter, upd) in zip(
                chains, scaled, products):
            st_scr[s, d] = st * decay + upd * diag
            masked.append((attn * tri4[d]).astype(BF16))
        for (rows, *_), (_, _, _, _, v_heads, _), (_, _, inter, _), attn_b in zip(
                loaded, scaled, products, masked):
            o_ref[rows, :] += _dot(attn_b, v_heads) + inter

    def group(i):
        for j in range(GLA_UNROLL):
            fwd = i * GLA_UNROLL + j
            chunk_steps([(s, d, n_chunks - 1 - fwd if d else fwd) for s in range(n_seq) for d in (0, 1)])

    if n_groups == 1:
        group(0)
    else:
        def body(i, carry):
            group(i)
            return carry
        lax.fori_loop(0, n_groups, body, 0)

    if emit_state:
        for s in range(n_seq):
            for d, s_ref in enumerate((sf_ref, sb_ref)):
                bd_scr[...] = st_scr[s, d].T
                for h in range(H_C):
                    _put_layer(s_ref, (s,), (h,), layer, n_carried == 0,
                               bd_scr[h * DK_C:(h + 1) * DK_C, h * DV_C:(h + 1) * DV_C])


def _gla(p, n_batch, n_tok, layer, init=None, emit_state=False, carried=None):
    n_seq = max(GLA_SEQS, GLA_ROWS // n_tok)
    tok = lambda w: pl.BlockSpec((n_seq * n_tok, w), lambda b: (b, 0))
    st_spec = pl.BlockSpec((n_seq, None, H_C, DK_C, DV_C), lambda b: (b, layer, 0, 0, 0))
    in_specs = [tok(QK_C_PAD), tok(QK_C_PAD), tok(WIDTH_C), tok(QK_C_PAD), tok(QK_C_PAD)]
    args = [p["qc"], p["kc"], p["vc"], p["gf"], p["gb"]]
    if init is not None:
        in_specs += [st_spec] * 2
        args += list(init)
    out_specs = [tok(WIDTH_C)]
    out_shape = [jax.ShapeDtypeStruct((n_batch * n_tok, WIDTH_C), F32)]
    aliases = {}
    if emit_state:
        if carried is not None:
            for k in range(len(carried)):
                aliases[len(args) + k] = len(out_specs) + k
            in_specs += [pl.BlockSpec(memory_space=pl.ANY)] * len(carried)
            args += list(carried)
        out_specs += [_all_layer_spec(n_seq, (H_C, DK_C, DV_C), layer, carried is None)] * 2
        out_shape += [jax.ShapeDtypeStruct((n_batch, DEPTH, H_C, DK_C, DV_C), F32)] * 2
    return pl.pallas_call(
        functools.partial(_gla_kernel, n_tok=n_tok, n_seq=n_seq, has_init=init is not None,
                          emit_state=emit_state, n_carried=0 if carried is None else len(carried), layer=layer),
        grid=(n_batch // n_seq,),
        in_specs=in_specs,
        out_specs=out_specs,
        out_shape=out_shape,
        input_output_aliases=aliases,
        scratch_shapes=[pltpu.VMEM((n_seq, 2, WIDTH_C, QK_C_PAD), F32), pltpu.VMEM((QK_C_PAD, WIDTH_C), F32)],
        compiler_params=_params(1),
    )(*args)


def _group_rms(x, same_group_bf16, group, gain):
    sq = x * x
    hi = sq.astype(BF16)
    mid = (sq - hi.astype(F32)).astype(BF16)
    ms = (_dot(hi, same_group_bf16) + _dot(mid, same_group_bf16)) * (1.0 / group)
    return x * lax.rsqrt(ms + EPS) * gain


def _merge_ffn_kernel(x_ref, mod_ref, oa_ref, ob_ref, oc_ref, rc_ref, na_ref, nb_ref, nc_ref,
                      gb_ref, gc_ref, wo_ref, gain_ref, w13_ref, w2_ref, fin_ref, o_ref, acc_ref,
                      *, lam_init, final):
    a = _rms(oa_ref[...], na_ref[...])
    b = _group_rms(ob_ref[...], gb_ref[...], DV_B, nb_ref[...]) * (1.0 - lam_init)
    c = _group_rms(oc_ref[...], gc_ref[...], DV_C, nc_ref[...]) * _silu(rc_ref[...])
    mixed = (_dot(a.astype(BF16), wo_ref[0:WIDTH_A, :])
             + _dot(b.astype(BF16), wo_ref[WIDTH_A:WIDTH_A + WIDTH_B, :])
             + _dot(c.astype(BF16), wo_ref[WIDTH_A + WIDTH_B:, :]))
    x = x_ref[...] + mod_ref[0, 5:6, :] * mixed
    y = _ffn_block(x, mod_ref, gain_ref[...], w13_ref, w2_ref, acc_ref, 6)
    if final:
        y = _rms(y, fin_ref[...])
    o_ref[...] = y


def _merge_ffn(x, mod, tokens_per_group, oa, ob, oc, rc, lw, w13, w2, consts, fin, layer, lam_init, final):
    n = x.shape[0]
    row = lambda w: pl.BlockSpec((TOKEN_TILE, w), lambda i: (i, 0))
    return pl.pallas_call(
        functools.partial(_merge_ffn_kernel, lam_init=lam_init, final=final),
        grid=(n // TOKEN_TILE,),
        in_specs=[row(D_MODEL), _mod_spec(tokens_per_group), row(WIDTH_A), row(WIDTH_B), row(WIDTH_C),
                  row(WIDTH_C), _layer_spec((1, WIDTH_A), layer), _layer_spec((1, WIDTH_B), layer),
                  _layer_spec((1, WIDTH_C), layer), _const_spec((WIDTH_B, WIDTH_B)),
                  _const_spec((WIDTH_C, WIDTH_C)), _layer_spec((D_MODEL, D_MODEL), layer, 1),
                  _layer_spec((1, D_MODEL), layer), _weight_spec(w13, layer), _weight_spec(w2, layer),
                  _const_spec((1, D_MODEL))],
        out_specs=row(D_MODEL),
        out_shape=jax.ShapeDtypeStruct((n, D_MODEL), F32),
        scratch_shapes=[pltpu.VMEM((TOKEN_TILE, D_MODEL), F32)],
        compiler_params=_params(1),
    )(x, mod, oa, ob, oc, rc, lw["out_norm_a"], lw["out_norm_b"], lw["out_norm_c"],
      consts["group_b"], consts["group_c"], lw["wo"], lw["norm_ffn2"], w13, w2, fin)


def _prepare_weights(w):
    n_layers = w["w_in"].shape[0]
    row = lambda v: v.reshape(n_layers, 1, -1)

    def pad_heads(m, width):
        m = m.reshape(n_layers, m.shape[1], H_A, width)
        m = jnp.pad(m, ((0, 0), (0, 0), (0, 0), (0, HEAD_PAD - width)))
        return m.reshape(n_layers, m.shape[1], H_A * HEAD_PAD)

    def gate_block(wg, first):
        lo = 0 if first else QK_C_PAD
        return jnp.pad(wg, ((0, 0), (0, 0), (lo, 2 * QK_C_PAD - lo - H_C * DK_C)))

    wg = jnp.concatenate([gate_block(w["gla_wg_f"], True), gate_block(w["gla_wg_b"], False),
                          jnp.zeros((n_layers, LANE - 2 * GATE_RANK, 2 * QK_C_PAD), F32)], axis=1)
    bg = gate_block(w["gla_bg_f"][:, None, :], True) + gate_block(w["gla_bg_b"][:, None, :], False)
    return {
        "norm_ffn1": row(w["norm_ffn1"]), "norm_ffn2": row(w["norm_ffn2"]),
        "norm_mix": row(w["norm_mix"]), "w_in_t": jnp.swapaxes(w["w_in"], 1, 2),
        "q_norm": row(w["mla_q_norm"]),
        "wuq": pad_heads(w["mla_w_uq"], DN_A + DR_A).astype(BF16),
        "kv_norm": row(w["mla_kv_norm"]),
        "wuk": pad_heads(w["mla_w_uk"], DN_A).astype(BF16),
        "wuvt": jnp.swapaxes(w["mla_w_uv"], 1, 2).astype(BF16),
        "wg": wg.astype(BF16), "bg": bg,
        "lam": jnp.stack([w["diff_lq1"], w["diff_lk1"], w["diff_lq2"], w["diff_lk2"]], axis=1),
        "out_norm_a": row(w["mla_out_norm"]),
        "out_norm_b": row(jnp.tile(w["diff_norm"], (1, H_B))),
        "out_norm_c": row(jnp.tile(w["gla_norm"], (1, H_C))),
        "wo": w["w_out"].astype(BF16),
    }


def _rope_tables(n_pos):
    pos = jnp.arange(n_pos)
    rows = (pos // GRID_W).astype(F32)
    cols = (pos % GRID_W).astype(F32)
    half = DH_B // 2
    inv = ROPE_BASE ** (-jnp.arange(0, half, 2, dtype=F32) / half)
    ang = jnp.concatenate([rows[:, None] * inv, rows[:, None] * inv,
                           cols[:, None] * inv, cols[:, None] * inv], axis=1)
    sign = jnp.tile(jnp.concatenate([-jnp.ones(8, F32), jnp.ones(8, F32)]), 2)
    cos32, sin32 = jnp.cos(ang), jnp.sin(ang) * sign
    cos_full, sin_full = jnp.tile(cos32, (1, LANE // 32)), jnp.tile(sin32, (1, LANE // 32))
    ones = lambda n: jnp.ones((n_pos, n), F32)
    zeros = lambda n: jnp.zeros((n_pos, n), F32)
    cos_head = jnp.concatenate([ones(ROPE_OFF), cos32, ones(LANE - ROPE_OFF - DR_A)], axis=1)
    sin_head = jnp.concatenate([zeros(ROPE_OFF), sin32, zeros(LANE - ROPE_OFF - DR_A)], axis=1)
    return cos_head, sin_head, cos_full, sin_full


def _same_group_matrix(width, group):
    idx = jnp.arange(width) // group
    return jnp.where(idx[:, None] == idx[None, :], 1.0, 0.0).astype(BF16)


def kernel(x_prompt, x_sample, cache_mla_ckv, cache_mla_krope, cache_diff_k, cache_diff_v, state_gla_fwd, state_gla_bwd, c, c_ctx, w_mod, b_mod, norm_ffn1, ffn1_w13, ffn1_w2, norm_mix, w_in, mla_q_norm, mla_w_uq, mla_kv_norm, mla_w_uk, mla_w_uv, mla_out_norm, diff_lq1, diff_lk1, diff_lq2, diff_lk2, diff_norm, gla_wg_f, gla_bg_f, gla_wg_b, gla_bg_b, gla_norm, w_out, norm_ffn2, ffn2_w13, ffn2_w2, final_norm):
    w = dict(norm_ffn1=norm_ffn1, ffn1_w13=ffn1_w13, ffn1_w2=ffn1_w2, norm_mix=norm_mix, w_in=w_in,
             mla_q_norm=mla_q_norm, mla_w_uq=mla_w_uq, mla_kv_norm=mla_kv_norm, mla_w_uk=mla_w_uk,
             mla_w_uv=mla_w_uv, mla_out_norm=mla_out_norm, diff_lq1=diff_lq1, diff_lk1=diff_lk1,
             diff_lq2=diff_lq2, diff_lk2=diff_lk2, diff_norm=diff_norm, gla_wg_f=gla_wg_f,
             gla_bg_f=gla_bg_f, gla_wg_b=gla_wg_b, gla_bg_b=gla_bg_b, gla_norm=gla_norm, w_out=w_out,
             norm_ffn2=norm_ffn2, ffn2_w13=ffn2_w13, ffn2_w2=ffn2_w2)
    n_ctx_b, n_ctx_t, _ = x_prompt.shape
    n_lat_b, n_lat_t, _ = x_sample.shape
    n_past = cache_mla_ckv.shape[2]

    cvec = jnp.concatenate([c_ctx[None, :], c, jnp.zeros((MOD_ROWS - 1 - n_lat_b, D_MODEL), F32)], axis=0)
    mod = _modulation(cvec, w_mod, b_mod).reshape(DEPTH, MOD_ROWS, N_MOD, D_MODEL)

    rope_tabs = _rope_tables(n_lat_t)
    consts = {"group_b": _same_group_matrix(WIDTH_B, DV_B), "group_c": _same_group_matrix(WIDTH_C, DV_C)}
    cache_kr = jnp.pad(cache_mla_krope, ((0, 0), (0, 0), (0, 0), (ROPE_OFF, LANE - ROPE_OFF - DR_A)))
    cache_dk = cache_diff_k.reshape(n_lat_b, DEPTH, n_past, WIDTH_B)
    cache_dv = cache_diff_v.reshape(n_lat_b, DEPTH, n_past, WIDTH_B)
    fin = final_norm.reshape(1, D_MODEL)

    xp = x_prompt.reshape(n_ctx_b * n_ctx_t, D_MODEL)
    xs = x_sample.reshape(n_lat_b * n_lat_t, D_MODEL)
    new_kr = []
    new_cache = new_states = None
    lw = _prepare_weights(w)
    ffn1_w = (ffn1_w13[0].astype(BF16), ffn1_w2[0].astype(BF16))
    for l in range(DEPTH):
        lam_init = 0.8 - 0.6 * math.exp(-0.3 * l)
        last = l == DEPTH - 1
        mod_ctx, mod_lat = mod[l, 0:1], mod[l, 1:1 + n_lat_b]

        def trunk(x, m, tpg, n_batch, n_tok, rope, mla_cache, diff_cache, gla_init, is_ctx, ffn2_w):
            if is_ctx:
                cast = ((ffn2_w13, ffn2_w2), l)
            else:
                cast = None if last else ((ffn1_w13, ffn1_w2), l + 1)
            x, converted = _ffn(x, m, tpg, lw["norm_ffn1"], ffn1_w[0], ffn1_w[1], l, mod_row=0, cast=cast)
            if is_ctx:
                ffn2_w = converted
            p = _project(x, m, tpg, lw, lw["w_in_t"], l, rope, seq_len=n_tok if is_ctx else 0,
                         carried=new_cache if is_ctx else None)
            oa = _mla(p, lw, n_batch, n_tok, mla_cache, l)
            ob = _diff(p, lw, n_batch, n_tok, diff_cache, l, lam_init)
            gla_out = _gla(p, n_batch, n_tok, l, init=gla_init, emit_state=is_ctx,
                           carried=new_states if is_ctx else None)
            x = _merge_ffn(x, m, tpg, oa, ob, gla_out[0], p["rc"], lw, ffn2_w[0], ffn2_w[1], consts, fin, l,
                           lam_init, last)
            return x, p, gla_out, converted

        xp, p, gla_out, ffn2_w = trunk(xp, mod_ctx, n_ctx_b * n_ctx_t, n_ctx_b, n_ctx_t, None, None, None,
                                       None, True, None)
        new_cache = p["cache"]
        new_states = tuple(gla_out[1:])
        new_kr.append(p["kr"][:, ROPE_OFF:ROPE_OFF + DR_A].reshape(n_ctx_b, n_ctx_t, DR_A))

        xs, _, _, next_ffn1_w = trunk(xs, mod_lat, n_lat_t, n_lat_b, n_lat_t, rope_tabs,
                                      (cache_mla_ckv, cache_kr), (cache_dk, cache_dv),
                                      (state_gla_fwd, state_gla_bwd), False, ffn2_w)
        ffn1_w = next_ffn1_w

    new_ckv, new_dkt, new_dvt = new_cache

    def token_major(t, dh):
        return jnp.transpose(t.reshape(n_ctx_b, DEPTH, H_B, dh, n_ctx_t), (0, 1, 4, 2, 3))

    return (xp.reshape(x_prompt.shape), xs.reshape(x_sample.shape), new_ckv, jnp.stack(new_kr, axis=1),
            token_major(new_dkt, 2 * DH_B), token_major(new_dvt, DV_B), new_states[0], new_states[1])
```

```python
import functools
import math

import jax
import jax.numpy as jnp
from jax import lax
from jax.experimental import pallas as pl
from jax.experimental.pallas import tpu as pltpu

F32 = jnp.float32
BF16 = jnp.bfloat16

D_MODEL = 1024
DEPTH = 2
GRID_W = 64
ROPE_BASE = 10000.0
EPS = 1e-6
H_A, DN_A, DR_A, DV_A = 6, 64, 32, 64
Q_RANK, KV_RANK = 384, 256
H_B, DH_B, DV_B = 4, 32, 64
H_C, DK_C, DV_C = 4, 48, 96
GATE_RANK = 16
GATE_NORM = 16.0
CHUNK = 64
WIDTH_A = H_A * DV_A
WIDTH_B = H_B * DV_B
WIDTH_C = H_C * DV_C
D_FF = 2816
N_MOD = 9

LANE = 128
FF_CHUNK = 256
TOKEN_TILE = 512
FFN_TILE = 512
CAST_BLOCKS = 16
Q_TILE = 1024
MOD_ROWS = 16
ONES_ROWS = 16
MOD_TILE = 2304
GLA_UNROLL = 4
ATTN_SEQS = 4
GLA_SEQS = 2
GLA_ROWS = 1024
VMEM_LIMIT = 56 * 1024 * 1024

HEAD_PAD = LANE
ROPE_OFF = DN_A
QK_C_PAD = 256
SEG = {}
_off = 0
for _name, _w in (("cq", Q_RANK), ("krope", LANE), ("ckv", KV_RANK), ("qb", WIDTH_B), ("kb", WIDTH_B),
                  ("vb", WIDTH_B), ("qc", QK_C_PAD), ("kc", QK_C_PAD), ("vc", WIDTH_C), ("gl", LANE),
                  ("rc", WIDTH_C)):
    SEG[_name] = (_off, _off + _w)
    _off += _w
PROJ_PAD = _off

PROJ_SPLITS = (Q_RANK, KV_RANK, DR_A, 2 * H_B * DH_B, 2 * H_B * DH_B, H_B * DV_B, H_C * DK_C, H_C * DK_C,
               WIDTH_C, WIDTH_C, GATE_RANK, GATE_RANK)
PROJ_COLS = sum(PROJ_SPLITS)
_SRC = [sum(PROJ_SPLITS[:i]) for i in range(len(PROJ_SPLITS) + 1)]


def _src(i, j=None):
    return ("src", _SRC[i], _SRC[(i if j is None else j) + 1])


PROJ_LAYOUT = (
    ("cq", (_src(0),)), ("ckv", (_src(1),)),
    ("krope", (("zero", 0, ROPE_OFF), _src(2), ("zero", 0, LANE - ROPE_OFF - DR_A))),
    ("qb", (_src(3),)), ("kb", (_src(4),)), ("vb", (_src(5),)),
    ("qc", (_src(6), ("zero", 0, QK_C_PAD - H_C * DK_C))),
    ("kc", (_src(7), ("zero", 0, QK_C_PAD - H_C * DK_C))),
    ("vc", (_src(8),)), ("rc", (_src(9),)),
    ("gl", (_src(10, 11), ("zero", 0, LANE - 2 * GATE_RANK))),
)


def _dot(a, b):
    return jnp.dot(a, b, preferred_element_type=F32)


def _dot_nt(a, b):
    return lax.dot_general(a, b, (((1,), (1,)), ((), ())), preferred_element_type=F32)


def _dot_tn(a, b):
    return lax.dot_general(a, b, (((0,), (0,)), ((), ())), preferred_element_type=F32)


def _rms(x, gain):
    ms = jnp.mean(x * x, axis=-1, keepdims=True)
    return (x * lax.rsqrt(ms + EPS)) * gain


def _silu(x):
    return x * jax.nn.sigmoid(x)


def _const_spec(shape):
    nd = len(shape)
    return pl.BlockSpec(shape, lambda *_: (0,) * nd)


def _layer_spec(shape, layer, buffers=None):
    nd = len(shape)
    mode = {} if buffers is None else {"pipeline_mode": pl.Buffered(buffers)}
    return pl.BlockSpec((None,) + tuple(shape), lambda *_: (layer,) + (0,) * nd, **mode)


def _params(n_axes):
    return pltpu.CompilerParams(dimension_semantics=("arbitrary",) * n_axes,
                                vmem_limit_bytes=VMEM_LIMIT)


def _mod_kernel(c_ref, w_ref, b_ref, o_ref):
    s = _silu(c_ref[...]).astype(BF16)
    o_ref[0] = _dot(s, w_ref[0].astype(BF16)) + b_ref[0]


def _modulation(cvec, w_mod, b_mod):
    n_rows = cvec.shape[0]
    width = N_MOD * D_MODEL
    return pl.pallas_call(
        _mod_kernel,
        grid=(DEPTH, width // MOD_TILE),
        in_specs=[
            pl.BlockSpec((n_rows, D_MODEL), lambda l, j: (0, 0)),
            pl.BlockSpec((1, D_MODEL, MOD_TILE), lambda l, j: (l, 0, j)),
            pl.BlockSpec((1, 1, MOD_TILE), lambda l, j: (l, 0, j)),
        ],
        out_specs=pl.BlockSpec((1, n_rows, MOD_TILE), lambda l, j: (l, 0, j)),
        out_shape=jax.ShapeDtypeStruct((DEPTH, n_rows, width), F32),
        compiler_params=_params(2),
    )(cvec, w_mod, b_mod.reshape(DEPTH, 1, width))


def _all_layer_spec(lead, shape, layer, first):
    zeros = (0,) * len(shape)
    if first:
        return pl.BlockSpec((lead, DEPTH) + tuple(shape), lambda i: (i, 0) + zeros)
    return pl.BlockSpec((lead, None) + tuple(shape), lambda i: (i, layer) + zeros)


def _put_layer(ref, lead, tail, layer, first, val):
    if not first:
        ref[lead + tail] = val
        return
    for l in range(DEPTH):
        ref[lead + (l,) + tail] = val if l == layer else jnp.zeros_like(val)


def _mod_spec(tokens_per_group, tile=TOKEN_TILE):
    return pl.BlockSpec((1, N_MOD, D_MODEL), lambda i: ((i * tile) // tokens_per_group, 0, 0))


def _ada_norm(x, gain, mod_ref, first_row):
    shift = mod_ref[0, first_row:first_row + 1, :]
    scale = mod_ref[0, first_row + 1:first_row + 2, :]
    return _rms(x, gain) * (1.0 + scale) + shift


def _ffn_block(x, mod_ref, gain, w13_ref, w2_ref, acc_ref, mod_row):
    u = _ada_norm(x, gain, mod_ref, mod_row).astype(BF16)
    for c in range(D_FF // FF_CHUNK):
        lo, hi = c * FF_CHUNK, (c + 1) * FF_CHUNK
        a = _dot(u, w13_ref[:, lo:hi])
        b = _dot(u, w13_ref[:, D_FF + lo:D_FF + hi])
        t = _dot((_silu(a) * b).astype(BF16), w2_ref[lo:hi, :])
        if c == 0:
            acc_ref[...] = t
        else:
            acc_ref[...] += t
    gate = mod_ref[0, mod_row + 2:mod_row + 3, :]
    return x + (0.5 * gate) * acc_ref[...]


def _ffn_kernel(*refs, mod_row, n_cast, steps_a):
    xa_ref, xb_ref, moda_ref, modb_ref, gain_ref, w13_ref, w2_ref = refs[:7]
    cast_in = refs[7:7 + n_cast]
    oa_ref, ob_ref = refs[7 + n_cast:9 + n_cast]
    cast_out = refs[9 + n_cast:9 + 2 * n_cast]
    acc_ref = refs[9 + 2 * n_cast]
    for src, dst in zip(cast_in, cast_out):
        dst[...] = src[...].astype(BF16)

    @pl.when(pl.program_id(0) < steps_a)
    def _():
        oa_ref[...] = _ffn_block(xa_ref[...], moda_ref, gain_ref[...], w13_ref, w2_ref, acc_ref, mod_row)

    @pl.when(pl.program_id(0) >= steps_a)
    def _():
        ob_ref[...] = _ffn_block(xb_ref[...], modb_ref, gain_ref[...], w13_ref, w2_ref, acc_ref, mod_row)


def _weight_spec(arr, layer):
    if arr.ndim == 2:
        return pl.BlockSpec(arr.shape, lambda *_: (0, 0), pipeline_mode=pl.Buffered(1))
    return _layer_spec(arr.shape[1:], layer, 1)


def _ffn(xa, moda, tpg_a, xb, modb, tpg_b, gain, w13, w2, layer, *, mod_row, casts=()):
    steps_a, steps_b = xa.shape[0] // FFN_TILE, xb.shape[0] // FFN_TILE
    steps = steps_a + steps_b
    in_a = lambda i: jnp.minimum(i, steps_a - 1)
    in_b = lambda i: jnp.maximum(i - steps_a, 0)
    tile_a = pl.BlockSpec((FFN_TILE, D_MODEL), lambda i: (in_a(i), 0))
    tile_b = pl.BlockSpec((FFN_TILE, D_MODEL), lambda i: (in_b(i), 0))
    mod_a = pl.BlockSpec((1, N_MOD, D_MODEL), lambda i: ((in_a(i) * FFN_TILE) // tpg_a, 0, 0))
    mod_b = pl.BlockSpec((1, N_MOD, D_MODEL), lambda i: ((in_b(i) * FFN_TILE) // tpg_b, 0, 0))
    in_specs = [tile_a, tile_b, mod_a, mod_b, _layer_spec((1, D_MODEL), layer),
                _weight_spec(w13, layer), _weight_spec(w2, layer)]
    args = [xa, xb, moda, modb, gain, w13, w2]
    out_specs = [tile_a, tile_b]
    out_shape = [jax.ShapeDtypeStruct(xa.shape, F32), jax.ShapeDtypeStruct(xb.shape, F32)]
    for src, cast_layer in casts:
        rows, cols = src.shape[1] // CAST_BLOCKS, src.shape[2]
        blk = lambda i: (i * CAST_BLOCKS) // steps
        in_specs.append(pl.BlockSpec((None, rows, cols),
                                     lambda i, cast_layer=cast_layer: (cast_layer, blk(i), 0)))
        out_specs.append(pl.BlockSpec((rows, cols), lambda i: (blk(i), 0)))
        out_shape.append(jax.ShapeDtypeStruct(src.shape[1:], BF16))
        args.append(src)
    outs = pl.pallas_call(
        functools.partial(_ffn_kernel, mod_row=mod_row, n_cast=len(casts), steps_a=steps_a),
        grid=(steps,),
        in_specs=in_specs,
        out_specs=out_specs,
        out_shape=out_shape,
        scratch_shapes=[pltpu.VMEM((FFN_TILE, D_MODEL), F32)],
        compiler_params=_params(1),
    )(*args)
    return outs[0], outs[1], tuple(outs[2:])


def _rope(x, cos, sin_signed):
    first = (lax.broadcasted_iota(jnp.int32, (x.shape[0], LANE), 1) % 16) < 8
    outs = []
    for j in range(x.shape[1] // LANE):
        xj = x[:, j * LANE:(j + 1) * LANE]
        partner = jnp.where(first, pltpu.roll(xj, LANE - 8, 1), pltpu.roll(xj, 8, 1))
        outs.append(xj * cos + partner * sin_signed)
    return outs[0] if len(outs) == 1 else jnp.concatenate(outs, axis=1)


def _log_sigmoid(z):
    return jnp.minimum(z, 0.0) - jnp.log1p(jnp.exp(-jnp.abs(z)))


def _proj_kernel(*refs, rope, n_carried, seq_len, layer):
    (x_ref, mod_ref, gain_ref, wint_ref, qn_ref, wuq_ref, kvn_ref, wg_ref, bg_ref) = refs[:9]
    refs = refs[9:]
    if rope:
        ch_ref, sh_ref, cf_ref, sf_ref = refs[:4]
        refs = refs[4:]
    refs = refs[n_carried:]
    (q_o, ckv_o, kr_o, qb_o, kb_o, vb_o, qc_o, kc_o, vc_o, rc_o, gf_o, gb_o) = refs[:12]
    refs = refs[12:]
    if seq_len:
        ckv_all_o, kbt_all_o, vbt_all_o = refs[:3]
        refs = refs[3:]
    wpt_ref, = refs

    @pl.when(pl.program_id(0) == 0)
    def _():
        for name, pieces in PROJ_LAYOUT:
            parts = []
            for kind, a, b in pieces:
                parts.append(jnp.zeros((b - a, D_MODEL), F32) if kind == "zero" else wint_ref[a:b, :])
            blk = parts[0] if len(parts) == 1 else jnp.concatenate(parts, axis=0)
            lo, hi = SEG[name]
            wpt_ref[lo:hi, :] = blk.astype(BF16)

    u = _ada_norm(x_ref[...], gain_ref[...], mod_ref, 3).astype(BF16)

    def seg(first, last=None):
        lo, hi = SEG[first][0], SEG[last or first][1]
        return _dot_nt(u, wpt_ref[lo:hi, :])

    cq_kr = seg("cq", "krope")
    q = _dot(_rms(cq_kr[:, :Q_RANK], qn_ref[...]).astype(BF16), wuq_ref[...])
    kr = cq_kr[:, Q_RANK:]
    qb = seg("qb")
    if seq_len:
        kbt = _dot_nt(wpt_ref[SEG["kb"][0]:SEG["kb"][1], :], u)
        vbt = _dot_nt(wpt_ref[SEG["vb"][0]:SEG["vb"][1], :], u)
        kb, vb = kbt.T, vbt.T
    else:
        kb, vb = seg("kb"), seg("vb")
    if rope:
        q = _rope(q, ch_ref[...], sh_ref[...])
        kr = _rope(kr, ch_ref[...], sh_ref[...])
        qb = _rope(qb, cf_ref[...], sf_ref[...])
        kb = _rope(kb, cf_ref[...], sf_ref[...])
    q_o[...] = (q * (DN_A + DR_A) ** -0.5).astype(BF16)
    ckv = _rms(seg("ckv"), kvn_ref[...])
    ckv_o[...] = ckv
    if seq_len:
        for j in range(u.shape[0] // seq_len):
            tok = slice(j * seq_len, (j + 1) * seq_len)
            _put_layer(ckv_all_o, (j,), (), layer, n_carried == 0, ckv[tok, :])
            _put_layer(kbt_all_o, (j,), (), layer, n_carried == 0, kbt[:, tok])
            _put_layer(vbt_all_o, (j,), (), layer, n_carried == 0, vbt[:, tok])
    kr_o[...] = kr
    qb_o[...] = (qb * DH_B ** -0.5).astype(BF16)
    kb_o[...] = kb
    vb_o[...] = vb
    qc_o[...] = seg("qc") * (DK_C ** -0.5)
    kc_o[...] = seg("kc")
    vc_gl = seg("vc", "gl")
    vc_o[...] = vc_gl[:, :WIDTH_C]
    rc_o[...] = seg("rc")
    z = _dot(vc_gl[:, WIDTH_C:].astype(BF16), wg_ref[...]) + bg_ref[...]
    g = _log_sigmoid(z) / GATE_NORM
    gf_o[...] = g[:, :QK_C_PAD]
    gb_o[...] = g[:, QK_C_PAD:]


PROJ_OUT = (("q", H_A * HEAD_PAD, BF16), ("ckv", KV_RANK, F32), ("kr", LANE, F32), ("qb", WIDTH_B, BF16),
            ("kb", WIDTH_B, F32), ("vb", WIDTH_B, F32), ("qc", QK_C_PAD, F32), ("kc", QK_C_PAD, F32),
            ("vc", WIDTH_C, F32), ("rc", WIDTH_C, F32), ("gf", QK_C_PAD, F32), ("gb", QK_C_PAD, F32))


def _project(x, mod, tokens_per_group, lw, w_in, layer, rope_tabs, seq_len=0, carried=None):
    n = x.shape[0]
    rope = rope_tabs is not None
    row = lambda w: pl.BlockSpec((TOKEN_TILE, w), lambda i: (i, 0))
    in_specs = [row(D_MODEL), _mod_spec(tokens_per_group), _layer_spec((1, D_MODEL), layer),
                _layer_spec((PROJ_COLS, D_MODEL), layer, 1), _layer_spec((1, Q_RANK), layer),
                _layer_spec((Q_RANK, H_A * HEAD_PAD), layer), _layer_spec((1, KV_RANK), layer),
                _layer_spec((LANE, 2 * QK_C_PAD), layer), _layer_spec((1, 2 * QK_C_PAD), layer)]
    args = [x, mod, lw["norm_mix"], w_in, lw["q_norm"], lw["wuq"], lw["kv_norm"], lw["wg"], lw["bg"]]
    if rope:
        n_pos = rope_tabs[0].shape[0]
        tab = pl.BlockSpec((TOKEN_TILE, LANE), lambda i: (i % (n_pos // TOKEN_TILE), 0))
        in_specs += [tab] * 4
        args += list(rope_tabs)
    out_specs = [row(w) for _, w, _ in PROJ_OUT]
    out_shape = [jax.ShapeDtypeStruct((n, w), dt) for _, w, dt in PROJ_OUT]
    aliases = {}
    if seq_len:
        per_tile = TOKEN_TILE // seq_len
        if carried is not None:
            for k in range(len(carried)):
                aliases[len(args) + k] = len(out_specs) + k
            in_specs += [pl.BlockSpec(memory_space=pl.ANY)] * len(carried)
            args += list(carried)
        for shape in ((seq_len, KV_RANK), (WIDTH_B, seq_len), (WIDTH_B, seq_len)):
            out_specs.append(_all_layer_spec(per_tile, shape, layer, carried is None))
            out_shape.append(jax.ShapeDtypeStruct((n // seq_len, DEPTH) + shape, F32))
    outs = pl.pallas_call(
        functools.partial(_proj_kernel, rope=rope, n_carried=0 if carried is None else len(carried),
                          seq_len=seq_len, layer=layer),
        grid=(n // TOKEN_TILE,),
        in_specs=in_specs,
        out_specs=out_specs,
        out_shape=out_shape,
        input_output_aliases=aliases,
        scratch_shapes=[pltpu.VMEM((PROJ_PAD, D_MODEL), BF16)],
        compiler_params=_params(1),
    )(*args)
    result = {name: o for (name, _, _), o in zip(PROJ_OUT, outs)}
    if seq_len:
        result["cache"] = tuple(outs[len(PROJ_OUT):])
    return result


def _lane_mask(width, lo, hi):
    lane = lax.broadcasted_iota(jnp.int32, (1, width), 1)
    return jnp.where((lane >= lo) & (lane < hi), 1.0, 0.0).astype(F32)


def _attend_t(units, dv):
    outs = []
    q, keys, vt_ext = units[0]()
    s = _dot_nt(q, keys)
    for u in range(len(units)):
        s_now, vt_now = s, vt_ext
        if u + 1 < len(units):
            q, keys, vt_ext = units[u + 1]()
            s = _dot_nt(q, keys)
        e = jnp.exp((s_now - jnp.max(s_now, axis=-1, keepdims=True)).astype(BF16))
        r = _dot_nt(vt_now, e)
        outs.append(r[0:dv] * (1.0 / r[dv:dv + 1]))
    return outs


def _attn_seqs(n_cache, n_q_tiles):
    return ATTN_SEQS if (n_cache == 0 and n_q_tiles == 1) else 1


def _mla_kernel(*refs, n_cache, n_seq, n_tok):
    if n_cache:
        q_ref, ckv_ref, kr_ref, cckv_ref, ckr_ref, wuk_ref, wuvt_ref, o_ref, kf_scr, vt_scr = refs
    else:
        q_ref, ckv_ref, kr_ref, wuk_ref, wuvt_ref, o_ref, kf_scr, vt_scr = refs

    @pl.when(pl.program_id(1) == 0)
    def _():
        def fill(s, row0, ckv, kr):
            rows = ckv.shape[0]
            cb = ckv.astype(BF16)
            kr_all = jnp.concatenate([kr] * H_A, axis=1)
            kf_scr[s, row0:row0 + rows, :] = (_dot(cb, wuk_ref[...]) + kr_all).astype(BF16)
            vt = _dot_nt(wuvt_ref[...], cb)
            for h in range(H_A):
                vt_scr[s * H_A + h, 0:DV_A, row0:row0 + rows] = vt[h * DV_A:(h + 1) * DV_A].astype(BF16)

        for s in range(n_seq):
            if n_cache:
                fill(s, 0, cckv_ref[...], ckr_ref[...])
            tok = slice(s * n_tok, (s + 1) * n_tok)
            fill(s, n_cache, ckv_ref[tok, :], kr_ref[tok, :])
        for u in range(n_seq * H_A):
            vt_scr[u, DV_A:, :] = jnp.ones((ONES_ROWS, vt_scr.shape[2]), BF16)

    n_q = q_ref.shape[0] // n_seq

    def unit(s, h):
        sl = slice(h * HEAD_PAD, (h + 1) * HEAD_PAD)
        return lambda: (q_ref[s * n_q:(s + 1) * n_q, sl], kf_scr[s, :, sl], vt_scr[s * H_A + h])

    heads = _attend_t([unit(s, h) for h in range(H_A) for s in range(n_seq)], DV_A)
    for s in range(n_seq):
        o_ref[s * n_q:(s + 1) * n_q, :] = jnp.concatenate(heads[s::n_seq], axis=0).T


def _mla(p, lw, n_batch, n_tok, cache, layer):
    n_cache = 0 if cache is None else cache[0].shape[2]
    q_tile = min(Q_TILE, n_tok)
    nqt = n_tok // q_tile
    n_seq = _attn_seqs(n_cache, nqt)
    in_specs = [pl.BlockSpec((n_seq * q_tile, H_A * HEAD_PAD), lambda b, i: (b * nqt + i, 0)),
                pl.BlockSpec((n_seq * n_tok, KV_RANK), lambda b, i: (b, 0)),
                pl.BlockSpec((n_seq * n_tok, LANE), lambda b, i: (b, 0))]
    args = [p["q"], p["ckv"], p["kr"]]
    if n_cache:
        in_specs += [pl.BlockSpec((None, None, n_cache, KV_RANK), lambda b, i: (b, layer, 0, 0)),
                     pl.BlockSpec((None, None, n_cache, LANE), lambda b, i: (b, layer, 0, 0))]
        args += list(cache)
    in_specs += [_layer_spec((KV_RANK, H_A * HEAD_PAD), layer), _layer_spec((WIDTH_A, KV_RANK), layer)]
    args += [lw["wuk"], lw["wuvt"]]
    n_keys = n_cache + n_tok
    return pl.pallas_call(
        functools.partial(_mla_kernel, n_cache=n_cache, n_seq=n_seq, n_tok=n_tok),
        grid=(n_batch // n_seq, nqt),
        in_specs=in_specs,
        out_specs=pl.BlockSpec((n_seq * q_tile, WIDTH_A), lambda b, i: (b * nqt + i, 0)),
        out_shape=jax.ShapeDtypeStruct((n_batch * n_tok, WIDTH_A), F32),
        scratch_shapes=[pltpu.VMEM((n_seq, n_keys, H_A * HEAD_PAD), BF16),
                        pltpu.VMEM((n_seq * H_A, DV_A + ONES_ROWS, n_keys), BF16)],
        compiler_params=_params(2),
    )(*args)


def _diff_kernel(*refs, n_cache, n_seq, n_tok, lam_init):
    if n_cache:
        q_ref, k_ref, v_ref, ck_ref, cv_ref, lam_ref, o_ref, k_scr, vt_scr = refs
    else:
        q_ref, k_ref, v_ref, lam_ref, o_ref, k_scr, vt_scr = refs

    @pl.when(pl.program_id(1) == 0)
    def _():
        def fill(s, row0, k, v):
            rows = k.shape[0]
            k_scr[s, row0:row0 + rows, :] = k.astype(BF16)
            vt = v.T
            for h in range(H_B):
                vt_scr[s * H_B + h, 0:DV_B, row0:row0 + rows] = vt[h * DV_B:(h + 1) * DV_B].astype(BF16)

        for s in range(n_seq):
            if n_cache:
                fill(s, 0, ck_ref[...], cv_ref[...])
            tok = slice(s * n_tok, (s + 1) * n_tok)
            fill(s, n_cache, k_ref[tok, :], v_ref[tok, :])
        for u in range(n_seq * H_B):
            vt_scr[u, DV_B:, :] = jnp.ones((ONES_ROWS, vt_scr.shape[2]), BF16)

    lv = lam_ref[...]
    lam = (jnp.exp(jnp.sum(lv[0:1] * lv[1:2], axis=-1, keepdims=True))
           - jnp.exp(jnp.sum(lv[2:3] * lv[3:4], axis=-1, keepdims=True)) + lam_init)
    n_q = q_ref.shape[0] // n_seq
    lane = lax.broadcasted_iota(jnp.int32, (n_q, LANE), 1)

    def unit(s, h, j):
        blk = slice((h // 2) * LANE, (h // 2 + 1) * LANE)
        lo = (h % 2) * 2 * DH_B + j * DH_B

        def fn():
            qh = q_ref[s * n_q:(s + 1) * n_q, blk]
            qm = jnp.where((lane >= lo) & (lane < lo + DH_B), qh, jnp.zeros_like(qh))
            return qm, k_scr[s, :, blk], vt_scr[s * H_B + h]
        return fn

    maps = _attend_t([unit(s, h, j) for h in range(H_B) for j in range(2) for s in range(n_seq)], DV_B)
    for s in range(n_seq):
        mine = maps[s::n_seq]
        heads = [mine[2 * h] - lam * mine[2 * h + 1] for h in range(H_B)]
        o_ref[s * n_q:(s + 1) * n_q, :] = jnp.concatenate(heads, axis=0).T


def _diff(p, lw, n_batch, n_tok, cache, layer, lam_init):
    n_cache = 0 if cache is None else cache[0].shape[2]
    q_tile = min(Q_TILE, n_tok)
    nqt = n_tok // q_tile
    n_seq = _attn_seqs(n_cache, nqt)
    in_specs = [pl.BlockSpec((n_seq * q_tile, WIDTH_B), lambda b, i: (b * nqt + i, 0)),
                pl.BlockSpec((n_seq * n_tok, WIDTH_B), lambda b, i: (b, 0)),
                pl.BlockSpec((n_seq * n_tok, WIDTH_B), lambda b, i: (b, 0))]
    args = [p["qb"], p["kb"], p["vb"]]
    if n_cache:
        in_specs += [pl.BlockSpec((None, None, n_cache, WIDTH_B), lambda b, i: (b, layer, 0, 0))] * 2
        args += list(cache)
    in_specs.append(_layer_spec((4, DH_B), layer))
    args.append(lw["lam"])
    n_keys = n_cache + n_tok
    return pl.pallas_call(
        functools.partial(_diff_kernel, n_cache=n_cache, n_seq=n_seq, n_tok=n_tok, lam_init=lam_init),
        grid=(n_batch // n_seq, nqt),
        in_specs=in_specs,
        out_specs=pl.BlockSpec((n_seq * q_tile, WIDTH_B), lambda b, i: (b * nqt + i, 0)),
        out_shape=jax.ShapeDtypeStruct((n_batch * n_tok, WIDTH_B), F32),
        scratch_shapes=[pltpu.VMEM((n_seq, n_keys, WIDTH_B), BF16),
                        pltpu.VMEM((n_seq * H_B, DV_B + ONES_ROWS, n_keys), BF16)],
        compiler_params=_params(2),
    )(*args)


def _split3(x):
    hi = x.astype(BF16)
    r1 = x - hi.astype(F32)
    mid = r1.astype(BF16)
    lo = (r1 - mid.astype(F32)).astype(BF16)
    return hi, mid, lo


def _dot_exact_lhs(a_bf16, x):
    hi, mid, lo = _split3(x)
    return _dot(a_bf16, hi) + _dot(a_bf16, mid) + _dot(a_bf16, lo)


def _gla_kernel(*refs, n_tok, n_seq, has_init, emit_state, n_carried, layer):
    q_ref, k_ref, v_ref, gf_ref, gb_ref = refs[:5]
    refs = refs[5:]
    if has_init:
        s0f_ref, s0b_ref = refs[:2]
        refs = refs[2:]
    refs = refs[n_carried:]
    o_ref = refs[0]
    refs = refs[1:]
    if emit_state:
        sf_ref, sb_ref = refs[:2]
        refs = refs[2:]
    st_scr, bd_scr = refs


    n_chunks = n_tok // CHUNK
    n_groups = n_chunks // GLA_UNROLL
    t_idx = lax.broadcasted_iota(jnp.int32, (CHUNK, CHUNK), 0)
    s_idx = lax.broadcasted_iota(jnp.int32, (CHUNK, CHUNK), 1)
    tri = [jnp.where(s_idx <= t_idx, 1.0, 0.0).astype(F32), jnp.where(s_idx >= t_idx, 1.0, 0.0).astype(F32)]
    tri4 = [jnp.concatenate([t] * H_C, axis=1) for t in tri]
    tri_b = [t.astype(BF16) for t in tri]
    head_k = [_lane_mask(QK_C_PAD, h * DK_C, (h + 1) * DK_C).astype(BF16) for h in range(H_C)]
    head_v = [_lane_mask(WIDTH_C, h * DV_C, (h + 1) * DV_C).astype(BF16) for h in range(H_C)]
    row_v = lax.broadcasted_iota(jnp.int32, (WIDTH_C, QK_C_PAD), 0)
    col_k = lax.broadcasted_iota(jnp.int32, (WIDTH_C, QK_C_PAD), 1)
    diag = jnp.zeros((WIDTH_C, QK_C_PAD), F32)
    for h in range(H_C):
        inside = ((row_v >= h * DV_C) & (row_v < (h + 1) * DV_C)
                  & (col_k >= h * DK_C) & (col_k < (h + 1) * DK_C))
        diag = jnp.where(inside, 1.0, diag)

    for s in range(n_seq):
        for d, s0_ref in enumerate((s0f_ref, s0b_ref) if has_init else (None, None)):
            if s0_ref is None:
                st_scr[s, d] = jnp.zeros((WIDTH_C, QK_C_PAD), F32)
            else:
                bd_scr[...] = jnp.zeros_like(bd_scr)
                for h in range(H_C):
                    bd_scr[h * DK_C:(h + 1) * DK_C, h * DV_C:(h + 1) * DV_C] = s0_ref[s, h]
                st_scr[s, d] = bd_scr[...].T
    o_ref[...] = jnp.zeros_like(o_ref)

    def chunk_steps(chains):
        loaded = []
        for s, d, c in chains:
            g_ref = gb_ref if d else gf_ref
            row0 = s * n_tok + c * CHUNK
            rows = pl.ds(row0 if isinstance(row0, int) else pl.multiple_of(row0, CHUNK), CHUNK)
            bcum = _dot_exact_lhs(tri_b[d], g_ref[rows, :])
            loaded.append((rows, q_ref[rows, :], k_ref[rows, :], v_ref[rows, :], bcum))
        scaled = []
        for (s, d, c), (rows, q, k, v, bcum) in zip(chains, loaded):
            blast = bcum[0:1, :] if d else bcum[CHUNK - 1:CHUNK, :]
            qe = (q * jnp.exp(bcum)).astype(BF16)
            ke = (k * jnp.exp(-bcum)).astype(BF16)
            kd = (k * jnp.exp(blast - bcum)).astype(BF16)
            vb = v.astype(BF16)
            ke_heads = jnp.concatenate([ke * head_k[h] for h in range(H_C)], axis=0)
            v_heads = jnp.concatenate([vb * head_v[h] for h in range(H_C)], axis=0)
            scaled.append((qe, ke_heads, kd, vb, v_heads, jnp.exp(blast)))
        products = []
        for (s, d, c), (qe, ke_heads, kd, vb, v_heads, decay) in zip(chains, scaled):
            st = st_scr[s, d]
            attn = _dot_nt(qe, ke_heads)
            inter = _dot_nt(qe, st.astype(BF16))
            upd = _dot_tn(vb, kd)
            products.append((st, attn, inter, upd))
        masked = []
        for (s, d, c), (qe, ke_heads, kd, vb, v_heads, decay), (st, attn, inter, upd) in zip(
                chains, scaled, products):
            st_scr[s, d] = st * decay + upd * diag
            masked.append((attn * tri4[d]).astype(BF16))
        for (rows, *_), (_, _, _, _, v_heads, _), (_, _, inter, _), attn_b in zip(
                loaded, scaled, products, masked):
            o_ref[rows, :] += _dot(attn_b, v_heads) + inter

    def group(i):
        for j in range(GLA_UNROLL):
            fwd = i * GLA_UNROLL + j
            chunk_steps([(s, d, n_chunks - 1 - fwd if d else fwd) for s in range(n_seq) for d in (0, 1)])

    if n_groups == 1:
        group(0)
    else:
        def body(i, carry):
            group(i)
            return carry
        lax.fori_loop(0, n_groups, body, 0)

    if emit_state:
        for s in range(n_seq):
            for d, s_ref in enumerate((sf_ref, sb_ref)):
                bd_scr[...] = st_scr[s, d].T
                for h in range(H_C):
                    _put_layer(s_ref, (s,), (h,), layer, n_carried == 0,
                               bd_scr[h * DK_C:(h + 1) * DK_C, h * DV_C:(h + 1) * DV_C])


def _gla(p, n_batch, n_tok, layer, init=None, emit_state=False, carried=None):
    n_seq = max(GLA_SEQS, GLA_ROWS // n_tok)
    tok = lambda w: pl.BlockSpec((n_seq * n_tok, w), lambda b: (b, 0))
    st_spec = pl.BlockSpec((n_seq, None, H_C, DK_C, DV_C), lambda b: (b, layer, 0, 0, 0))
    in_specs = [tok(QK_C_PAD), tok(QK_C_PAD), tok(WIDTH_C), tok(QK_C_PAD), tok(QK_C_PAD)]
    args = [p["qc"], p["kc"], p["vc"], p["gf"], p["gb"]]
    if init is not None:
        in_specs += [st_spec] * 2
        args += list(init)
    out_specs = [tok(WIDTH_C)]
    out_shape = [jax.ShapeDtypeStruct((n_batch * n_tok, WIDTH_C), F32)]
    aliases = {}
    if emit_state:
        if carried is not None:
            for k in range(len(carried)):
                aliases[len(args) + k] = len(out_specs) + k
            in_specs += [pl.BlockSpec(memory_space=pl.ANY)] * len(carried)
            args += list(carried)
        out_specs += [_all_layer_spec(n_seq, (H_C, DK_C, DV_C), layer, carried is None)] * 2
        out_shape += [jax.ShapeDtypeStruct((n_batch, DEPTH, H_C, DK_C, DV_C), F32)] * 2
    return pl.pallas_call(
        functools.partial(_gla_kernel, n_tok=n_tok, n_seq=n_seq, has_init=init is not None,
                          emit_state=emit_state, n_carried=0 if carried is None else len(carried), layer=layer),
        grid=(n_batch // n_seq,),
        in_specs=in_specs,
        out_specs=out_specs,
        out_shape=out_shape,
        input_output_aliases=aliases,
        scratch_shapes=[pltpu.VMEM((n_seq, 2, WIDTH_C, QK_C_PAD), F32), pltpu.VMEM((QK_C_PAD, WIDTH_C), F32)],
        compiler_params=_params(1),
    )(*args)


def _group_rms(x, same_group_bf16, group, gain):
    sq = x * x
    hi = sq.astype(BF16)
    mid = (sq - hi.astype(F32)).astype(BF16)
    ms = (_dot(hi, same_group_bf16) + _dot(mid, same_group_bf16)) * (1.0 / group)
    return x * lax.rsqrt(ms + EPS) * gain


def _merge_ffn_kernel(x_ref, mod_ref, oa_ref, ob_ref, oc_ref, rc_ref, na_ref, nb_ref, nc_ref,
                      gb_ref, gc_ref, wo_ref, gain_ref, w13_ref, w2_ref, fin_ref, o_ref, acc_ref,
                      *, lam_init, final):
    a = _rms(oa_ref[...], na_ref[...])
    b = _group_rms(ob_ref[...], gb_ref[...], DV_B, nb_ref[...]) * (1.0 - lam_init)
    c = _group_rms(oc_ref[...], gc_ref[...], DV_C, nc_ref[...]) * _silu(rc_ref[...])
    mixed = (_dot(a.astype(BF16), wo_ref[0:WIDTH_A, :])
             + _dot(b.astype(BF16), wo_ref[WIDTH_A:WIDTH_A + WIDTH_B, :])
             + _dot(c.astype(BF16), wo_ref[WIDTH_A + WIDTH_B:, :]))
    x = x_ref[...] + mod_ref[0, 5:6, :] * mixed
    y = _ffn_block(x, mod_ref, gain_ref[...], w13_ref, w2_ref, acc_ref, 6)
    if final:
        y = _rms(y, fin_ref[...])
    o_ref[...] = y


def _merge_ffn(x, mod, tokens_per_group, oa, ob, oc, rc, lw, w13, w2, consts, fin, layer, lam_init, final):
    n = x.shape[0]
    row = lambda w: pl.BlockSpec((TOKEN_TILE, w), lambda i: (i, 0))
    return pl.pallas_call(
        functools.partial(_merge_ffn_kernel, lam_init=lam_init, final=final),
        grid=(n // TOKEN_TILE,),
        in_specs=[row(D_MODEL), _mod_spec(tokens_per_group), row(WIDTH_A), row(WIDTH_B), row(WIDTH_C),
                  row(WIDTH_C), _layer_spec((1, WIDTH_A), layer), _layer_spec((1, WIDTH_B), layer),
                  _layer_spec((1, WIDTH_C), layer), _const_spec((WIDTH_B, WIDTH_B)),
                  _const_spec((WIDTH_C, WIDTH_C)), _layer_spec((D_MODEL, D_MODEL), layer, 1),
                  _layer_spec((1, D_MODEL), layer), _weight_spec(w13, layer), _weight_spec(w2, layer),
                  _const_spec((1, D_MODEL))],
        out_specs=row(D_MODEL),
        out_shape=jax.ShapeDtypeStruct((n, D_MODEL), F32),
        scratch_shapes=[pltpu.VMEM((TOKEN_TILE, D_MODEL), F32)],
        compiler_params=_params(1),
    )(x, mod, oa, ob, oc, rc, lw["out_norm_a"], lw["out_norm_b"], lw["out_norm_c"],
      consts["group_b"], consts["group_c"], lw["wo"], lw["norm_ffn2"], w13, w2, fin)


def _prepare_weights(w):
    n_layers = w["w_in"].shape[0]
    row = lambda v: v.reshape(n_layers, 1, -1)

    def pad_heads(m, width):
        m = m.reshape(n_layers, m.shape[1], H_A, width)
        m = jnp.pad(m, ((0, 0), (0, 0), (0, 0), (0, HEAD_PAD - width)))
        return m.reshape(n_layers, m.shape[1], H_A * HEAD_PAD)

    def gate_block(wg, first):
        lo = 0 if first else QK_C_PAD
        return jnp.pad(wg, ((0, 0), (0, 0), (lo, 2 * QK_C_PAD - lo - H_C * DK_C)))

    wg = jnp.concatenate([gate_block(w["gla_wg_f"], True), gate_block(w["gla_wg_b"], False),
                          jnp.zeros((n_layers, LANE - 2 * GATE_RANK, 2 * QK_C_PAD), F32)], axis=1)
    bg = gate_block(w["gla_bg_f"][:, None, :], True) + gate_block(w["gla_bg_b"][:, None, :], False)
    return {
        "norm_ffn1": row(w["norm_ffn1"]), "norm_ffn2": row(w["norm_ffn2"]),
        "norm_mix": row(w["norm_mix"]), "w_in_t": jnp.swapaxes(w["w_in"], 1, 2),
        "q_norm": row(w["mla_q_norm"]),
        "wuq": pad_heads(w["mla_w_uq"], DN_A + DR_A).astype(BF16),
        "kv_norm": row(w["mla_kv_norm"]),
        "wuk": pad_heads(w["mla_w_uk"], DN_A).astype(BF16),
        "wuvt": jnp.swapaxes(w["mla_w_uv"], 1, 2).astype(BF16),
        "wg": wg.astype(BF16), "bg": bg,
        "lam": jnp.stack([w["diff_lq1"], w["diff_lk1"], w["diff_lq2"], w["diff_lk2"]], axis=1),
        "out_norm_a": row(w["mla_out_norm"]),
        "out_norm_b": row(jnp.tile(w["diff_norm"], (1, H_B))),
        "out_norm_c": row(jnp.tile(w["gla_norm"], (1, H_C))),
        "wo": w["w_out"].astype(BF16),
    }


def _rope_tables(n_pos):
    pos = jnp.arange(n_pos)
    rows = (pos // GRID_W).astype(F32)
    cols = (pos % GRID_W).astype(F32)
    half = DH_B // 2
    inv = ROPE_BASE ** (-jnp.arange(0, half, 2, dtype=F32) / half)
    ang = jnp.concatenate([rows[:, None] * inv, rows[:, None] * inv,
                           cols[:, None] * inv, cols[:, None] * inv], axis=1)
    sign = jnp.tile(jnp.concatenate([-jnp.ones(8, F32), jnp.ones(8, F32)]), 2)
    cos32, sin32 = jnp.cos(ang), jnp.sin(ang) * sign
    cos_full, sin_full = jnp.tile(cos32, (1, LANE // 32)), jnp.tile(sin32, (1, LANE // 32))
    ones = lambda n: jnp.ones((n_pos, n), F32)
    zeros = lambda n: jnp.zeros((n_pos, n), F32)
    cos_head = jnp.concatenate([ones(ROPE_OFF), cos32, ones(LANE - ROPE_OFF - DR_A)], axis=1)
    sin_head = jnp.concatenate([zeros(ROPE_OFF), sin32, zeros(LANE - ROPE_OFF - DR_A)], axis=1)
    return cos_head, sin_head, cos_full, sin_full


def _same_group_matrix(width, group):
    idx = jnp.arange(width) // group
    return jnp.where(idx[:, None] == idx[None, :], 1.0, 0.0).astype(BF16)


def kernel(x_prompt, x_sample, cache_mla_ckv, cache_mla_krope, cache_diff_k, cache_diff_v, state_gla_fwd, state_gla_bwd, c, c_ctx, w_mod, b_mod, norm_ffn1, ffn1_w13, ffn1_w2, norm_mix, w_in, mla_q_norm, mla_w_uq, mla_kv_norm, mla_w_uk, mla_w_uv, mla_out_norm, diff_lq1, diff_lk1, diff_lq2, diff_lk2, diff_norm, gla_wg_f, gla_bg_f, gla_wg_b, gla_bg_b, gla_norm, w_out, norm_ffn2, ffn2_w13, ffn2_w2, final_norm):
    w = dict(norm_ffn1=norm_ffn1, ffn1_w13=ffn1_w13, ffn1_w2=ffn1_w2, norm_mix=norm_mix, w_in=w_in,
             mla_q_norm=mla_q_norm, mla_w_uq=mla_w_uq, mla_kv_norm=mla_kv_norm, mla_w_uk=mla_w_uk,
             mla_w_uv=mla_w_uv, mla_out_norm=mla_out_norm, diff_lq1=diff_lq1, diff_lk1=diff_lk1,
             diff_lq2=diff_lq2, diff_lk2=diff_lk2, diff_norm=diff_norm, gla_wg_f=gla_wg_f,
             gla_bg_f=gla_bg_f, gla_wg_b=gla_wg_b, gla_bg_b=gla_bg_b, gla_norm=gla_norm, w_out=w_out,
             norm_ffn2=norm_ffn2, ffn2_w13=ffn2_w13, ffn2_w2=ffn2_w2)
    n_ctx_b, n_ctx_t, _ = x_prompt.shape
    n_lat_b, n_lat_t, _ = x_sample.shape
    n_past = cache_mla_ckv.shape[2]

    cvec = jnp.concatenate([c_ctx[None, :], c, jnp.zeros((MOD_ROWS - 1 - n_lat_b, D_MODEL), F32)], axis=0)
    mod = _modulation(cvec, w_mod, b_mod).reshape(DEPTH, MOD_ROWS, N_MOD, D_MODEL)

    rope_tabs = _rope_tables(n_lat_t)
    consts = {"group_b": _same_group_matrix(WIDTH_B, DV_B), "group_c": _same_group_matrix(WIDTH_C, DV_C)}
    cache_kr = jnp.pad(cache_mla_krope, ((0, 0), (0, 0), (0, 0), (ROPE_OFF, LANE - ROPE_OFF - DR_A)))
    cache_dk = cache_diff_k.reshape(n_lat_b, DEPTH, n_past, WIDTH_B)
    cache_dv = cache_diff_v.reshape(n_lat_b, DEPTH, n_past, WIDTH_B)
    fin = final_norm.reshape(1, D_MODEL)

    xp = x_prompt.reshape(n_ctx_b * n_ctx_t, D_MODEL)
    xs = x_sample.reshape(n_lat_b * n_lat_t, D_MODEL)
    new_kr = []
    new_cache = new_states = None
    lw = _prepare_weights(w)
    ffn1_w = (ffn1_w13[0].astype(BF16), ffn1_w2[0].astype(BF16))
    for l in range(DEPTH):
        lam_init = 0.8 - 0.6 * math.exp(-0.3 * l)
        last = l == DEPTH - 1
        mod_ctx, mod_lat = mod[l, 0:1], mod[l, 1:1 + n_lat_b]

        casts = [(ffn2_w13, l), (ffn2_w2, l)] + ([] if last else [(ffn1_w13, l + 1), (ffn1_w2, l + 1)])
        xp, xs, converted = _ffn(xp, mod_ctx, n_ctx_b * n_ctx_t, xs, mod_lat, n_lat_t, lw["norm_ffn1"],
                                 ffn1_w[0], ffn1_w[1], l, mod_row=0, casts=casts)
        ffn2_w, ffn1_w = converted[:2], converted[2:]

        def trunk(x, m, tpg, n_batch, n_tok, rope, mla_cache, diff_cache, gla_init, is_ctx):
            p = _project(x, m, tpg, lw, lw["w_in_t"], l, rope, seq_len=n_tok if is_ctx else 0,
                         carried=new_cache if is_ctx else None)
            oa = _mla(p, lw, n_batch, n_tok, mla_cache, l)
            ob = _diff(p, lw, n_batch, n_tok, diff_cache, l, lam_init)
            gla_out = _gla(p, n_batch, n_tok, l, init=gla_init, emit_state=is_ctx,
                           carried=new_states if is_ctx else None)
            x = _merge_ffn(x, m, tpg, oa, ob, gla_out[0], p["rc"], lw, ffn2_w[0], ffn2_w[1], consts, fin, l,
                           lam_init, last)
            return x, p, gla_out

        xp, p, gla_out = trunk(xp, mod_ctx, n_ctx_b * n_ctx_t, n_ctx_b, n_ctx_t, None, None, None, None, True)
        new_cache = p["cache"]
        new_states = tuple(gla_out[1:])
        new_kr.append(p["kr"][:, ROPE_OFF:ROPE_OFF + DR_A].reshape(n_ctx_b, n_ctx_t, DR_A))

        xs, _, _ = trunk(xs, mod_lat, n_lat_t, n_lat_b, n_lat_t, rope_tabs, (cache_mla_ckv, cache_kr),
                         (cache_dk, cache_dv), (state_gla_fwd, state_gla_bwd), False)

    new_ckv, new_dkt, new_dvt = new_cache

    def token_major(t, dh):
        return jnp.transpose(t.reshape(n_ctx_b, DEPTH, H_B, dh, n_ctx_t), (0, 1, 4, 2, 3))

    return (xp.reshape(x_prompt.shape), xs.reshape(x_sample.shape), new_ckv, jnp.stack(new_kr, axis=1),
            token_major(new_dkt, 2 * DH_B), token_major(new_dvt, DV_B), new_states[0], new_states[1])
```

```python
import functools
import math

import jax
import jax.numpy as jnp
from jax import lax
from jax.experimental import pallas as pl
from jax.experimental.pallas import tpu as pltpu

F32 = jnp.float32
BF16 = jnp.bfloat16

D_MODEL = 1024
DEPTH = 2
GRID_W = 64
ROPE_BASE = 10000.0
EPS = 1e-6
H_A, DN_A, DR_A, DV_A = 6, 64, 32, 64
Q_RANK, KV_RANK = 384, 256
H_B, DH_B, DV_B = 4, 32, 64
H_C, DK_C, DV_C = 4, 48, 96
GATE_RANK = 16
GATE_NORM = 16.0
CHUNK = 64
WIDTH_A = H_A * DV_A
WIDTH_B = H_B * DV_B
WIDTH_C = H_C * DV_C
D_FF = 2816
N_MOD = 9

LANE = 128
FF_CHUNK = 256
TOKEN_TILE = 512
FFN_TILE = 1024
Q_TILE = 1024
MOD_ROWS = 16
Q_ROWS = 128
ONES_ROWS = 16
MOD_TILE = 2304
GLA_UNROLL = 4
ATTN_SEQS = 4
GLA_SEQS = 2
GLA_ROWS = 1024
VMEM_LIMIT = 56 * 1024 * 1024

HEAD_PAD = LANE
ROPE_OFF = DN_A
QK_C_PAD = 256
SEG = {}
_off = 0
for _name, _w in (("cq", Q_RANK), ("krope", LANE), ("ckv", KV_RANK), ("qb", WIDTH_B), ("kb", WIDTH_B),
                  ("vb", WIDTH_B), ("qc", QK_C_PAD), ("kc", QK_C_PAD), ("vc", WIDTH_C), ("gl", LANE),
                  ("rc", WIDTH_C)):
    SEG[_name] = (_off, _off + _w)
    _off += _w
PROJ_PAD = _off

PROJ_SPLITS = (Q_RANK, KV_RANK, DR_A, 2 * H_B * DH_B, 2 * H_B * DH_B, H_B * DV_B, H_C * DK_C, H_C * DK_C,
               WIDTH_C, WIDTH_C, GATE_RANK, GATE_RANK)
PROJ_COLS = sum(PROJ_SPLITS)
_SRC = [sum(PROJ_SPLITS[:i]) for i in range(len(PROJ_SPLITS) + 1)]


def _src(i, j=None):
    return ("src", _SRC[i], _SRC[(i if j is None else j) + 1])


PROJ_LAYOUT = (
    ("cq", (_src(0),)), ("ckv", (_src(1),)),
    ("krope", (("zero", 0, ROPE_OFF), _src(2), ("zero", 0, LANE - ROPE_OFF - DR_A))),
    ("qb", (_src(3),)), ("kb", (_src(4),)), ("vb", (_src(5),)),
    ("qc", (_src(6), ("zero", 0, QK_C_PAD - H_C * DK_C))),
    ("kc", (_src(7), ("zero", 0, QK_C_PAD - H_C * DK_C))),
    ("vc", (_src(8),)), ("rc", (_src(9),)),
    ("gl", (_src(10, 11), ("zero", 0, LANE - 2 * GATE_RANK))),
)


def _dot(a, b):
    return jnp.dot(a, b, preferred_element_type=F32)


def _dot_nt(a, b):
    return lax.dot_general(a, b, (((1,), (1,)), ((), ())), preferred_element_type=F32)


def _dot_tn(a, b):
    return lax.dot_general(a, b, (((0,), (0,)), ((), ())), preferred_element_type=F32)


def _rms(x, gain):
    ms = jnp.mean(x * x, axis=-1, keepdims=True)
    return (x * lax.rsqrt(ms + EPS)) * gain


def _silu(x):
    return x * jax.nn.sigmoid(x)


def _const_spec(shape):
    nd = len(shape)
    return pl.BlockSpec(shape, lambda *_: (0,) * nd)


def _layer_spec(shape, layer, buffers=None):
    nd = len(shape)
    mode = {} if buffers is None else {"pipeline_mode": pl.Buffered(buffers)}
    return pl.BlockSpec((None,) + tuple(shape), lambda *_: (layer,) + (0,) * nd, **mode)


def _params(n_axes):
    return pltpu.CompilerParams(dimension_semantics=("arbitrary",) * n_axes,
                                vmem_limit_bytes=VMEM_LIMIT)


def _mod_kernel(c_ref, w_ref, b_ref, o_ref):
    s = _silu(c_ref[...]).astype(BF16)
    o_ref[0] = _dot(s, w_ref[0].astype(BF16)) + b_ref[0]


def _modulation(cvec, w_mod, b_mod):
    n_rows = cvec.shape[0]
    width = N_MOD * D_MODEL
    return pl.pallas_call(
        _mod_kernel,
        grid=(DEPTH, width // MOD_TILE),
        in_specs=[
            pl.BlockSpec((n_rows, D_MODEL), lambda l, j: (0, 0)),
            pl.BlockSpec((1, D_MODEL, MOD_TILE), lambda l, j: (l, 0, j)),
            pl.BlockSpec((1, 1, MOD_TILE), lambda l, j: (l, 0, j)),
        ],
        out_specs=pl.BlockSpec((1, n_rows, MOD_TILE), lambda l, j: (l, 0, j)),
        out_shape=jax.ShapeDtypeStruct((DEPTH, n_rows, width), F32),
        compiler_params=_params(2),
    )(cvec, w_mod, b_mod.reshape(DEPTH, 1, width))


def _all_layer_spec(lead, shape, layer, first):
    zeros = (0,) * len(shape)
    if first:
        return pl.BlockSpec((lead, DEPTH) + tuple(shape), lambda i: (i, 0) + zeros)
    return pl.BlockSpec((lead, None) + tuple(shape), lambda i: (i, layer) + zeros)


def _put_layer(ref, lead, tail, layer, first, val):
    if not first:
        ref[lead + tail] = val
        return
    for l in range(DEPTH):
        ref[lead + (l,) + tail] = val if l == layer else jnp.zeros_like(val)


def _mod_spec(tokens_per_group, tile=TOKEN_TILE):
    return pl.BlockSpec((1, N_MOD, D_MODEL), lambda i: ((i * tile) // tokens_per_group, 0, 0))


def _ada_norm(x, gain, mod_ref, first_row):
    shift = mod_ref[0, first_row:first_row + 1, :]
    scale = mod_ref[0, first_row + 1:first_row + 2, :]
    return _rms(x, gain) * (1.0 + scale) + shift


def _ffn_block(x, mod_ref, gain, w13_ref, w2_ref, acc_ref, mod_row):
    u = _ada_norm(x, gain, mod_ref, mod_row).astype(BF16)
    for c in range(D_FF // FF_CHUNK):
        lo, hi = c * FF_CHUNK, (c + 1) * FF_CHUNK
        a = _dot(u, w13_ref[:, lo:hi])
        b = _dot(u, w13_ref[:, D_FF + lo:D_FF + hi])
        t = _dot((_silu(a) * b).astype(BF16), w2_ref[lo:hi, :])
        if c == 0:
            acc_ref[...] = t
        else:
            acc_ref[...] += t
    gate = mod_ref[0, mod_row + 2:mod_row + 3, :]
    return x + (0.5 * gate) * acc_ref[...]


def _ffn_kernel(*refs, mod_row, n_cast):
    x_ref, mod_ref, gain_ref, w13_ref, w2_ref = refs[:5]
    cast_in = refs[5:5 + n_cast]
    o_ref = refs[5 + n_cast]
    cast_out = refs[6 + n_cast:6 + 2 * n_cast]
    acc_ref = refs[6 + 2 * n_cast]
    for src, dst in zip(cast_in, cast_out):
        dst[...] = src[...].astype(BF16)
    o_ref[...] = _ffn_block(x_ref[...], mod_ref, gain_ref[...], w13_ref, w2_ref, acc_ref, mod_row)


def _weight_spec(arr, layer):
    if arr.ndim == 2:
        return pl.BlockSpec(arr.shape, lambda *_: (0, 0), pipeline_mode=pl.Buffered(1))
    return _layer_spec(arr.shape[1:], layer, 1)


def _ffn(x, mod, tokens_per_group, gain, w13, w2, layer, *, mod_row, cast=None):
    n = x.shape[0]
    steps = n // FFN_TILE
    tile = pl.BlockSpec((FFN_TILE, D_MODEL), lambda i: (i, 0))
    in_specs = [tile, _mod_spec(tokens_per_group, FFN_TILE), _layer_spec((1, D_MODEL), layer),
                _weight_spec(w13, layer), _weight_spec(w2, layer)]
    args = [x, mod, gain, w13, w2]
    out_specs = [tile]
    out_shape = [jax.ShapeDtypeStruct((n, D_MODEL), F32)]
    n_cast = 0
    if cast is not None:
        sources, cast_layer = cast
        n_cast = len(sources)
        for src in sources:
            rows, cols = src.shape[1] // steps, src.shape[2]
            in_specs.append(pl.BlockSpec((None, rows, cols), lambda i: (cast_layer, i, 0)))
            out_specs.append(pl.BlockSpec((rows, cols), lambda i: (i, 0)))
            out_shape.append(jax.ShapeDtypeStruct(src.shape[1:], BF16))
        args += list(sources)
    outs = pl.pallas_call(
        functools.partial(_ffn_kernel, mod_row=mod_row, n_cast=n_cast),
        grid=(steps,),
        in_specs=in_specs,
        out_specs=out_specs,
        out_shape=out_shape,
        scratch_shapes=[pltpu.VMEM((FFN_TILE, D_MODEL), F32)],
        compiler_params=_params(1),
    )(*args)
    return outs[0], tuple(outs[1:])


def _rope(x, cos, sin_signed):
    first = (lax.broadcasted_iota(jnp.int32, (x.shape[0], LANE), 1) % 16) < 8
    outs = []
    for j in range(x.shape[1] // LANE):
        xj = x[:, j * LANE:(j + 1) * LANE]
        partner = jnp.where(first, pltpu.roll(xj, LANE - 8, 1), pltpu.roll(xj, 8, 1))
        outs.append(xj * cos + partner * sin_signed)
    return outs[0] if len(outs) == 1 else jnp.concatenate(outs, axis=1)


def _log_sigmoid(z):
    return jnp.minimum(z, 0.0) - jnp.log1p(jnp.exp(-jnp.abs(z)))


def _proj_kernel(*refs, rope, n_carried, seq_len, layer):
    (x_ref, mod_ref, gain_ref, wint_ref, qn_ref, wuq_ref, kvn_ref, wg_ref, bg_ref) = refs[:9]
    refs = refs[9:]
    if rope:
        ch_ref, sh_ref, cf_ref, sf_ref = refs[:4]
        refs = refs[4:]
    refs = refs[n_carried:]
    (q_o, ckv_o, kr_o, qb_o, kb_o, vb_o, qc_o, kc_o, vc_o, rc_o, gf_o, gb_o) = refs[:12]
    refs = refs[12:]
    if seq_len:
        ckv_all_o, kbt_all_o, vbt_all_o = refs[:3]
        refs = refs[3:]
    wpt_ref, = refs

    @pl.when(pl.program_id(0) == 0)
    def _():
        for name, pieces in PROJ_LAYOUT:
            parts = []
            for kind, a, b in pieces:
                parts.append(jnp.zeros((b - a, D_MODEL), F32) if kind == "zero" else wint_ref[a:b, :])
            blk = parts[0] if len(parts) == 1 else jnp.concatenate(parts, axis=0)
            lo, hi = SEG[name]
            wpt_ref[lo:hi, :] = blk.astype(BF16)

    u = _ada_norm(x_ref[...], gain_ref[...], mod_ref, 3).astype(BF16)

    def seg(first, last=None):
        lo, hi = SEG[first][0], SEG[last or first][1]
        return _dot_nt(u, wpt_ref[lo:hi, :])

    cq_kr = seg("cq", "krope")
    q = _dot(_rms(cq_kr[:, :Q_RANK], qn_ref[...]).astype(BF16), wuq_ref[...])
    kr = cq_kr[:, Q_RANK:]
    qb = seg("qb")
    if seq_len:
        kbt = _dot_nt(wpt_ref[SEG["kb"][0]:SEG["kb"][1], :], u)
        vbt = _dot_nt(wpt_ref[SEG["vb"][0]:SEG["vb"][1], :], u)
        kb, vb = kbt.T, vbt.T
    else:
        kb, vb = seg("kb"), seg("vb")
    if rope:
        q = _rope(q, ch_ref[...], sh_ref[...])
        kr = _rope(kr, ch_ref[...], sh_ref[...])
        qb = _rope(qb, cf_ref[...], sf_ref[...])
        kb = _rope(kb, cf_ref[...], sf_ref[...])
    q_o[...] = (q * (DN_A + DR_A) ** -0.5).astype(BF16)
    ckv = _rms(seg("ckv"), kvn_ref[...])
    ckv_o[...] = ckv
    if seq_len:
        for j in range(u.shape[0] // seq_len):
            tok = slice(j * seq_len, (j + 1) * seq_len)
            _put_layer(ckv_all_o, (j,), (), layer, n_carried == 0, ckv[tok, :])
            _put_layer(kbt_all_o, (j,), (), layer, n_carried == 0, kbt[:, tok])
            _put_layer(vbt_all_o, (j,), (), layer, n_carried == 0, vbt[:, tok])
    kr_o[...] = kr
    qb_o[...] = (qb * DH_B ** -0.5).astype(BF16)
    kb_o[...] = kb
    vb_o[...] = vb
    qc_o[...] = seg("qc") * (DK_C ** -0.5)
    kc_o[...] = seg("kc")
    vc_gl = seg("vc", "gl")
    vc_o[...] = vc_gl[:, :WIDTH_C]
    rc_o[...] = seg("rc")
    z = _dot(vc_gl[:, WIDTH_C:].astype(BF16), wg_ref[...]) + bg_ref[...]
    g = _log_sigmoid(z) / GATE_NORM
    gf_o[...] = g[:, :QK_C_PAD]
    gb_o[...] = g[:, QK_C_PAD:]


PROJ_OUT = (("q", H_A * HEAD_PAD, BF16), ("ckv", KV_RANK, F32), ("kr", LANE, F32), ("qb", WIDTH_B, BF16),
            ("kb", WIDTH_B, F32), ("vb", WIDTH_B, F32), ("qc", QK_C_PAD, F32), ("kc", QK_C_PAD, F32),
            ("vc", WIDTH_C, F32), ("rc", WIDTH_C, F32), ("gf", QK_C_PAD, F32), ("gb", QK_C_PAD, F32))


def _project(x, mod, tokens_per_group, lw, w_in, layer, rope_tabs, seq_len=0, carried=None):
    n = x.shape[0]
    rope = rope_tabs is not None
    row = lambda w: pl.BlockSpec((TOKEN_TILE, w), lambda i: (i, 0))
    in_specs = [row(D_MODEL), _mod_spec(tokens_per_group), _layer_spec((1, D_MODEL), layer),
                _layer_spec((PROJ_COLS, D_MODEL), layer, 1), _layer_spec((1, Q_RANK), layer),
                _layer_spec((Q_RANK, H_A * HEAD_PAD), layer), _layer_spec((1, KV_RANK), layer),
                _layer_spec((LANE, 2 * QK_C_PAD), layer), _layer_spec((1, 2 * QK_C_PAD), layer)]
    args = [x, mod, lw["norm_mix"], w_in, lw["q_norm"], lw["wuq"], lw["kv_norm"], lw["wg"], lw["bg"]]
    if rope:
        n_pos = rope_tabs[0].shape[0]
        tab = pl.BlockSpec((TOKEN_TILE, LANE), lambda i: (i % (n_pos // TOKEN_TILE), 0))
        in_specs += [tab] * 4
        args += list(rope_tabs)
    out_specs = [row(w) for _, w, _ in PROJ_OUT]
    out_shape = [jax.ShapeDtypeStruct((n, w), dt) for _, w, dt in PROJ_OUT]
    aliases = {}
    if seq_len:
        per_tile = TOKEN_TILE // seq_len
        if carried is not None:
            for k in range(len(carried)):
                aliases[len(args) + k] = len(out_specs) + k
            in_specs += [pl.BlockSpec(memory_space=pl.ANY)] * len(carried)
            args += list(carried)
        for shape in ((seq_len, KV_RANK), (WIDTH_B, seq_len), (WIDTH_B, seq_len)):
            out_specs.append(_all_layer_spec(per_tile, shape, layer, carried is None))
            out_shape.append(jax.ShapeDtypeStruct((n // seq_len, DEPTH) + shape, F32))
    outs = pl.pallas_call(
        functools.partial(_proj_kernel, rope=rope, n_carried=0 if carried is None else len(carried),
                          seq_len=seq_len, layer=layer),
        grid=(n // TOKEN_TILE,),
        in_specs=in_specs,
        out_specs=out_specs,
        out_shape=out_shape,
        input_output_aliases=aliases,
        scratch_shapes=[pltpu.VMEM((PROJ_PAD, D_MODEL), BF16)],
        compiler_params=_params(1),
    )(*args)
    result = {name: o for (name, _, _), o in zip(PROJ_OUT, outs)}
    if seq_len:
        result["cache"] = tuple(outs[len(PROJ_OUT):])
    return result


def _lane_mask(width, lo, hi):
    lane = lax.broadcasted_iota(jnp.int32, (1, width), 1)
    return jnp.where((lane >= lo) & (lane < hi), 1.0, 0.0).astype(F32)


def _attend_t(units, dv):
    outs = []
    jobs = []
    for u, unit in enumerate(units):
        q, keys, vt_ext = unit()
        for r in range(0, q.shape[0], Q_ROWS):
            jobs.append((u, q[r:r + Q_ROWS], keys, vt_ext, r + Q_ROWS >= q.shape[0]))
    parts = []
    s = _dot_nt(jobs[0][1], jobs[0][2])
    for j, (u, _, _, vt_ext, last_block) in enumerate(jobs):
        s_now = s
        if j + 1 < len(jobs):
            s = _dot_nt(jobs[j + 1][1], jobs[j + 1][2])
        parts.append(jnp.exp((s_now - jnp.max(s_now, axis=-1, keepdims=True)).astype(BF16)))
        if last_block:
            e = parts[0] if len(parts) == 1 else jnp.concatenate(parts, axis=0)
            parts = []
            r = _dot_nt(vt_ext, e)
            outs.append(r[0:dv] * (1.0 / r[dv:dv + 1]))
    return outs


def _attn_seqs(n_cache, n_q_tiles):
    return ATTN_SEQS if (n_cache == 0 and n_q_tiles == 1) else 1


def _mla_kernel(*refs, n_cache, n_seq, n_tok):
    if n_cache:
        q_ref, ckv_ref, kr_ref, cckv_ref, ckr_ref, wuk_ref, wuvt_ref, o_ref, kf_scr, vt_scr = refs
    else:
        q_ref, ckv_ref, kr_ref, wuk_ref, wuvt_ref, o_ref, kf_scr, vt_scr = refs

    @pl.when(pl.program_id(1) == 0)
    def _():
        def fill(s, row0, ckv, kr):
            rows = ckv.shape[0]
            cb = ckv.astype(BF16)
            kr_all = jnp.concatenate([kr] * H_A, axis=1)
            kf_scr[s, row0:row0 + rows, :] = (_dot(cb, wuk_ref[...]) + kr_all).astype(BF16)
            vt = _dot_nt(wuvt_ref[...], cb)
            for h in range(H_A):
                vt_scr[s * H_A + h, 0:DV_A, row0:row0 + rows] = vt[h * DV_A:(h + 1) * DV_A].astype(BF16)

        for s in range(n_seq):
            if n_cache:
                fill(s, 0, cckv_ref[...], ckr_ref[...])
            tok = slice(s * n_tok, (s + 1) * n_tok)
            fill(s, n_cache, ckv_ref[tok, :], kr_ref[tok, :])
        for u in range(n_seq * H_A):
            vt_scr[u, DV_A:, :] = jnp.ones((ONES_ROWS, vt_scr.shape[2]), BF16)

    n_q = q_ref.shape[0] // n_seq

    def unit(s, h):
        sl = slice(h * HEAD_PAD, (h + 1) * HEAD_PAD)
        return lambda: (q_ref[s * n_q:(s + 1) * n_q, sl], kf_scr[s, :, sl], vt_scr[s * H_A + h])

    heads = _attend_t([unit(s, h) for h in range(H_A) for s in range(n_seq)], DV_A)
    for s in range(n_seq):
        o_ref[s * n_q:(s + 1) * n_q, :] = jnp.concatenate(heads[s::n_seq], axis=0).T


def _mla(p, lw, n_batch, n_tok, cache, layer):
    n_cache = 0 if cache is None else cache[0].shape[2]
    q_tile = min(Q_TILE, n_tok)
    nqt = n_tok // q_tile
    n_seq = _attn_seqs(n_cache, nqt)
    in_specs = [pl.BlockSpec((n_seq * q_tile, H_A * HEAD_PAD), lambda b, i: (b * nqt + i, 0)),
                pl.BlockSpec((n_seq * n_tok, KV_RANK), lambda b, i: (b, 0)),
                pl.BlockSpec((n_seq * n_tok, LANE), lambda b, i: (b, 0))]
    args = [p["q"], p["ckv"], p["kr"]]
    if n_cache:
        in_specs += [pl.BlockSpec((None, None, n_cache, KV_RANK), lambda b, i: (b, layer, 0, 0)),
                     pl.BlockSpec((None, None, n_cache, LANE), lambda b, i: (b, layer, 0, 0))]
        args += list(cache)
    in_specs += [_layer_spec((KV_RANK, H_A * HEAD_PAD), layer), _layer_spec((WIDTH_A, KV_RANK), layer)]
    args += [lw["wuk"], lw["wuvt"]]
    n_keys = n_cache + n_tok
    return pl.pallas_call(
        functools.partial(_mla_kernel, n_cache=n_cache, n_seq=n_seq, n_tok=n_tok),
        grid=(n_batch // n_seq, nqt),
        in_specs=in_specs,
        out_specs=pl.BlockSpec((n_seq * q_tile, WIDTH_A), lambda b, i: (b * nqt + i, 0)),
        out_shape=jax.ShapeDtypeStruct((n_batch * n_tok, WIDTH_A), F32),
        scratch_shapes=[pltpu.VMEM((n_seq, n_keys, H_A * HEAD_PAD), BF16),
                        pltpu.VMEM((n_seq * H_A, DV_A + ONES_ROWS, n_keys), BF16)],
        compiler_params=_params(2),
    )(*args)


def _diff_kernel(*refs, n_cache, n_seq, n_tok, lam_init):
    if n_cache:
        q_ref, k_ref, v_ref, ck_ref, cv_ref, lam_ref, o_ref, k_scr, vt_scr = refs
    else:
        q_ref, k_ref, v_ref, lam_ref, o_ref, k_scr, vt_scr = refs

    @pl.when(pl.program_id(1) == 0)
    def _():
        def fill(s, row0, k, v):
            rows = k.shape[0]
            k_scr[s, row0:row0 + rows, :] = k.astype(BF16)
            vt = v.T
            for h in range(H_B):
                vt_scr[s * H_B + h, 0:DV_B, row0:row0 + rows] = vt[h * DV_B:(h + 1) * DV_B].astype(BF16)

        for s in range(n_seq):
            if n_cache:
                fill(s, 0, ck_ref[...], cv_ref[...])
            tok = slice(s * n_tok, (s + 1) * n_tok)
            fill(s, n_cache, k_ref[tok, :], v_ref[tok, :])
        for u in range(n_seq * H_B):
            vt_scr[u, DV_B:, :] = jnp.ones((ONES_ROWS, vt_scr.shape[2]), BF16)

    lv = lam_ref[...]
    lam = (jnp.exp(jnp.sum(lv[0:1] * lv[1:2], axis=-1, keepdims=True))
           - jnp.exp(jnp.sum(lv[2:3] * lv[3:4], axis=-1, keepdims=True)) + lam_init)
    n_q = q_ref.shape[0] // n_seq
    lane = lax.broadcasted_iota(jnp.int32, (n_q, LANE), 1)

    def unit(s, h, j):
        blk = slice((h // 2) * LANE, (h // 2 + 1) * LANE)
        lo = (h % 2) * 2 * DH_B + j * DH_B

        def fn():
            qh = q_ref[s * n_q:(s + 1) * n_q, blk]
            qm = jnp.where((lane >= lo) & (lane < lo + DH_B), qh, jnp.zeros_like(qh))
            return qm, k_scr[s, :, blk], vt_scr[s * H_B + h]
        return fn

    maps = _attend_t([unit(s, h, j) for h in range(H_B) for j in range(2) for s in range(n_seq)], DV_B)
    for s in range(n_seq):
        mine = maps[s::n_seq]
        heads = [mine[2 * h] - lam * mine[2 * h + 1] for h in range(H_B)]
        o_ref[s * n_q:(s + 1) * n_q, :] = jnp.concatenate(heads, axis=0).T


def _diff(p, lw, n_batch, n_tok, cache, layer, lam_init):
    n_cache = 0 if cache is None else cache[0].shape[2]
    q_tile = min(Q_TILE, n_tok)
    nqt = n_tok // q_tile
    n_seq = _attn_seqs(n_cache, nqt)
    in_specs = [pl.BlockSpec((n_seq * q_tile, WIDTH_B), lambda b, i: (b * nqt + i, 0)),
                pl.BlockSpec((n_seq * n_tok, WIDTH_B), lambda b, i: (b, 0)),
                pl.BlockSpec((n_seq * n_tok, WIDTH_B), lambda b, i: (b, 0))]
    args = [p["qb"], p["kb"], p["vb"]]
    if n_cache:
        in_specs += [pl.BlockSpec((None, None, n_cache, WIDTH_B), lambda b, i: (b, layer, 0, 0))] * 2
        args += list(cache)
    in_specs.append(_layer_spec((4, DH_B), layer))
    args.append(lw["lam"])
    n_keys = n_cache + n_tok
    return pl.pallas_call(
        functools.partial(_diff_kernel, n_cache=n_cache, n_seq=n_seq, n_tok=n_tok, lam_init=lam_init),
        grid=(n_batch // n_seq, nqt),
        in_specs=in_specs,
        out_specs=pl.BlockSpec((n_seq * q_tile, WIDTH_B), lambda b, i: (b * nqt + i, 0)),
        out_shape=jax.ShapeDtypeStruct((n_batch * n_tok, WIDTH_B), F32),
        scratch_shapes=[pltpu.VMEM((n_seq, n_keys, WIDTH_B), BF16),
                        pltpu.VMEM((n_seq * H_B, DV_B + ONES_ROWS, n_keys), BF16)],
        compiler_params=_params(2),
    )(*args)


def _split3(x):
    hi = x.astype(BF16)
    r1 = x - hi.astype(F32)
    mid = r1.astype(BF16)
    lo = (r1 - mid.astype(F32)).astype(BF16)
    return hi, mid, lo


def _dot_exact_lhs(a_bf16, x):
    hi, mid, lo = _split3(x)
    return _dot(a_bf16, hi) + _dot(a_bf16, mid) + _dot(a_bf16, lo)


def _gla_kernel(*refs, n_tok, n_seq, has_init, emit_state, n_carried, layer):
    q_ref, k_ref, v_ref, gf_ref, gb_ref = refs[:5]
    refs = refs[5:]
    if has_init:
        s0f_ref, s0b_ref = refs[:2]
        refs = refs[2:]
    refs = refs[n_carried:]
    o_ref = refs[0]
    refs = refs[1:]
    if emit_state:
        sf_ref, sb_ref = refs[:2]
        refs = refs[2:]
    st_scr, bd_scr = refs


    n_chunks = n_tok // CHUNK
    n_groups = n_chunks // GLA_UNROLL
    t_idx = lax.broadcasted_iota(jnp.int32, (CHUNK, CHUNK), 0)
    s_idx = lax.broadcasted_iota(jnp.int32, (CHUNK, CHUNK), 1)
    tri = [jnp.where(s_idx <= t_idx, 1.0, 0.0).astype(F32), jnp.where(s_idx >= t_idx, 1.0, 0.0).astype(F32)]
    tri4 = [jnp.concatenate([t] * H_C, axis=1) for t in tri]
    tri_b = [t.astype(BF16) for t in tri]
    head_k = [_lane_mask(QK_C_PAD, h * DK_C, (h + 1) * DK_C).astype(BF16) for h in range(H_C)]
    head_v = [_lane_mask(WIDTH_C, h * DV_C, (h + 1) * DV_C).astype(BF16) for h in range(H_C)]
    row_v = lax.broadcasted_iota(jnp.int32, (WIDTH_C, QK_C_PAD), 0)
    col_k = lax.broadcasted_iota(jnp.int32, (WIDTH_C, QK_C_PAD), 1)
    diag = jnp.zeros((WIDTH_C, QK_C_PAD), F32)
    for h in range(H_C):
        inside = ((row_v >= h * DV_C) & (row_v < (h + 1) * DV_C)
                  & (col_k >= h * DK_C) & (col_k < (h + 1) * DK_C))
        diag = jnp.where(inside, 1.0, diag)

    for s in range(n_seq):
        for d, s0_ref in enumerate((s0f_ref, s0b_ref) if has_init else (None, None)):
            if s0_ref is None:
                st_scr[s, d] = jnp.zeros((WIDTH_C, QK_C_PAD), F32)
            else:
                bd_scr[...] = jnp.zeros_like(bd_scr)
                for h in range(H_C):
                    bd_scr[h * DK_C:(h + 1) * DK_C, h * DV_C:(h + 1) * DV_C] = s0_ref[s, h]
                st_scr[s, d] = bd_scr[...].T
    o_ref[...] = jnp.zeros_like(o_ref)

    def chunk_steps(chains):
        loaded = []
        for s, d, c in chains:
            g_ref = gb_ref if d else gf_ref
            row0 = s * n_tok + c * CHUNK
            rows = pl.ds(row0 if isinstance(row0, int) else pl.multiple_of(row0, CHUNK), CHUNK)
            bcum = _dot_exact_lhs(tri_b[d], g_ref[rows, :])
            loaded.append((rows, q_ref[rows, :], k_ref[rows, :], v_ref[rows, :], bcum))
        scaled = []
        for (s, d, c), (rows, q, k, v, bcum) in zip(chains, loaded):
            blast = bcum[0:1, :] if d else bcum[CHUNK - 1:CHUNK, :]
            qe = (q * jnp.exp(bcum)).astype(BF16)
            ke = (k * jnp.exp(-bcum)).astype(BF16)
            kd = (k * jnp.exp(blast - bcum)).astype(BF16)
            vb = v.astype(BF16)
            ke_heads = jnp.concatenate([ke * head_k[h] for h in range(H_C)], axis=0)
            v_heads = jnp.concatenate([vb * head_v[h] for h in range(H_C)], axis=0)
            scaled.append((qe, ke_heads, kd, vb, v_heads, jnp.exp(blast)))
        products = []
        for (s, d, c), (qe, ke_heads, kd, vb, v_heads, decay) in zip(chains, scaled):
            st = st_scr[s, d]
            attn = _dot_nt(qe, ke_heads)
            inter = _dot_nt(qe, st.astype(BF16))
            upd = _dot_tn(vb, kd)
            products.append((st, attn, inter, upd))
        masked = []
        for (s, d, c), (qe, ke_heads, kd, vb, v_heads, decay), (st, attn, inter, upd) in zip(
                chains, scaled, products):
            st_scr[s, d] = st * decay + upd * diag
            masked.append((attn * tri4[d]).astype(BF16))
        for (rows, *_), (_, _, _, _, v_heads, _), (_, _, inter, _), attn_b in zip(
                loaded, scaled, products, masked):
            o_ref[rows, :] += _dot(attn_b, v_heads) + inter

    def group(i):
        for j in range(GLA_UNROLL):
            fwd = i * GLA_UNROLL + j
            chunk_steps([(s, d, n_chunks - 1 - fwd if d else fwd) for s in range(n_seq) for d in (0, 1)])

    if n_groups == 1:
        group(0)
    else:
        def body(i, carry):
            group(i)
            return carry
        lax.fori_loop(0, n_groups, body, 0)

    if emit_state:
        for s in range(n_seq):
            for d, s_ref in enumerate((sf_ref, sb_ref)):
                bd_scr[...] = st_scr[s, d].T
                for h in range(H_C):
                    _put_layer(s_ref, (s,), (h,), layer, n_carried == 0,
                               bd_scr[h * DK_C:(h + 1) * DK_C, h * DV_C:(h + 1) * DV_C])


def _gla(p, n_batch, n_tok, layer, init=None, emit_state=False, carried=None):
    n_seq = max(GLA_SEQS, GLA_ROWS // n_tok)
    tok = lambda w: pl.BlockSpec((n_seq * n_tok, w), lambda b: (b, 0))
    st_spec = pl.BlockSpec((n_seq, None, H_C, DK_C, DV_C), lambda b: (b, layer, 0, 0, 0))
    in_specs = [tok(QK_C_PAD), tok(QK_C_PAD), tok(WIDTH_C), tok(QK_C_PAD), tok(QK_C_PAD)]
    args = [p["qc"], p["kc"], p["vc"], p["gf"], p["gb"]]
    if init is not None:
        in_specs += [st_spec] * 2
        args += list(init)
    out_specs = [tok(WIDTH_C)]
    out_shape = [jax.ShapeDtypeStruct((n_batch * n_tok, WIDTH_C), F32)]
    aliases = {}
    if emit_state:
        if carried is not None:
            for k in range(len(carried)):
                aliases[len(args) + k] = len(out_specs) + k
            in_specs += [pl.BlockSpec(memory_space=pl.ANY)] * len(carried)
            args += list(carried)
        out_specs += [_all_layer_spec(n_seq, (H_C, DK_C, DV_C), layer, carried is None)] * 2
        out_shape += [jax.ShapeDtypeStruct((n_batch, DEPTH, H_C, DK_C, DV_C), F32)] * 2
    return pl.pallas_call(
        functools.partial(_gla_kernel, n_tok=n_tok, n_seq=n_seq, has_init=init is not None,
                          emit_state=emit_state, n_carried=0 if carried is None else len(carried), layer=layer),
        grid=(n_batch // n_seq,),
        in_specs=in_specs,
        out_specs=out_specs,
        out_shape=out_shape,
        input_output_aliases=aliases,
        scratch_shapes=[pltpu.VMEM((n_seq, 2, WIDTH_C, QK_C_PAD), F32), pltpu.VMEM((QK_C_PAD, WIDTH_C), F32)],
        compiler_params=_params(1),
    )(*args)


def _group_rms(x, same_group_bf16, group, gain):
    sq = x * x
    hi = sq.astype(BF16)
    mid = (sq - hi.astype(F32)).astype(BF16)
    ms = (_dot(hi, same_group_bf16) + _dot(mid, same_group_bf16)) * (1.0 / group)
    return x * lax.rsqrt(ms + EPS) * gain


def _merge_ffn_kernel(x_ref, mod_ref, oa_ref, ob_ref, oc_ref, rc_ref, na_ref, nb_ref, nc_ref,
                      gb_ref, gc_ref, wo_ref, gain_ref, w13_ref, w2_ref, fin_ref, o_ref, acc_ref,
                      *, lam_init, final):
    a = _rms(oa_ref[...], na_ref[...])
    b = _group_rms(ob_ref[...], gb_ref[...], DV_B, nb_ref[...]) * (1.0 - lam_init)
    c = _group_rms(oc_ref[...], gc_ref[...], DV_C, nc_ref[...]) * _silu(rc_ref[...])
    mixed = (_dot(a.astype(BF16), wo_ref[0:WIDTH_A, :])
             + _dot(b.astype(BF16), wo_ref[WIDTH_A:WIDTH_A + WIDTH_B, :])
             + _dot(c.astype(BF16), wo_ref[WIDTH_A + WIDTH_B:, :]))
    x = x_ref[...] + mod_ref[0, 5:6, :] * mixed
    y = _ffn_block(x, mod_ref, gain_ref[...], w13_ref, w2_ref, acc_ref, 6)
    if final:
        y = _rms(y, fin_ref[...])
    o_ref[...] = y


def _merge_ffn(x, mod, tokens_per_group, oa, ob, oc, rc, lw, w13, w2, consts, fin, layer, lam_init, final):
    n = x.shape[0]
    row = lambda w: pl.BlockSpec((TOKEN_TILE, w), lambda i: (i, 0))
    return pl.pallas_call(
        functools.partial(_merge_ffn_kernel, lam_init=lam_init, final=final),
        grid=(n // TOKEN_TILE,),
        in_specs=[row(D_MODEL), _mod_spec(tokens_per_group), row(WIDTH_A), row(WIDTH_B), row(WIDTH_C),
                  row(WIDTH_C), _layer_spec((1, WIDTH_A), layer), _layer_spec((1, WIDTH_B), layer),
                  _layer_spec((1, WIDTH_C), layer), _const_spec((WIDTH_B, WIDTH_B)),
                  _const_spec((WIDTH_C, WIDTH_C)), _layer_spec((D_MODEL, D_MODEL), layer, 1),
                  _layer_spec((1, D_MODEL), layer), _weight_spec(w13, layer), _weight_spec(w2, layer),
                  _const_spec((1, D_MODEL))],
        out_specs=row(D_MODEL),
        out_shape=jax.ShapeDtypeStruct((n, D_MODEL), F32),
        scratch_shapes=[pltpu.VMEM((TOKEN_TILE, D_MODEL), F32)],
        compiler_params=_params(1),
    )(x, mod, oa, ob, oc, rc, lw["out_norm_a"], lw["out_norm_b"], lw["out_norm_c"],
      consts["group_b"], consts["group_c"], lw["wo"], lw["norm_ffn2"], w13, w2, fin)


def _prepare_weights(w):
    n_layers = w["w_in"].shape[0]
    row = lambda v: v.reshape(n_layers, 1, -1)

    def pad_heads(m, width):
        m = m.reshape(n_layers, m.shape[1], H_A, width)
        m = jnp.pad(m, ((0, 0), (0, 0), (0, 0), (0, HEAD_PAD - width)))
        return m.reshape(n_layers, m.shape[1], H_A * HEAD_PAD)

    def gate_block(wg, first):
        lo = 0 if first else QK_C_PAD
        return jnp.pad(wg, ((0, 0), (0, 0), (lo, 2 * QK_C_PAD - lo - H_C * DK_C)))

    wg = jnp.concatenate([gate_block(w["gla_wg_f"], True), gate_block(w["gla_wg_b"], False),
                          jnp.zeros((n_layers, LANE - 2 * GATE_RANK, 2 * QK_C_PAD), F32)], axis=1)
    bg = gate_block(w["gla_bg_f"][:, None, :], True) + gate_block(w["gla_bg_b"][:, None, :], False)
    return {
        "norm_ffn1": row(w["norm_ffn1"]), "norm_ffn2": row(w["norm_ffn2"]),
        "norm_mix": row(w["norm_mix"]), "w_in_t": jnp.swapaxes(w["w_in"], 1, 2),
        "q_norm": row(w["mla_q_norm"]),
        "wuq": pad_heads(w["mla_w_uq"], DN_A + DR_A).astype(BF16),
        "kv_norm": row(w["mla_kv_norm"]),
        "wuk": pad_heads(w["mla_w_uk"], DN_A).astype(BF16),
        "wuvt": jnp.swapaxes(w["mla_w_uv"], 1, 2).astype(BF16),
        "wg": wg.astype(BF16), "bg": bg,
        "lam": jnp.stack([w["diff_lq1"], w["diff_lk1"], w["diff_lq2"], w["diff_lk2"]], axis=1),
        "out_norm_a": row(w["mla_out_norm"]),
        "out_norm_b": row(jnp.tile(w["diff_norm"], (1, H_B))),
        "out_norm_c": row(jnp.tile(w["gla_norm"], (1, H_C))),
        "wo": w["w_out"].astype(BF16),
    }


def _rope_tables(n_pos):
    pos = jnp.arange(n_pos)
    rows = (pos // GRID_W).astype(F32)
    cols = (pos % GRID_W).astype(F32)
    half = DH_B // 2
    inv = ROPE_BASE ** (-jnp.arange(0, half, 2, dtype=F32) / half)
    ang = jnp.concatenate([rows[:, None] * inv, rows[:, None] * inv,
                           cols[:, None] * inv, cols[:, None] * inv], axis=1)
    sign = jnp.tile(jnp.concatenate([-jnp.ones(8, F32), jnp.ones(8, F32)]), 2)
    cos32, sin32 = jnp.cos(ang), jnp.sin(ang) * sign
    cos_full, sin_full = jnp.tile(cos32, (1, LANE // 32)), jnp.tile(sin32, (1, LANE // 32))
    ones = lambda n: jnp.ones((n_pos, n), F32)
    zeros = lambda n: jnp.zeros((n_pos, n), F32)
    cos_head = jnp.concatenate([ones(ROPE_OFF), cos32, ones(LANE - ROPE_OFF - DR_A)], axis=1)
    sin_head = jnp.concatenate([zeros(ROPE_OFF), sin32, zeros(LANE - ROPE_OFF - DR_A)], axis=1)
    return cos_head, sin_head, cos_full, sin_full


def _same_group_matrix(width, group):
    idx = jnp.arange(width) // group
    return jnp.where(idx[:, None] == idx[None, :], 1.0, 0.0).astype(BF16)


def kernel(x_prompt, x_sample, cache_mla_ckv, cache_mla_krope, cache_diff_k, cache_diff_v, state_gla_fwd, state_gla_bwd, c, c_ctx, w_mod, b_mod, norm_ffn1, ffn1_w13, ffn1_w2, norm_mix, w_in, mla_q_norm, mla_w_uq, mla_kv_norm, mla_w_uk, mla_w_uv, mla_out_norm, diff_lq1, diff_lk1, diff_lq2, diff_lk2, diff_norm, gla_wg_f, gla_bg_f, gla_wg_b, gla_bg_b, gla_norm, w_out, norm_ffn2, ffn2_w13, ffn2_w2, final_norm):
    w = dict(norm_ffn1=norm_ffn1, ffn1_w13=ffn1_w13, ffn1_w2=ffn1_w2, norm_mix=norm_mix, w_in=w_in,
             mla_q_norm=mla_q_norm, mla_w_uq=mla_w_uq, mla_kv_norm=mla_kv_norm, mla_w_uk=mla_w_uk,
             mla_w_uv=mla_w_uv, mla_out_norm=mla_out_norm, diff_lq1=diff_lq1, diff_lk1=diff_lk1,
             diff_lq2=diff_lq2, diff_lk2=diff_lk2, diff_norm=diff_norm, gla_wg_f=gla_wg_f,
             gla_bg_f=gla_bg_f, gla_wg_b=gla_wg_b, gla_bg_b=gla_bg_b, gla_norm=gla_norm, w_out=w_out,
             norm_ffn2=norm_ffn2, ffn2_w13=ffn2_w13, ffn2_w2=ffn2_w2)
    n_ctx_b, n_ctx_t, _ = x_prompt.shape
    n_lat_b, n_lat_t, _ = x_sample.shape
    n_past = cache_mla_ckv.shape[2]

    cvec = jnp.concatenate([c_ctx[None, :], c, jnp.zeros((MOD_ROWS - 1 - n_lat_b, D_MODEL), F32)], axis=0)
    mod = _modulation(cvec, w_mod, b_mod).reshape(DEPTH, MOD_ROWS, N_MOD, D_MODEL)

    rope_tabs = _rope_tables(n_lat_t)
    consts = {"group_b": _same_group_matrix(WIDTH_B, DV_B), "group_c": _same_group_matrix(WIDTH_C, DV_C)}
    cache_kr = jnp.pad(cache_mla_krope, ((0, 0), (0, 0), (0, 0), (ROPE_OFF, LANE - ROPE_OFF - DR_A)))
    cache_dk = cache_diff_k.reshape(n_lat_b, DEPTH, n_past, WIDTH_B)
    cache_dv = cache_diff_v.reshape(n_lat_b, DEPTH, n_past, WIDTH_B)
    fin = final_norm.reshape(1, D_MODEL)

    xp = x_prompt.reshape(n_ctx_b * n_ctx_t, D_MODEL)
    xs = x_sample.reshape(n_lat_b * n_lat_t, D_MODEL)
    new_kr = []
    new_cache = new_states = None
    lw = _prepare_weights(w)
    ffn1_w = (ffn1_w13[0].astype(BF16), ffn1_w2[0].astype(BF16))
    for l in range(DEPTH):
        lam_init = 0.8 - 0.6 * math.exp(-0.3 * l)
        last = l == DEPTH - 1
        mod_ctx, mod_lat = mod[l, 0:1], mod[l, 1:1 + n_lat_b]

        def trunk(x, m, tpg, n_batch, n_tok, rope, mla_cache, diff_cache, gla_init, is_ctx, ffn2_w):
            if is_ctx:
                cast = ((ffn2_w13, ffn2_w2), l)
            else:
                cast = None if last else ((ffn1_w13, ffn1_w2), l + 1)
            x, converted = _ffn(x, m, tpg, lw["norm_ffn1"], ffn1_w[0], ffn1_w[1], l, mod_row=0, cast=cast)
            if is_ctx:
                ffn2_w = converted
            p = _project(x, m, tpg, lw, lw["w_in_t"], l, rope, seq_len=n_tok if is_ctx else 0,
                         carried=new_cache if is_ctx else None)
            oa = _mla(p, lw, n_batch, n_tok, mla_cache, l)
            ob = _diff(p, lw, n_batch, n_tok, diff_cache, l, lam_init)
            gla_out = _gla(p, n_batch, n_tok, l, init=gla_init, emit_state=is_ctx,
                           carried=new_states if is_ctx else None)
            x = _merge_ffn(x, m, tpg, oa, ob, gla_out[0], p["rc"], lw, ffn2_w[0], ffn2_w[1], consts, fin, l,
                           lam_init, last)
            return x, p, gla_out, converted

        xp, p, gla_out, ffn2_w = trunk(xp, mod_ctx, n_ctx_b * n_ctx_t, n_ctx_b, n_ctx_t, None, None, None,
                                       None, True, None)
        new_cache = p["cache"]
        new_states = tuple(gla_out[1:])
        new_kr.append(p["kr"][:, ROPE_OFF:ROPE_OFF + DR_A].reshape(n_ctx_b, n_ctx_t, DR_A))

        xs, _, _, next_ffn1_w = trunk(xs, mod_lat, n_lat_t, n_lat_b, n_lat_t, rope_tabs,
                                      (cache_mla_ckv, cache_kr), (cache_dk, cache_dv),
                                      (state_gla_fwd, state_gla_bwd), False, ffn2_w)
        ffn1_w = next_ffn1_w

    new_ckv, new_dkt, new_dvt = new_cache

    def token_major(t, dh):
        return jnp.transpose(t.reshape(n_ctx_b, DEPTH, H_B, dh, n_ctx_t), (0, 1, 4, 2, 3))

    return (xp.reshape(x_prompt.shape), xs.reshape(x_sample.shape), new_ckv, jnp.stack(new_kr, axis=1),
            token_major(new_dkt, 2 * DH_B), token_major(new_dvt, DV_B), new_states[0], new_states[1])
```
